```python
import math
import jax, jax.numpy as jnp
from jax import lax
import numpy as np

D_MODEL = 1024
BATCH = 8
SEQ = 2048
DEPTH = 2

HEAD_DIM = 64
NA_HEADS = 8
NA_WIDTH = NA_HEADS * HEAD_DIM
GRID_W = 64
NA_KH_MAX = 8
NA_KW = 16
NA_QCOLS = 16
NA_KCOLS = NA_QCOLS + NA_KW
SW_HEADS = 8
SW_KV_HEADS = 2
SW_GROUP = SW_HEADS // SW_KV_HEADS
SW_WIDTH = SW_HEADS * HEAD_DIM
SW_KV_WIDTH = SW_KV_HEADS * HEAD_DIM
SW_WINDOW = 128
SW_BLOCK = 128
ROT_DIM = HEAD_DIM // 4
ROPE_THETA = 500000.0
OFF_QNA = NA_WIDTH
OFF_KNA = 2 * NA_WIDTH
OFF_VNA = 3 * NA_WIDTH
OFF_QSW = OFF_VNA + SW_WIDTH
OFF_KSW = OFF_QSW + SW_KV_WIDTH
OFF_VSW = OFF_KSW + SW_KV_WIDTH
N_BRANCHES = 2
PROJ_COLS = OFF_VSW + N_BRANCHES * D_MODEL
D_FF_DENSE = 2816
N_EXPERTS = 8
TOP_K = 2
D_FF_EXPERT = 3584
N_DENSE_LAYERS = (DEPTH + 1) // 2
N_MOE_LAYERS = DEPTH // 2
DEEPNORM_ALPHA = (2 * DEPTH) ** 0.25
DEEPNORM_BETA = (8 * DEPTH) ** -0.25
LN_EPS = 1e-5
NEG_INF = -1e30

kernel_name = "hybrid_natten_swa_moe_deepnorm"


def layer_norm(x, g, b):
    xf = x.astype(jnp.float32)
    mu = jnp.mean(xf, axis=-1, keepdims=True)
    var = jnp.mean(jnp.square(xf - mu), axis=-1, keepdims=True)
    return ((xf - mu) * lax.rsqrt(var + LN_EPS)).astype(x.dtype) * g + b


def partial_rotary(x, pos):
    half = ROT_DIM // 2
    inv_freq = 1.0 / (ROPE_THETA ** (jnp.arange(0, ROT_DIM, 2, dtype=jnp.float32) / ROT_DIM))
    ang = pos.astype(jnp.float32)[:, None] * inv_freq[None, :]
    cos = jnp.cos(ang)[None, :, None, :]
    sin = jnp.sin(ang)[None, :, None, :]
    x1 = x[..., :half].astype(jnp.float32)
    x2 = x[..., half:ROT_DIM].astype(jnp.float32)
    rot = jnp.concatenate([x1 * cos - x2 * sin, x2 * cos + x1 * sin], axis=-1).astype(x.dtype)
    return jnp.concatenate([rot, x[..., ROT_DIM:]], axis=-1)


def neighbourhood_attention(q, k, v, rpb):
    B, S, H, D = q.shape
    rows = S // GRID_W
    kh = min(NA_KH_MAX, rows)
    n_cb = GRID_W // NA_QCOLS
    r = np.arange(rows)
    row_start = np.clip(r - kh // 2, 0, rows - kh)
    key_rows = row_start[:, None] + np.arange(kh)[None, :]
    cb_start = np.clip(np.arange(n_cb) * NA_QCOLS - NA_KW // 2, 0, GRID_W - NA_KCOLS)
    key_cols = cb_start[:, None] + np.arange(NA_KCOLS)[None, :]
    key_tok = (key_rows[:, None, :, None] * GRID_W + key_cols[None, :, None, :])
    key_tok = key_tok.reshape(rows, n_cb, kh * NA_KCOLS)
    kg = jnp.take(k, key_tok, axis=1)
    vg = jnp.take(v, key_tok, axis=1)
    qb = q.reshape(B, rows, n_cb, NA_QCOLS, H, D)
    s = jnp.einsum('brjqhd,brjkhd->brjhqk', qb, kg,
                   preferred_element_type=jnp.float32) * (D ** -0.5)
    qcol = np.arange(n_cb)[:, None] * NA_QCOLS + np.arange(NA_QCOLS)[None, :]
    qcs = np.clip(qcol - NA_KW // 2, 0, GRID_W - NA_KW)
    kcol = key_cols[:, None, :]
    valid = (kcol >= qcs[..., None]) & (kcol < qcs[..., None] + NA_KW)
    dr = key_rows - r[:, None] + (NA_KH_MAX - 1)
    dc = np.clip(kcol - qcol[..., None] + (NA_KW - 1), 0, 2 * NA_KW - 2)
    dri = dr[:, None, None, :, None]
    dci = dc[None, :, :, None, :]
    bias = rpb[:, dri, dci].astype(jnp.float32)
    mask = np.broadcast_to(valid[None, :, :, None, :], (rows, n_cb, NA_QCOLS, kh, NA_KCOLS))
    bias = jnp.where(mask[None], bias, NEG_INF)
    bias = bias.reshape(H, rows, n_cb, NA_QCOLS, kh * NA_KCOLS).transpose(1, 2, 0, 3, 4)
    p = jax.nn.softmax(s + bias[None], axis=-1)
    o = jnp.einsum('brjhqk,brjkhd->brjqhd', p.astype(v.dtype), vg)
    return o.reshape(B, S, H * D)


def sliding_window_attention(q, k, v, sink):
    B, S, HQ, D = q.shape
    nb = S // SW_BLOCK
    pad = ((0, 0), (SW_BLOCK, SW_BLOCK), (0, 0), (0, 0))
    kp = jnp.pad(k, pad).reshape(B, nb + 2, SW_BLOCK, SW_KV_HEADS, D)
    vp = jnp.pad(v, pad).reshape(B, nb + 2, SW_BLOCK, SW_KV_HEADS, D)
    kb = jnp.concatenate([kp[:, :-2], kp[:, 1:-1], kp[:, 2:]], axis=2)
    vb = jnp.concatenate([vp[:, :-2], vp[:, 1:-1], vp[:, 2:]], axis=2)
    qb = q.reshape(B, nb, SW_BLOCK, SW_KV_HEADS, SW_GROUP, D)
    s = jnp.einsum('bnqkgd,bnskd->bnkgqs', qb, kb,
                   preferred_element_type=jnp.float32) * (D ** -0.5)
    qpos = np.arange(SW_BLOCK)[:, None] + SW_BLOCK
    kpos = np.arange(3 * SW_BLOCK)[None, :]
    band = np.abs(qpos - kpos) <= SW_WINDOW
    kabs = np.arange(nb)[:, None, None] * SW_BLOCK + kpos[None] - SW_BLOCK
    valid = band[None] & (kabs >= 0) & (kabs < S)
    s = jnp.where(valid[None, :, None, None], s, NEG_INF)
    sink_logit = sink.astype(jnp.float32).reshape(SW_KV_HEADS, SW_GROUP)[None, None, :, :, None, None]
    m = jnp.maximum(jnp.max(s, axis=-1, keepdims=True), sink_logit)
    p = jnp.exp(s - m)
    denom = jnp.sum(p, axis=-1, keepdims=True) + jnp.exp(sink_logit - m)
    o = jnp.einsum('bnkgqs,bnskd->bnqkgd', (p / denom).astype(v.dtype), vb)
    return o.reshape(B, S, HQ * D)


def mixer_block(x, w_in, b_gate, rpb, sink, w_branch_na, w_branch_sw, w_out):
    B, S, _ = x.shape
    proj = x @ w_in
    q_na, k_na, v_na, q_sw, k_sw, v_sw, gate_logits = jnp.split(
        proj, [OFF_QNA, OFF_KNA, OFF_VNA, OFF_QSW, OFF_KSW, OFF_VSW], axis=-1)
    hs = lambda t, h: t.reshape(B, S, h, HEAD_DIM)
    y_na = neighbourhood_attention(hs(q_na, NA_HEADS), hs(k_na, NA_HEADS), hs(v_na, NA_HEADS), rpb)
    pos = jnp.arange(S, dtype=jnp.int32)
    qs = partial_rotary(hs(q_sw, SW_HEADS), pos)
    ks = partial_rotary(hs(k_sw, SW_KV_HEADS), pos)
    y_sw = sliding_window_attention(qs, ks, hs(v_sw, SW_KV_HEADS), sink)
    y_na = y_na @ w_branch_na
    y_sw = y_sw @ w_branch_sw
    gates = jax.nn.sigmoid((gate_logits + b_gate).astype(jnp.float32)).astype(x.dtype)
    g_na, g_sw = jnp.split(gates, N_BRANCHES, axis=-1)
    return (g_na * y_na + g_sw * y_sw) @ w_out


def swiglu(x, w_gate, w_up, w_down):
    return (jax.nn.silu(x @ w_gate) * (x @ w_up)) @ w_down


def moe_swiglu(x, w_router, w_gate, w_up, w_down):
    B, S, D = x.shape
    xt = x.reshape(B * S, D)
    logits = (xt @ w_router).astype(jnp.float32)
    top_vals, top_idx = lax.top_k(logits, TOP_K)
    top_w = jax.nn.softmax(top_vals, axis=-1)
    combine = jnp.sum(jax.nn.one_hot(top_idx, N_EXPERTS, dtype=jnp.float32) * top_w[..., None], axis=1)
    combine = combine.astype(x.dtype)
    y = jnp.zeros_like(xt)
    for e in range(N_EXPERTS):
        y = y + combine[:, e:e + 1] * swiglu(xt, w_gate[e], w_up[e], w_down[e])
    return y.reshape(B, S, D)


def setup_inputs(seed: int = 0) -> dict:
    key = jax.random.key(seed)
    ks = jax.random.split(key, 24)
    f32 = jnp.float32
    nrm = lambda k, shape, scale: jax.random.normal(k, shape, f32) * scale
    beta = DEEPNORM_BETA
    col_scale = np.ones((PROJ_COLS,), np.float32)
    col_scale[OFF_KNA:OFF_VNA] = beta
    col_scale[OFF_KSW:OFF_VSW] = beta
    w_in = nrm(ks[1], (DEPTH, D_MODEL, PROJ_COLS), D_MODEL ** -0.5) * jnp.asarray(col_scale)
    return {
        "x": nrm(ks[0], (BATCH, SEQ, D_MODEL), 1.0),
        "emb_ln_g": 1.0 + nrm(ks[2], (D_MODEL,), 0.02),
        "emb_ln_b": nrm(ks[3], (D_MODEL,), 0.02),
        "w_in": w_in,
        "b_gate": nrm(ks[4], (DEPTH, N_BRANCHES * D_MODEL), 0.1),
        "na_rpb": nrm(ks[5], (DEPTH, NA_HEADS, 2 * NA_KH_MAX - 1, 2 * NA_KW - 1), 0.2),
        "sw_sink": nrm(ks[6], (DEPTH, SW_HEADS), 0.5),
        "w_branch_na": nrm(ks[7], (DEPTH, NA_WIDTH, D_MODEL), beta * NA_WIDTH ** -0.5),
        "w_branch_sw": nrm(ks[8], (DEPTH, SW_WIDTH, D_MODEL), beta * SW_WIDTH ** -0.5),
        "w_out": nrm(ks[9], (DEPTH, D_MODEL, D_MODEL), beta * D_MODEL ** -0.5),
        "ln1_g": 1.0 + nrm(ks[10], (DEPTH, D_MODEL), 0.02),
        "ln1_b": nrm(ks[11], (DEPTH, D_MODEL), 0.02),
        "ffn_w_gate": nrm(ks[12], (N_DENSE_LAYERS, D_MODEL, D_FF_DENSE), beta * D_MODEL ** -0.5),
        "ffn_w_up": nrm(ks[13], (N_DENSE_LAYERS, D_MODEL, D_FF_DENSE), beta * D_MODEL ** -0.5),
        "ffn_w_down": nrm(ks[14], (N_DENSE_LAYERS, D_FF_DENSE, D_MODEL), beta * D_FF_DENSE ** -0.5),
        "moe_router": nrm(ks[15], (N_MOE_LAYERS, D_MODEL, N_EXPERTS), D_MODEL ** -0.5),
        "moe_w_gate": nrm(ks[16], (N_MOE_LAYERS, N_EXPERTS, D_MODEL, D_FF_EXPERT), beta * D_MODEL ** -0.5),
        "moe_w_up": nrm(ks[17], (N_MOE_LAYERS, N_EXPERTS, D_MODEL, D_FF_EXPERT), beta * D_MODEL ** -0.5),
        "moe_w_down": nrm(ks[18], (N_MOE_LAYERS, N_EXPERTS, D_FF_EXPERT, D_MODEL), beta * D_FF_EXPERT ** -0.5),
        "ln2_g": 1.0 + nrm(ks[19], (DEPTH, D_MODEL), 0.02),
        "ln2_b": nrm(ks[20], (DEPTH, D_MODEL), 0.02),
    }


def reference(x, emb_ln_g, emb_ln_b, w_in, b_gate, na_rpb, sw_sink, w_branch_na, w_branch_sw,
              w_out, ln1_g, ln1_b, ffn_w_gate, ffn_w_up, ffn_w_down, moe_router, moe_w_gate,
              moe_w_up, moe_w_down, ln2_g, ln2_b):
    x = layer_norm(x, emb_ln_g, emb_ln_b)
    for layer in range(DEPTH):
        m = mixer_block(x, w_in[layer], b_gate[layer], na_rpb[layer], sw_sink[layer],
                        w_branch_na[layer], w_branch_sw[layer], w_out[layer])
        x = layer_norm(DEEPNORM_ALPHA * x + m, ln1_g[layer], ln1_b[layer])
        i = layer // 2
        if layer % 2 == 0:
            f = swiglu(x, ffn_w_gate[i], ffn_w_up[i], ffn_w_down[i])
        else:
            f = moe_swiglu(x, moe_router[i], moe_w_gate[i], moe_w_up[i], moe_w_down[i])
        x = layer_norm(DEEPNORM_ALPHA * x + f, ln2_g[layer], ln2_b[layer])
    return x
```

```python
import functools

import numpy as np
import jax
import jax.numpy as jnp
from jax import lax
from jax.experimental import pallas as pl
from jax.experimental.pallas import tpu as pltpu

F32 = jnp.float32
BF16 = jnp.bfloat16

D_MODEL = 1024
SEQ = 2048
DEPTH = 2
HEAD_DIM = 64
NA_HEADS = 8
NA_WIDTH = NA_HEADS * HEAD_DIM
GRID_W = 64
GRID_ROWS = SEQ // GRID_W
NA_KH = 8
NA_KW = 16
SW_HEADS = 8
SW_KV_HEADS = 2
SW_GROUP = SW_HEADS // SW_KV_HEADS
SW_WIDTH = SW_HEADS * HEAD_DIM
SW_KV_WIDTH = SW_KV_HEADS * HEAD_DIM
SW_WINDOW = 128
SW_BLOCK = 128
ROT_DIM = HEAD_DIM // 4
ROPE_THETA = 500000.0
OFF_QNA = NA_WIDTH
OFF_KNA = 2 * NA_WIDTH
OFF_VNA = 3 * NA_WIDTH
OFF_QSW = OFF_VNA + SW_WIDTH
OFF_KSW = OFF_QSW + SW_KV_WIDTH
OFF_VSW = OFF_KSW + SW_KV_WIDTH
PROJ_COLS = OFF_VSW + 2 * D_MODEL
N_EXPERTS = 8
DEEPNORM_ALPHA = (2 * DEPTH) ** 0.25
LN_EPS = 1e-5
NEG_INF = -1e30
QK_SCALE = HEAD_DIM ** -0.5

LANES = 128
MXU_DIM = 256
V7X_VMEM_BYTES = 64 * 1024 * 1024
VMEM_LIMIT = V7X_VMEM_BYTES * 7 // 8

ROW_TILE = 512
FFN_ROW_TILE = 1024
MOE_FF_TILE = 512
DENSE_FF_TILE = 1408


def _layer_norm(z, g, b):
    mu = jnp.mean(z, axis=-1, keepdims=True)
    d = z - mu
    var = jnp.mean(d * d, axis=-1, keepdims=True)
    return d * lax.rsqrt(var + LN_EPS) * g + b


def _sigmoid(z):
    return 1.0 / (1.0 + jnp.exp(-z))


def _params(*sem):
    return pltpu.CompilerParams(dimension_semantics=sem, vmem_limit_bytes=VMEM_LIMIT)


def _ln_kernel(x_ref, g_ref, b_ref, o_ref):
    o_ref[...] = _layer_norm(x_ref[...], g_ref[...], b_ref[...])


def _ln(x, g, b):
    n = x.shape[0]
    row = pl.BlockSpec((ROW_TILE, D_MODEL), lambda i: (i, 0))
    vec = pl.BlockSpec((1, D_MODEL), lambda i: (0, 0))
    return pl.pallas_call(
        _ln_kernel, grid=(n // ROW_TILE,), in_specs=[row, vec, vec], out_specs=row,
        out_shape=jax.ShapeDtypeStruct((n, D_MODEL), F32), compiler_params=_params("parallel"),
        name="embed_ln",
    )(x, g.reshape(1, -1), b.reshape(1, -1))


def _rotary_tables():
    half = ROT_DIM // 2
    inv_freq = 1.0 / (ROPE_THETA ** (jnp.arange(0, ROT_DIM, 2, dtype=F32) / ROT_DIM))
    ang = jnp.arange(SEQ, dtype=jnp.int32).astype(F32)[:, None] * inv_freq[None, :]
    cos, sin = jnp.cos(ang), jnp.sin(ang)
    ones = jnp.ones((SEQ, HEAD_DIM - ROT_DIM), F32)
    zeros = jnp.zeros((SEQ, HEAD_DIM - ROT_DIM), F32)
    zh = jnp.zeros((SEQ, half), F32)
    cos_h = jnp.concatenate([cos, cos, ones], axis=1)
    sa_h = jnp.concatenate([-sin, zh, zeros], axis=1)
    sb_h = jnp.concatenate([zh, sin, zeros], axis=1)
    two = lambda t: jnp.concatenate([t, t], axis=1)
    return two(cos_h), two(sa_h), two(sb_h)


def _proj_kernel(x_ref, w_ref, bg_ref, cos_ref, sa_ref, sb_ref,
                 qna_ref, kna_ref, vna_ref, qsw_ref, ksw_ref, vsw_ref, gate_ref):
    xb = x_ref[...].astype(BF16)

    def mm(lo, hi):
        return jnp.dot(xb, w_ref[:, lo:hi], preferred_element_type=F32)

    qna_ref[...] = (mm(0, OFF_QNA) * QK_SCALE).astype(BF16)
    kna_ref[...] = mm(OFF_QNA, OFF_KNA).astype(BF16)
    vna_ref[...] = mm(OFF_KNA, OFF_VNA).astype(BF16)

    cos, sa, sb = cos_ref[...], sa_ref[...], sb_ref[...]
    half = ROT_DIM // 2

    def rot(t):
        return t * cos + pltpu.roll(t, LANES - half, 1) * sa + pltpu.roll(t, half, 1) * sb

    q = mm(OFF_VNA, OFF_QSW)
    for c in range(SW_WIDTH // LANES):
        sl = slice(c * LANES, (c + 1) * LANES)
        qsw_ref[:, sl] = (rot(q[:, sl]) * QK_SCALE).astype(BF16)

    kv = mm(OFF_QSW, OFF_VSW)
    k = rot(kv[:, :SW_KV_WIDTH])
    v = kv[:, SW_KV_WIDTH:]
    first = lax.broadcasted_iota(jnp.int32, k.shape, 1) < HEAD_DIM

    def dup(t):
        r = pltpu.roll(t, HEAD_DIM, 1)
        return jnp.concatenate([jnp.where(first, t, r), jnp.where(first, r, t)], axis=1)

    ksw_ref[...] = dup(k).astype(BF16)
    vsw_ref[...] = dup(v).astype(BF16)

    gw = 512
    for c in range(2 * D_MODEL // gw):
        z = mm(OFF_VSW + c * gw, OFF_VSW + (c + 1) * gw) + bg_ref[:, c * gw:(c + 1) * gw]
        gate_ref[:, c * gw:(c + 1) * gw] = _sigmoid(z).astype(BF16)


def _proj(x, w_bf, b_gate, tables):
    n = x.shape[0]
    tiles_per_seq = SEQ // ROW_TILE
    row = lambda w: pl.BlockSpec((ROW_TILE, w), lambda i: (i, 0))
    full = lambda a: pl.BlockSpec(a.shape, lambda i: (0,) * a.ndim)
    tab = pl.BlockSpec((ROW_TILE, LANES), lambda i: (i % tiles_per_seq, 0))
    bg = b_gate.reshape(1, -1)
    widths = (NA_WIDTH, NA_WIDTH, NA_WIDTH, SW_WIDTH, 2 * SW_KV_WIDTH, 2 * SW_KV_WIDTH, 2 * D_MODEL)
    return pl.pallas_call(
        _proj_kernel, grid=(n // ROW_TILE,),
        in_specs=[row(D_MODEL), full(w_bf), full(bg), tab, tab, tab],
        out_specs=[row(w) for w in widths],
        out_shape=[jax.ShapeDtypeStruct((n, w), BF16) for w in widths],
        compiler_params=_params("parallel"), name="in_proj",
    )(x, w_bf, bg, *tables)


def _na_bias_tables(rpb):
    c = np.arange(GRID_W)
    qcs = np.clip(c - NA_KW // 2, 0, GRID_W - NA_KW)
    valid = (c[None, :] >= qcs[:, None]) & (c[None, :] < qcs[:, None] + NA_KW)
    dc = np.clip(c[None, :] - c[:, None] + (NA_KW - 1), 0, 2 * NA_KW - 2)
    dr = np.arange(NA_KH)[:, None] + np.arange(NA_KH)[None, :]
    b = rpb.astype(F32)[:, dr[:, None, :, None], dc[None, :, None, :]]
    b = jnp.where(valid[None, None, :, None, :], b, NEG_INF)
    b = b.transpose(1, 0, 2, 3, 4)
    return b.reshape(NA_KH, 2, 4 * GRID_W, NA_KH * GRID_W)


def _na_kernel(q_ref, k_ref, v_ref, bias_ref, o_ref):
    lane = lax.broadcasted_iota(jnp.int32, (GRID_W, MXU_DIM), 1)
    head_mask = [(lane >= h * HEAD_DIM) & (lane < (h + 1) * HEAD_DIM) for h in range(4)]
    nk = NA_KH * GRID_W

    def row(r, carry):
        rs = jnp.clip(r - NA_KH // 2, 0, GRID_ROWS - NA_KH)
        variant = rs - r + (NA_KH - 1)
        q = q_ref[pl.ds(pl.multiple_of(r * GRID_W, GRID_W), GRID_W), :]
        k0 = pl.multiple_of(rs * GRID_W, GRID_W)
        outs = []
        for g in range(2):
            gs = slice(g * MXU_DIM, (g + 1) * MXU_DIM)
            qg = q[:, gs]
            lhs = jnp.concatenate([jnp.where(m, qg, jnp.zeros_like(qg)) for m in head_mask], axis=0)
            s = lax.dot_general(lhs, k_ref[pl.ds(k0, nk), gs], (((1,), (1,)), ((), ())),
                                preferred_element_type=F32)
            s = s + bias_ref[variant, g]
            p = jnp.exp(s - jnp.max(s, axis=-1, keepdims=True))
            l = jnp.sum(p, axis=-1, keepdims=True)
            o = jnp.dot(p.astype(BF16), v_ref[pl.ds(k0, nk), gs], preferred_element_type=F32) / l
            og = jnp.zeros((GRID_W, MXU_DIM), F32)
            for h in range(4):
                og = og + jnp.where(head_mask[h], o[h * GRID_W:(h + 1) * GRID_W], 0.0)
            outs.append(og)
        o_ref[pl.ds(pl.multiple_of(r * GRID_W, GRID_W), GRID_W), :] = (
            jnp.concatenate(outs, axis=1).astype(BF16))
        return carry

    lax.fori_loop(0, GRID_ROWS, row, 0)


def _na_attention(q, k, v, bias):
    b = q.shape[0]
    seq = pl.BlockSpec((None, SEQ, NA_WIDTH), lambda i: (i, 0, 0))
    return pl.pallas_call(
        _na_kernel, grid=(b,),
        in_specs=[seq, seq, seq, pl.BlockSpec(bias.shape, lambda i: (0, 0, 0, 0))],
        out_specs=seq, out_shape=jax.ShapeDtypeStruct((b, SEQ, NA_WIDTH), BF16),
        compiler_params=_params("parallel"), name="na_attention",
    )(q, k, v, bias)


def _sw_kernel(sink_ref, q_ref, k_ref, v_ref, o_ref):
    nb = SEQ // SW_BLOCK
    nkeys = 3 * SW_BLOCK
    iq = lax.broadcasted_iota(jnp.int32, (SW_BLOCK, nkeys), 0)
    jk = lax.broadcasted_iota(jnp.int32, (SW_BLOCK, nkeys), 1)
    band = (jk >= iq) & (jk <= iq + 2 * SW_WINDOW)
    first = lax.broadcasted_iota(jnp.int32, (SW_BLOCK, LANES), 1) < HEAD_DIM

    def block(n, carry):
        kabs = jk + (n - 1) * SW_BLOCK
        mask1 = jnp.where(band & (kabs >= 0) & (kabs < SEQ), 0.0, NEG_INF).astype(F32)
        mask = jnp.concatenate([mask1] * SW_GROUP, axis=0)
        starts = [pl.multiple_of(jnp.clip(n + d, 0, nb - 1) * SW_BLOCK, SW_BLOCK) for d in (-1, 0, 1)]
        q = q_ref[pl.ds(pl.multiple_of(n * SW_BLOCK, SW_BLOCK), SW_BLOCK), :]
        for kvh in range(SW_KV_HEADS):
            ks = slice(kvh * LANES, (kvh + 1) * LANES)
            kk = jnp.concatenate([k_ref[pl.ds(s, SW_BLOCK), ks] for s in starts], axis=0)
            vv = jnp.concatenate([v_ref[pl.ds(s, SW_BLOCK), ks] for s in starts], axis=0)
            parts, sinks = [], []
            for j in range(SW_GROUP):
                h = kvh * SW_GROUP + j
                qc = q[:, (h // 2) * LANES:(h // 2 + 1) * LANES]
                keep = first if h % 2 == 0 else jnp.logical_not(first)
                parts.append(jnp.where(keep, qc, jnp.zeros_like(qc)))
                sinks.append(jnp.full((SW_BLOCK, 1), sink_ref[h], F32))
            lhs = jnp.concatenate(parts, axis=0)
            sink = jnp.concatenate(sinks, axis=0)
            s = lax.dot_general(lhs, kk, (((1,), (1,)), ((), ())), preferred_element_type=F32) + mask
            m = jnp.maximum(jnp.max(s, axis=-1, keepdims=True), sink)
            p = jnp.exp(s - m)
            denom = jnp.sum(p, axis=-1, keepdims=True) + jnp.exp(sink - m)
            o = jnp.dot(p.astype(BF16), vv, preferred_element_type=F32) / denom
            for c in range(SW_GROUP // 2):
                even = o[(2 * c) * SW_BLOCK:(2 * c + 1) * SW_BLOCK]
                odd = o[(2 * c + 1) * SW_BLOCK:(2 * c + 2) * SW_BLOCK]
                col = (kvh * (SW_GROUP // 2) + c) * LANES
                o_ref[pl.ds(pl.multiple_of(n * SW_BLOCK, SW_BLOCK), SW_BLOCK), col:col + LANES] = (
                    jnp.where(first, even, odd).astype(BF16))
        return carry

    lax.fori_loop(0, nb, block, 0)


def _sw_attention(q, k2, v2, sink):
    b = q.shape[0]
    qs = pl.BlockSpec((None, SEQ, SW_WIDTH), lambda i: (i, 0, 0))
    kvs = pl.BlockSpec((None, SEQ, 2 * SW_KV_WIDTH), lambda i: (i, 0, 0))
    return pl.pallas_call(
        _sw_kernel, grid=(b,),
        in_specs=[pl.BlockSpec(memory_space=pltpu.SMEM), qs, kvs, kvs],
        out_specs=qs, out_shape=jax.ShapeDtypeStruct((b, SEQ, SW_WIDTH), BF16),
        compiler_params=_params("parallel"), name="sw_attention",
    )(sink.astype(F32), q, k2, v2)


def _merge_kernel(x_ref, yna_ref, ysw_ref, gate_ref, wna_ref, wsw_ref, wout_ref, g_ref, b_ref, o_ref):
    a = jnp.dot(yna_ref[...], wna_ref[...], preferred_element_type=F32)
    s = jnp.dot(ysw_ref[...], wsw_ref[...], preferred_element_type=F32)
    m = gate_ref[:, :D_MODEL].astype(F32) * a + gate_ref[:, D_MODEL:].astype(F32) * s
    z = DEEPNORM_ALPHA * x_ref[...] + jnp.dot(m.astype(BF16), wout_ref[...], preferred_element_type=F32)
    o_ref[...] = _layer_norm(z, g_ref[...], b_ref[...])


def _merge(x, y_na, y_sw, gates, w_na, w_sw, w_out, g, b):
    n = x.shape[0]
    row = lambda w: pl.BlockSpec((ROW_TILE, w), lambda i: (i, 0))
    full = lambda a: pl.BlockSpec(a.shape, lambda i: (0,) * a.ndim)
    g2, b2 = g.reshape(1, -1), b.reshape(1, -1)
    return pl.pallas_call(
        _merge_kernel, grid=(n // ROW_TILE,),
        in_specs=[row(D_MODEL), row(NA_WIDTH), row(SW_WIDTH), row(2 * D_MODEL),
                  full(w_na), full(w_sw), full(w_out), full(g2), full(b2)],
        out_specs=row(D_MODEL), out_shape=jax.ShapeDtypeStruct((n, D_MODEL), F32),
        compiler_params=_params("parallel"), name="merge_ln1",
    )(x, y_na, y_sw, gates, w_na, w_sw, w_out, g2, b2)


def _router_kernel(x_ref, w_ref, cw_ref):
    x, w = x_ref[...], w_ref[...]
    xh, wh = x.astype(BF16), w.astype(BF16)
    xl = (x - xh.astype(F32)).astype(BF16)
    wl = (w - wh.astype(F32)).astype(BF16)
    dot = functools.partial(jnp.dot, preferred_element_type=F32)
    logits = dot(xh, wh) + dot(xl, wh) + dot(xh, wl)
    lane = lax.broadcasted_iota(jnp.int32, logits.shape, 1).astype(F32)
    lg = jnp.where(lane < N_EXPERTS, logits, -jnp.inf)
    m1 = jnp.max(lg, axis=-1, keepdims=True)
    i1 = jnp.min(jnp.where(lg == m1, lane, float(LANES)), axis=-1, keepdims=True)
    lg2 = jnp.where(lane == i1, -jnp.inf, lg)
    m2 = jnp.max(lg2, axis=-1, keepdims=True)
    i2 = jnp.min(jnp.where(lg2 == m2, lane, float(LANES)), axis=-1, keepdims=True)
    e2 = jnp.exp(m2 - m1)
    denom = 1.0 + e2
    cw_ref[...] = jnp.where(lane == i1, 1.0 / denom, 0.0) + jnp.where(lane == i2, e2 / denom, 0.0)


def _router(x, w_router):
    n = x.shape[0]
    w = jnp.zeros((D_MODEL, LANES), F32).at[:, :N_EXPERTS].set(w_router)
    return pl.pallas_call(
        _router_kernel, grid=(n // ROW_TILE,),
        in_specs=[pl.BlockSpec((ROW_TILE, D_MODEL), lambda i: (i, 0)),
                  pl.BlockSpec((D_MODEL, LANES), lambda i: (0, 0))],
        out_specs=pl.BlockSpec((ROW_TILE, LANES), lambda i: (i, 0)),
        out_shape=jax.ShapeDtypeStruct((n, LANES), F32),
        compiler_params=_params("parallel"), name="router_top2",
    )(x, w)


def _ffn_kernel(*refs, routed):
    if routed:
        x_ref, cw_ref, wg_ref, wu_ref, wd_ref, g_ref, b_ref, o_ref, xb_ref, acc_ref = refs
    else:
        x_ref, wg_ref, wu_ref, wd_ref, g_ref, b_ref, o_ref, xb_ref, acc_ref = refs
    e, j = pl.program_id(1), pl.program_id(2)

    @pl.when((e == 0) & (j == 0))
    def _():
        xb_ref[...] = x_ref[...].astype(BF16)
        acc_ref[...] = jnp.zeros_like(acc_ref)

    xb = xb_ref[...]
    hg = jnp.dot(xb, wg_ref[...].astype(BF16), preferred_element_type=F32)
    hu = jnp.dot(xb, wu_ref[...].astype(BF16), preferred_element_type=F32)
    h = hg * _sigmoid(hg) * hu
    part = jnp.dot(h.astype(BF16), wd_ref[...].astype(BF16), preferred_element_type=F32)
    if routed:
        lane = lax.broadcasted_iota(jnp.int32, cw_ref.shape, 1)
        part = part * jnp.sum(jnp.where(lane == e, cw_ref[...], 0.0), axis=-1, keepdims=True)
    acc_ref[...] += part

    @pl.when((e == pl.num_programs(1) - 1) & (j == pl.num_programs(2) - 1))
    def _():
        o_ref[...] = _layer_norm(DEEPNORM_ALPHA * x_ref[...] + acc_ref[...], g_ref[...], b_ref[...])


def _ffn(x, cw, w_gate, w_up, w_down, g, b, ff_tile):
    n = x.shape[0]
    n_exp, _, d_ff = w_gate.shape
    tm = FFN_ROW_TILE
    row = pl.BlockSpec((tm, D_MODEL), lambda i, e, j: (i, 0))
    vec = pl.BlockSpec((1, D_MODEL), lambda i, e, j: (0, 0))
    up = pl.BlockSpec((None, D_MODEL, ff_tile), lambda i, e, j: (e, 0, j))
    down = pl.BlockSpec((None, ff_tile, D_MODEL), lambda i, e, j: (e, j, 0))
    routed = cw is not None
    in_specs = [row] + ([pl.BlockSpec((tm, LANES), lambda i, e, j: (i, 0))] if routed else [])
    args = [x] + ([cw] if routed else [])
    return pl.pallas_call(
        functools.partial(_ffn_kernel, routed=routed),
        grid=(n // tm, n_exp, d_ff // ff_tile),
        in_specs=in_specs + [up, up, down, vec, vec], out_specs=row,
        out_shape=jax.ShapeDtypeStruct((n, D_MODEL), F32),
        scratch_shapes=[pltpu.VMEM((tm, D_MODEL), BF16), pltpu.VMEM((tm, D_MODEL), F32)],
        compiler_params=_params("parallel", "arbitrary", "arbitrary"),
        name="moe_swiglu_ln2" if routed else "swiglu_ln2",
    )(*args, w_gate, w_up, w_down, g.reshape(1, -1), b.reshape(1, -1))


def kernel(x, emb_ln_g, emb_ln_b, w_in, b_gate, na_rpb, sw_sink, w_branch_na, w_branch_sw, w_out,
           ln1_g, ln1_b, ffn_w_gate, ffn_w_up, ffn_w_down, moe_router, moe_w_gate, moe_w_up,
           moe_w_down, ln2_g, ln2_b):
    batch, seq, d = x.shape
    assert (seq, d) == (SEQ, D_MODEL)
    n = batch * seq
    tables = _rotary_tables()
    h = _ln(x.reshape(n, d), emb_ln_g, emb_ln_b)
    for layer in range(DEPTH):
        q_na, k_na, v_na, q_sw, k_sw, v_sw, gates = _proj(
            h, w_in[layer].astype(BF16), b_gate[layer], tables)
        seq3 = lambda t: t.reshape(batch, seq, t.shape[-1])
        y_na = _na_attention(seq3(q_na), seq3(k_na), seq3(v_na), _na_bias_tables(na_rpb[layer]))
        y_sw = _sw_attention(seq3(q_sw), seq3(k_sw), seq3(v_sw), sw_sink[layer])
        h = _merge(h, y_na.reshape(n, -1), y_sw.reshape(n, -1), gates,
                   w_branch_na[layer].astype(BF16), w_branch_sw[layer].astype(BF16),
                   w_out[layer].astype(BF16), ln1_g[layer], ln1_b[layer])
        i = layer // 2
        if layer % 2 == 0:
            h = _ffn(h, None, ffn_w_gate[i:i + 1].astype(BF16), ffn_w_up[i:i + 1].astype(BF16),
                     ffn_w_down[i:i + 1].astype(BF16), ln2_g[layer], ln2_b[layer], DENSE_FF_TILE)
        else:
            cw = _router(h, moe_router[i])
            h = _ffn(h, cw, moe_w_gate[i], moe_w_up[i], moe_w_down[i],
                     ln2_g[layer], ln2_b[layer], MOE_FF_TILE)
    return h.reshape(batch, seq, d)
```

```python
import functools

import numpy as np
import jax
import jax.numpy as jnp
from jax import lax
from jax.experimental import pallas as pl
from jax.experimental.pallas import tpu as pltpu

F32 = jnp.float32
BF16 = jnp.bfloat16

D_MODEL = 1024
SEQ = 2048
DEPTH = 2
HEAD_DIM = 64
NA_HEADS = 8
NA_WIDTH = NA_HEADS * HEAD_DIM
GRID_W = 64
GRID_ROWS = SEQ // GRID_W
NA_KH = 8
NA_KW = 16
SW_HEADS = 8
SW_KV_HEADS = 2
SW_GROUP = SW_HEADS // SW_KV_HEADS
SW_WIDTH = SW_HEADS * HEAD_DIM
SW_KV_WIDTH = SW_KV_HEADS * HEAD_DIM
SW_WINDOW = 128
SW_BLOCK = 128
ROT_DIM = HEAD_DIM // 4
ROPE_THETA = 500000.0
OFF_QNA = NA_WIDTH
OFF_KNA = 2 * NA_WIDTH
OFF_VNA = 3 * NA_WIDTH
OFF_QSW = OFF_VNA + SW_WIDTH
OFF_KSW = OFF_QSW + SW_KV_WIDTH
OFF_VSW = OFF_KSW + SW_KV_WIDTH
PROJ_COLS = OFF_VSW + 2 * D_MODEL
N_EXPERTS = 8
DEEPNORM_ALPHA = (2 * DEPTH) ** 0.25
LN_EPS = 1e-5
NEG_INF = -1e30
QK_SCALE = HEAD_DIM ** -0.5

LANES = 128
MXU_DIM = 256
V7X_VMEM_BYTES = 64 * 1024 * 1024
VMEM_LIMIT = V7X_VMEM_BYTES * 7 // 8

ROW_TILE = 512
FFN_ROW_TILE = 1024
MOE_FF_TILE = 512
DENSE_FF_TILE = 1408


def _layer_norm(z, g, b):
    mu = jnp.mean(z, axis=-1, keepdims=True)
    d = z - mu
    var = jnp.mean(d * d, axis=-1, keepdims=True)
    return d * lax.rsqrt(var + LN_EPS) * g + b


def _sigmoid(z):
    return 1.0 / (1.0 + jnp.exp(-z))


def _params(*sem):
    return pltpu.CompilerParams(dimension_semantics=sem, vmem_limit_bytes=VMEM_LIMIT)


def _ln_kernel(x_ref, g_ref, b_ref, o_ref):
    o_ref[...] = _layer_norm(x_ref[...], g_ref[...], b_ref[...])


def _ln(x, g, b):
    n = x.shape[0]
    row = pl.BlockSpec((ROW_TILE, D_MODEL), lambda i: (i, 0))
    vec = pl.BlockSpec((1, D_MODEL), lambda i: (0, 0))
    return pl.pallas_call(
        _ln_kernel, grid=(n // ROW_TILE,), in_specs=[row, vec, vec], out_specs=row,
        out_shape=jax.ShapeDtypeStruct((n, D_MODEL), F32), compiler_params=_params("parallel"),
        name="embed_ln",
    )(x, g.reshape(1, -1), b.reshape(1, -1))


def _rotary_tables():
    half = ROT_DIM // 2
    inv_freq = 1.0 / (ROPE_THETA ** (jnp.arange(0, ROT_DIM, 2, dtype=F32) / ROT_DIM))
    ang = jnp.arange(SEQ, dtype=jnp.int32).astype(F32)[:, None] * inv_freq[None, :]
    cos, sin = jnp.cos(ang), jnp.sin(ang)
    ones = jnp.ones((SEQ, HEAD_DIM - ROT_DIM), F32)
    zeros = jnp.zeros((SEQ, HEAD_DIM - ROT_DIM), F32)
    zh = jnp.zeros((SEQ, half), F32)
    cos_h = jnp.concatenate([cos, cos, ones], axis=1)
    sa_h = jnp.concatenate([-sin, zh, zeros], axis=1)
    sb_h = jnp.concatenate([zh, sin, zeros], axis=1)
    two = lambda t: jnp.concatenate([t, t], axis=1)
    return two(cos_h), two(sa_h), two(sb_h)


def _proj_kernel(x_ref, w_ref, bg_ref, cos_ref, sa_ref, sb_ref,
                 qna_ref, kna_ref, vna_ref, qsw_ref, ksw_ref, vsw_ref, gate_ref):
    xb = x_ref[...].astype(BF16)

    def mm(lo, hi):
        return jnp.dot(xb, w_ref[:, lo:hi], preferred_element_type=F32)

    qna_ref[...] = (mm(0, OFF_QNA) * QK_SCALE).astype(BF16)
    kna_ref[...] = mm(OFF_QNA, OFF_KNA).astype(BF16)
    vna_ref[...] = mm(OFF_KNA, OFF_VNA).astype(BF16)

    cos, sa, sb = cos_ref[...], sa_ref[...], sb_ref[...]
    half = ROT_DIM // 2

    def rot(t):
        return t * cos + pltpu.roll(t, LANES - half, 1) * sa + pltpu.roll(t, half, 1) * sb

    q = mm(OFF_VNA, OFF_QSW)
    for c in range(SW_WIDTH // LANES):
        sl = slice(c * LANES, (c + 1) * LANES)
        qsw_ref[:, sl] = (rot(q[:, sl]) * QK_SCALE).astype(BF16)

    kv = mm(OFF_QSW, OFF_VSW)
    k = rot(kv[:, :SW_KV_WIDTH])
    v = kv[:, SW_KV_WIDTH:]
    first = lax.broadcasted_iota(jnp.int32, k.shape, 1) < HEAD_DIM

    def dup(t):
        r = pltpu.roll(t, HEAD_DIM, 1)
        return jnp.concatenate([jnp.where(first, t, r), jnp.where(first, r, t)], axis=1)

    ksw_ref[...] = dup(k).astype(BF16)
    vsw_ref[...] = dup(v).astype(BF16)

    gw = 512
    for c in range(2 * D_MODEL // gw):
        z = mm(OFF_VSW + c * gw, OFF_VSW + (c + 1) * gw) + bg_ref[:, c * gw:(c + 1) * gw]
        gate_ref[:, c * gw:(c + 1) * gw] = _sigmoid(z).astype(BF16)


def _proj(x, w_bf, b_gate, tables):
    n = x.shape[0]
    tiles_per_seq = SEQ // ROW_TILE
    row = lambda w: pl.BlockSpec((ROW_TILE, w), lambda i: (i, 0))
    full = lambda a: pl.BlockSpec(a.shape, lambda i: (0,) * a.ndim)
    tab = pl.BlockSpec((ROW_TILE, LANES), lambda i: (i % tiles_per_seq, 0))
    bg = b_gate.reshape(1, -1)
    widths = (NA_WIDTH, NA_WIDTH, NA_WIDTH, SW_WIDTH, 2 * SW_KV_WIDTH, 2 * SW_KV_WIDTH, 2 * D_MODEL)
    return pl.pallas_call(
        _proj_kernel, grid=(n // ROW_TILE,),
        in_specs=[row(D_MODEL), full(w_bf), full(bg), tab, tab, tab],
        out_specs=[row(w) for w in widths],
        out_shape=[jax.ShapeDtypeStruct((n, w), BF16) for w in widths],
        compiler_params=_params("parallel"), name="in_proj",
    )(x, w_bf, bg, *tables)


def _na_bias_tables(rpb):
    c = np.arange(GRID_W)
    qcs = np.clip(c - NA_KW // 2, 0, GRID_W - NA_KW)
    valid = (c[None, :] >= qcs[:, None]) & (c[None, :] < qcs[:, None] + NA_KW)
    n_dr, n_dc = 2 * NA_KH - 1, 2 * NA_KW - 1
    span = 2 * GRID_W - 1
    lead = GRID_W - NA_KW
    ext = jnp.pad(rpb.astype(F32), ((0, 0), (0, 0), (lead, span + 1 - lead - n_dc)))
    flat = jnp.broadcast_to(ext[:, :, None, :], (NA_HEADS, n_dr, GRID_W, span + 1))
    flat = flat.reshape(NA_HEADS, n_dr, GRID_W * (span + 1))[:, :, :GRID_W * span]
    t = flat.reshape(NA_HEADS, n_dr, GRID_W, span)[:, :, :, GRID_W - 1:]
    t = jnp.where(valid[None, None], t, NEG_INF)
    b = jnp.stack([t[:, v:v + NA_KH] for v in range(NA_KH)], axis=0)
    b = b.transpose(0, 1, 3, 2, 4)
    return b.reshape(NA_KH, 2, 4 * GRID_W, NA_KH * GRID_W)


def _na_kernel(q_ref, k_ref, v_ref, bias_ref, o_ref):
    lane = lax.broadcasted_iota(jnp.int32, (GRID_W, MXU_DIM), 1)
    head_mask = [(lane >= h * HEAD_DIM) & (lane < (h + 1) * HEAD_DIM) for h in range(4)]
    nk = NA_KH * GRID_W

    def row(r, carry):
        rs = jnp.clip(r - NA_KH // 2, 0, GRID_ROWS - NA_KH)
        variant = rs - r + (NA_KH - 1)
        q = q_ref[pl.ds(pl.multiple_of(r * GRID_W, GRID_W), GRID_W), :]
        k0 = pl.multiple_of(rs * GRID_W, GRID_W)
        outs = []
        for g in range(2):
            gs = slice(g * MXU_DIM, (g + 1) * MXU_DIM)
            qg = q[:, gs]
            lhs = jnp.concatenate([jnp.where(m, qg, jnp.zeros_like(qg)) for m in head_mask], axis=0)
            s = lax.dot_general(lhs, k_ref[pl.ds(k0, nk), gs], (((1,), (1,)), ((), ())),
                                preferred_element_type=F32)
            s = s + bias_ref[variant, g]
            p = jnp.exp(s - jnp.max(s, axis=-1, keepdims=True))
            l = jnp.sum(p, axis=-1, keepdims=True)
            o = jnp.dot(p.astype(BF16), v_ref[pl.ds(k0, nk), gs], preferred_element_type=F32) / l
            og = jnp.zeros((GRID_W, MXU_DIM), F32)
            for h in range(4):
                og = og + jnp.where(head_mask[h], o[h * GRID_W:(h + 1) * GRID_W], 0.0)
            outs.append(og)
        o_ref[pl.ds(pl.multiple_of(r * GRID_W, GRID_W), GRID_W), :] = (
            jnp.concatenate(outs, axis=1).astype(BF16))
        return carry

    lax.fori_loop(0, GRID_ROWS, row, 0)


def _na_attention(q, k, v, bias):
    b = q.shape[0]
    seq = pl.BlockSpec((None, SEQ, NA_WIDTH), lambda i: (i, 0, 0))
    return pl.pallas_call(
        _na_kernel, grid=(b,),
        in_specs=[seq, seq, seq, pl.BlockSpec(bias.shape, lambda i: (0, 0, 0, 0))],
        out_specs=seq, out_shape=jax.ShapeDtypeStruct((b, SEQ, NA_WIDTH), BF16),
        compiler_params=_params("parallel"), name="na_attention",
    )(q, k, v, bias)


def _sw_kernel(sink_ref, q_ref, k_ref, v_ref, o_ref):
    nb = SEQ // SW_BLOCK
    nkeys = 3 * SW_BLOCK
    iq = lax.broadcasted_iota(jnp.int32, (SW_BLOCK, nkeys), 0)
    jk = lax.broadcasted_iota(jnp.int32, (SW_BLOCK, nkeys), 1)
    band = (jk >= iq) & (jk <= iq + 2 * SW_WINDOW)
    first = lax.broadcasted_iota(jnp.int32, (SW_BLOCK, LANES), 1) < HEAD_DIM

    def block(n, carry):
        kabs = jk + (n - 1) * SW_BLOCK
        mask1 = jnp.where(band & (kabs >= 0) & (kabs < SEQ), 0.0, NEG_INF).astype(F32)
        mask = jnp.concatenate([mask1] * SW_GROUP, axis=0)
        starts = [pl.multiple_of(jnp.clip(n + d, 0, nb - 1) * SW_BLOCK, SW_BLOCK) for d in (-1, 0, 1)]
        q = q_ref[pl.ds(pl.multiple_of(n * SW_BLOCK, SW_BLOCK), SW_BLOCK), :]
        for kvh in range(SW_KV_HEADS):
            ks = slice(kvh * LANES, (kvh + 1) * LANES)
            kk = jnp.concatenate([k_ref[pl.ds(s, SW_BLOCK), ks] for s in starts], axis=0)
            vv = jnp.concatenate([v_ref[pl.ds(s, SW_BLOCK), ks] for s in starts], axis=0)
            parts, sinks = [], []
            for j in range(SW_GROUP):
                h = kvh * SW_GROUP + j
                qc = q[:, (h // 2) * LANES:(h // 2 + 1) * LANES]
                keep = first if h % 2 == 0 else jnp.logical_not(first)
                parts.append(jnp.where(keep, qc, jnp.zeros_like(qc)))
                sinks.append(jnp.full((SW_BLOCK, 1), sink_ref[h], F32))
            lhs = jnp.concatenate(parts, axis=0)
            sink = jnp.concatenate(sinks, axis=0)
            s = lax.dot_general(lhs, kk, (((1,), (1,)), ((), ())), preferred_element_type=F32) + mask
            m = jnp.maximum(jnp.max(s, axis=-1, keepdims=True), sink)
            p = jnp.exp(s - m)
            denom = jnp.sum(p, axis=-1, keepdims=True) + jnp.exp(sink - m)
            o = jnp.dot(p.astype(BF16), vv, preferred_element_type=F32) / denom
            for c in range(SW_GROUP // 2):
                even = o[(2 * c) * SW_BLOCK:(2 * c + 1) * SW_BLOCK]
                odd = o[(2 * c + 1) * SW_BLOCK:(2 * c + 2) * SW_BLOCK]
                col = (kvh * (SW_GROUP // 2) + c) * LANES
                o_ref[pl.ds(pl.multiple_of(n * SW_BLOCK, SW_BLOCK), SW_BLOCK), col:col + LANES] = (
                    jnp.where(first, even, odd).astype(BF16))
        return carry

    lax.fori_loop(0, nb, block, 0)


def _sw_attention(q, k2, v2, sink):
    b = q.shape[0]
    qs = pl.BlockSpec((None, SEQ, SW_WIDTH), lambda i: (i, 0, 0))
    kvs = pl.BlockSpec((None, SEQ, 2 * SW_KV_WIDTH), lambda i: (i, 0, 0))
    return pl.pallas_call(
        _sw_kernel, grid=(b,),
        in_specs=[pl.BlockSpec(memory_space=pltpu.SMEM), qs, kvs, kvs],
        out_specs=qs, out_shape=jax.ShapeDtypeStruct((b, SEQ, SW_WIDTH), BF16),
        compiler_params=_params("parallel"), name="sw_attention",
    )(sink.astype(F32), q, k2, v2)


def _merge_kernel(x_ref, yna_ref, ysw_ref, gate_ref, wna_ref, wsw_ref, wout_ref, g_ref, b_ref, o_ref):
    a = jnp.dot(yna_ref[...], wna_ref[...], preferred_element_type=F32)
    s = jnp.dot(ysw_ref[...], wsw_ref[...], preferred_element_type=F32)
    m = gate_ref[:, :D_MODEL].astype(F32) * a + gate_ref[:, D_MODEL:].astype(F32) * s
    z = DEEPNORM_ALPHA * x_ref[...] + jnp.dot(m.astype(BF16), wout_ref[...], preferred_element_type=F32)
    o_ref[...] = _layer_norm(z, g_ref[...], b_ref[...])


def _merge(x, y_na, y_sw, gates, w_na, w_sw, w_out, g, b):
    n = x.shape[0]
    row = lambda w: pl.BlockSpec((ROW_TILE, w), lambda i: (i, 0))
    full = lambda a: pl.BlockSpec(a.shape, lambda i: (0,) * a.ndim)
    g2, b2 = g.reshape(1, -1), b.reshape(1, -1)
    return pl.pallas_call(
        _merge_kernel, grid=(n // ROW_TILE,),
        in_specs=[row(D_MODEL), row(NA_WIDTH), row(SW_WIDTH), row(2 * D_MODEL),
                  full(w_na), full(w_sw), full(w_out), full(g2), full(b2)],
        out_specs=row(D_MODEL), out_shape=jax.ShapeDtypeStruct((n, D_MODEL), F32),
        compiler_params=_params("parallel"), name="merge_ln1",
    )(x, y_na, y_sw, gates, w_na, w_sw, w_out, g2, b2)


def _router_kernel(x_ref, w_ref, cw_ref):
    x, w = x_ref[...], w_ref[...]
    xh, wh = x.astype(BF16), w.astype(BF16)
    xl = (x - xh.astype(F32)).astype(BF16)
    wl = (w - wh.astype(F32)).astype(BF16)
    dot = functools.partial(jnp.dot, preferred_element_type=F32)
    logits = dot(xh, wh) + dot(xl, wh) + dot(xh, wl)
    lane = lax.broadcasted_iota(jnp.int32, logits.shape, 1).astype(F32)
    lg = jnp.where(lane < N_EXPERTS, logits, -jnp.inf)
    m1 = jnp.max(lg, axis=-1, keepdims=True)
    i1 = jnp.min(jnp.where(lg == m1, lane, float(LANES)), axis=-1, keepdims=True)
    lg2 = jnp.where(lane == i1, -jnp.inf, lg)
    m2 = jnp.max(lg2, axis=-1, keepdims=True)
    i2 = jnp.min(jnp.where(lg2 == m2, lane, float(LANES)), axis=-1, keepdims=True)
    e2 = jnp.exp(m2 - m1)
    denom = 1.0 + e2
    cw_ref[...] = jnp.where(lane == i1, 1.0 / denom, 0.0) + jnp.where(lane == i2, e2 / denom, 0.0)


def _router(x, w_router):
    n = x.shape[0]
    w = jnp.zeros((D_MODEL, LANES), F32).at[:, :N_EXPERTS].set(w_router)
    return pl.pallas_call(
        _router_kernel, grid=(n // ROW_TILE,),
        in_specs=[pl.BlockSpec((ROW_TILE, D_MODEL), lambda i: (i, 0)),
                  pl.BlockSpec((D_MODEL, LANES), lambda i: (0, 0))],
        out_specs=pl.BlockSpec((ROW_TILE, LANES), lambda i: (i, 0)),
        out_shape=jax.ShapeDtypeStruct((n, LANES), F32),
        compiler_params=_params("parallel"), name="router_top2",
    )(x, w)


def _ffn_kernel(*refs, routed):
    if routed:
        x_ref, cw_ref, wg_ref, wu_ref, wd_ref, g_ref, b_ref, o_ref, xb_ref, acc_ref = refs
    else:
        x_ref, wg_ref, wu_ref, wd_ref, g_ref, b_ref, o_ref, xb_ref, acc_ref = refs
    e, j = pl.program_id(1), pl.program_id(2)

    @pl.when((e == 0) & (j == 0))
    def _():
        xb_ref[...] = x_ref[...].astype(BF16)
        acc_ref[...] = jnp.zeros_like(acc_ref)

    xb = xb_ref[...]
    hg = jnp.dot(xb, wg_ref[...].astype(BF16), preferred_element_type=F32)
    hu = jnp.dot(xb, wu_ref[...].astype(BF16), preferred_element_type=F32)
    h = hg * _sigmoid(hg) * hu
    part = jnp.dot(h.astype(BF16), wd_ref[...].astype(BF16), preferred_element_type=F32)
    if routed:
        lane = lax.broadcasted_iota(jnp.int32, cw_ref.shape, 1)
        part = part * jnp.sum(jnp.where(lane == e, cw_ref[...], 0.0), axis=-1, keepdims=True)
    acc_ref[...] += part

    @pl.when((e == pl.num_programs(1) - 1) & (j == pl.num_programs(2) - 1))
    def _():
        o_ref[...] = _layer_norm(DEEPNORM_ALPHA * x_ref[...] + acc_ref[...], g_ref[...], b_ref[...])


def _ffn(x, cw, w_gate, w_up, w_down, g, b, ff_tile):
    n = x.shape[0]
    n_exp, _, d_ff = w_gate.shape
    tm = FFN_ROW_TILE
    row = pl.BlockSpec((tm, D_MODEL), lambda i, e, j: (i, 0))
    vec = pl.BlockSpec((1, D_MODEL), lambda i, e, j: (0, 0))
    up = pl.BlockSpec((None, D_MODEL, ff_tile), lambda i, e, j: (e, 0, j))
    down = pl.BlockSpec((None, ff_tile, D_MODEL), lambda i, e, j: (e, j, 0))
    routed = cw is not None
    in_specs = [row] + ([pl.BlockSpec((tm, LANES), lambda i, e, j: (i, 0))] if routed else [])
    args = [x] + ([cw] if routed else [])
    return pl.pallas_call(
        functools.partial(_ffn_kernel, routed=routed),
        grid=(n // tm, n_exp, d_ff // ff_tile),
        in_specs=in_specs + [up, up, down, vec, vec], out_specs=row,
        out_shape=jax.ShapeDtypeStruct((n, D_MODEL), F32),
        scratch_shapes=[pltpu.VMEM((tm, D_MODEL), BF16), pltpu.VMEM((tm, D_MODEL), F32)],
        compiler_params=_params("parallel", "arbitrary", "arbitrary"),
        name="moe_swiglu_ln2" if routed else "swiglu_ln2",
    )(*args, w_gate, w_up, w_down, g.reshape(1, -1), b.reshape(1, -1))


def kernel(x, emb_ln_g, emb_ln_b, w_in, b_gate, na_rpb, sw_sink, w_branch_na, w_branch_sw, w_out,
           ln1_g, ln1_b, ffn_w_gate, ffn_w_up, ffn_w_down, moe_router, moe_w_gate, moe_w_up,
           moe_w_down, ln2_g, ln2_b):
    batch, seq, d = x.shape
    assert (seq, d) == (SEQ, D_MODEL)
    n = batch * seq
    tables = _rotary_tables()
    h = _ln(x.reshape(n, d), emb_ln_g, emb_ln_b)
    for layer in range(DEPTH):
        q_na, k_na, v_na, q_sw, k_sw, v_sw, gates = _proj(
            h, w_in[layer].astype(BF16), b_gate[layer], tables)
        seq3 = lambda t: t.reshape(batch, seq, t.shape[-1])
        y_na = _na_attention(seq3(q_na), seq3(k_na), seq3(v_na), _na_bias_tables(na_rpb[layer]))
        y_sw = _sw_attention(seq3(q_sw), seq3(k_sw), seq3(v_sw), sw_sink[layer])
        h = _merge(h, y_na.reshape(n, -1), y_sw.reshape(n, -1), gates,
                   w_branch_na[layer].astype(BF16), w_branch_sw[layer].astype(BF16),
                   w_out[layer].astype(BF16), ln1_g[layer], ln1_b[layer])
        i = layer // 2
        if layer % 2 == 0:
            h = _ffn(h, None, ffn_w_gate[i:i + 1].astype(BF16), ffn_w_up[i:i + 1].astype(BF16),
                     ffn_w_down[i:i + 1].astype(BF16), ln2_g[layer], ln2_b[layer], DENSE_FF_TILE)
        else:
            cw = _router(h, moe_router[i])
            h = _ffn(h, cw, moe_w_gate[i], moe_w_up[i], moe_w_down[i],
                     ln2_g[layer], ln2_b[layer], MOE_FF_TILE)
    return h.reshape(batch, seq, d)
```

```python
import functools

import numpy as np
import jax
import jax.numpy as jnp
from jax import lax
from jax.experimental import pallas as pl
from jax.experimental.pallas import tpu as pltpu

F32 = jnp.float32
BF16 = jnp.bfloat16

D_MODEL = 1024
SEQ = 2048
DEPTH = 2
HEAD_DIM = 64
NA_HEADS = 8
NA_WIDTH = NA_HEADS * HEAD_DIM
GRID_W = 64
GRID_ROWS = SEQ // GRID_W
NA_KH = 8
NA_KW = 16
SW_HEADS = 8
SW_KV_HEADS = 2
SW_GROUP = SW_HEADS // SW_KV_HEADS
SW_WIDTH = SW_HEADS * HEAD_DIM
SW_KV_WIDTH = SW_KV_HEADS * HEAD_DIM
SW_WINDOW = 128
SW_BLOCK = 128
ROT_DIM = HEAD_DIM // 4
ROPE_THETA = 500000.0
OFF_QNA = NA_WIDTH
OFF_KNA = 2 * NA_WIDTH
OFF_VNA = 3 * NA_WIDTH
OFF_QSW = OFF_VNA + SW_WIDTH
OFF_KSW = OFF_QSW + SW_KV_WIDTH
OFF_VSW = OFF_KSW + SW_KV_WIDTH
PROJ_COLS = OFF_VSW + 2 * D_MODEL
N_EXPERTS = 8
DEEPNORM_ALPHA = (2 * DEPTH) ** 0.25
LN_EPS = 1e-5
NEG_INF = -1e30
QK_SCALE = HEAD_DIM ** -0.5

LANES = 128
MXU_DIM = 256
V7X_VMEM_BYTES = 64 * 1024 * 1024
VMEM_LIMIT = V7X_VMEM_BYTES * 7 // 8

ROW_TILE = 512
FFN_ROW_TILE = 1024
DENSE_FF_TILE = 1408
MOE_ROW_TILE = 1536
MOE_FF_TILE = 512
MOE_CHUNK = 256
SEG_ALIGN = 16


def _layer_norm(z, g, b):
    mu = jnp.mean(z, axis=-1, keepdims=True)
    d = z - mu
    var = jnp.mean(d * d, axis=-1, keepdims=True)
    return d * lax.rsqrt(var + LN_EPS) * g + b


def _sigmoid(z):
    return 1.0 / (1.0 + jnp.exp(-z))


def _params(*sem):
    return pltpu.CompilerParams(dimension_semantics=sem, vmem_limit_bytes=VMEM_LIMIT)


def _ln_kernel(x_ref, g_ref, b_ref, o_ref):
    o_ref[...] = _layer_norm(x_ref[...], g_ref[...], b_ref[...])


def _ln(x, g, b):
    n = x.shape[0]
    row = pl.BlockSpec((ROW_TILE, D_MODEL), lambda i: (i, 0))
    vec = pl.BlockSpec((1, D_MODEL), lambda i: (0, 0))
    return pl.pallas_call(
        _ln_kernel, grid=(n // ROW_TILE,), in_specs=[row, vec, vec], out_specs=row,
        out_shape=jax.ShapeDtypeStruct((n, D_MODEL), F32), compiler_params=_params("parallel"),
        name="embed_ln",
    )(x, g.reshape(1, -1), b.reshape(1, -1))


def _rotary_tables():
    half = ROT_DIM // 2
    inv_freq = 1.0 / (ROPE_THETA ** (jnp.arange(0, ROT_DIM, 2, dtype=F32) / ROT_DIM))
    ang = jnp.arange(SEQ, dtype=jnp.int32).astype(F32)[:, None] * inv_freq[None, :]
    cos, sin = jnp.cos(ang), jnp.sin(ang)
    ones = jnp.ones((SEQ, HEAD_DIM - ROT_DIM), F32)
    zeros = jnp.zeros((SEQ, HEAD_DIM - ROT_DIM), F32)
    zh = jnp.zeros((SEQ, half), F32)
    cos_h = jnp.concatenate([cos, cos, ones], axis=1)
    sa_h = jnp.concatenate([-sin, zh, zeros], axis=1)
    sb_h = jnp.concatenate([zh, sin, zeros], axis=1)
    two = lambda t: jnp.concatenate([t, t], axis=1)
    return two(cos_h), two(sa_h), two(sb_h)


def _proj_kernel(x_ref, w_ref, bg_ref, cos_ref, sa_ref, sb_ref,
                 qna_ref, kna_ref, vna_ref, qsw_ref, ksw_ref, vsw_ref, gate_ref):
    xb = x_ref[...].astype(BF16)

    def mm(lo, hi):
        return jnp.dot(xb, w_ref[:, lo:hi], preferred_element_type=F32)

    qna_ref[...] = (mm(0, OFF_QNA) * QK_SCALE).astype(BF16)
    kna_ref[...] = mm(OFF_QNA, OFF_KNA).astype(BF16)
    vna_ref[...] = mm(OFF_KNA, OFF_VNA).astype(BF16)

    cos, sa, sb = cos_ref[...], sa_ref[...], sb_ref[...]
    half = ROT_DIM // 2

    def rot(t):
        return t * cos + pltpu.roll(t, LANES - half, 1) * sa + pltpu.roll(t, half, 1) * sb

    q = mm(OFF_VNA, OFF_QSW)
    for c in range(SW_WIDTH // LANES):
        sl = slice(c * LANES, (c + 1) * LANES)
        qsw_ref[:, sl] = (rot(q[:, sl]) * QK_SCALE).astype(BF16)

    kv = mm(OFF_QSW, OFF_VSW)
    k = rot(kv[:, :SW_KV_WIDTH])
    v = kv[:, SW_KV_WIDTH:]
    first = lax.broadcasted_iota(jnp.int32, k.shape, 1) < HEAD_DIM

    def dup(t):
        r = pltpu.roll(t, HEAD_DIM, 1)
        return jnp.concatenate([jnp.where(first, t, r), jnp.where(first, r, t)], axis=1)

    ksw_ref[...] = dup(k).astype(BF16)
    vsw_ref[...] = dup(v).astype(BF16)

    gw = 512
    for c in range(2 * D_MODEL // gw):
        z = mm(OFF_VSW + c * gw, OFF_VSW + (c + 1) * gw) + bg_ref[:, c * gw:(c + 1) * gw]
        gate_ref[:, c * gw:(c + 1) * gw] = _sigmoid(z).astype(BF16)


def _proj(x, w_bf, b_gate, tables):
    n = x.shape[0]
    tiles_per_seq = SEQ // ROW_TILE
    row = lambda w: pl.BlockSpec((ROW_TILE, w), lambda i: (i, 0))
    full = lambda a: pl.BlockSpec(a.shape, lambda i: (0,) * a.ndim)
    tab = pl.BlockSpec((ROW_TILE, LANES), lambda i: (i % tiles_per_seq, 0))
    bg = b_gate.reshape(1, -1)
    widths = (NA_WIDTH, NA_WIDTH, NA_WIDTH, SW_WIDTH, 2 * SW_KV_WIDTH, 2 * SW_KV_WIDTH, 2 * D_MODEL)
    return pl.pallas_call(
        _proj_kernel, grid=(n // ROW_TILE,),
        in_specs=[row(D_MODEL), full(w_bf), full(bg), tab, tab, tab],
        out_specs=[row(w) for w in widths],
        out_shape=[jax.ShapeDtypeStruct((n, w), BF16) for w in widths],
        compiler_params=_params("parallel"), name="in_proj",
    )(x, w_bf, bg, *tables)


def _na_bias_tables(rpb):
    c = np.arange(GRID_W)
    qcs = np.clip(c - NA_KW // 2, 0, GRID_W - NA_KW)
    valid = (c[None, :] >= qcs[:, None]) & (c[None, :] < qcs[:, None] + NA_KW)
    n_dr, n_dc = 2 * NA_KH - 1, 2 * NA_KW - 1
    span = 2 * GRID_W - 1
    lead = GRID_W - NA_KW
    ext = jnp.pad(rpb.astype(F32), ((0, 0), (0, 0), (lead, span + 1 - lead - n_dc)))
    flat = jnp.broadcast_to(ext[:, :, None, :], (NA_HEADS, n_dr, GRID_W, span + 1))
    flat = flat.reshape(NA_HEADS, n_dr, GRID_W * (span + 1))[:, :, :GRID_W * span]
    t = flat.reshape(NA_HEADS, n_dr, GRID_W, span)[:, :, :, GRID_W - 1:]
    t = jnp.where(valid[None, None], t, NEG_INF)
    b = jnp.stack([t[:, v:v + NA_KH] for v in range(NA_KH)], axis=0)
    b = b.transpose(0, 1, 3, 2, 4)
    return b.reshape(NA_KH, 2, 4 * GRID_W, NA_KH * GRID_W)


def _na_kernel(q_ref, k_ref, v_ref, bias_ref, o_ref):
    lane = lax.broadcasted_iota(jnp.int32, (GRID_W, MXU_DIM), 1)
    head_mask = [(lane >= h * HEAD_DIM) & (lane < (h + 1) * HEAD_DIM) for h in range(4)]
    nk = NA_KH * GRID_W

    def row(r, carry):
        rs = jnp.clip(r - NA_KH // 2, 0, GRID_ROWS - NA_KH)
        variant = rs - r + (NA_KH - 1)
        q = q_ref[pl.ds(pl.multiple_of(r * GRID_W, GRID_W), GRID_W), :]
        k0 = pl.multiple_of(rs * GRID_W, GRID_W)
        outs = []
        for g in range(2):
            gs = slice(g * MXU_DIM, (g + 1) * MXU_DIM)
            qg = q[:, gs]
            lhs = jnp.concatenate([jnp.where(m, qg, jnp.zeros_like(qg)) for m in head_mask], axis=0)
            s = lax.dot_general(lhs, k_ref[pl.ds(k0, nk), gs], (((1,), (1,)), ((), ())),
                                preferred_element_type=F32)
            s = s + bias_ref[variant, g]
            p = jnp.exp(s - jnp.max(s, axis=-1, keepdims=True))
            l = jnp.sum(p, axis=-1, keepdims=True)
            o = jnp.dot(p.astype(BF16), v_ref[pl.ds(k0, nk), gs], preferred_element_type=F32) / l
            og = jnp.zeros((GRID_W, MXU_DIM), F32)
            for h in range(4):
                og = og + jnp.where(head_mask[h], o[h * GRID_W:(h + 1) * GRID_W], 0.0)
            outs.append(og)
        o_ref[pl.ds(pl.multiple_of(r * GRID_W, GRID_W), GRID_W), :] = (
            jnp.concatenate(outs, axis=1).astype(BF16))
        return carry

    lax.fori_loop(0, GRID_ROWS, row, 0)


def _na_attention(q, k, v, bias):
    b = q.shape[0]
    seq = pl.BlockSpec((None, SEQ, NA_WIDTH), lambda i: (i, 0, 0))
    return pl.pallas_call(
        _na_kernel, grid=(b,),
        in_specs=[seq, seq, seq, pl.BlockSpec(bias.shape, lambda i: (0, 0, 0, 0))],
        out_specs=seq, out_shape=jax.ShapeDtypeStruct((b, SEQ, NA_WIDTH), BF16),
        compiler_params=_params("parallel"), name="na_attention",
    )(q, k, v, bias)


def _sw_kernel(sink_ref, q_ref, k_ref, v_ref, o_ref):
    nb = SEQ // SW_BLOCK
    nkeys = 3 * SW_BLOCK
    iq = lax.broadcasted_iota(jnp.int32, (SW_BLOCK, nkeys), 0)
    jk = lax.broadcasted_iota(jnp.int32, (SW_BLOCK, nkeys), 1)
    band = (jk >= iq) & (jk <= iq + 2 * SW_WINDOW)
    first = lax.broadcasted_iota(jnp.int32, (SW_BLOCK, LANES), 1) < HEAD_DIM

    def block(n, carry):
        kabs = jk + (n - 1) * SW_BLOCK
        mask1 = jnp.where(band & (kabs >= 0) & (kabs < SEQ), 0.0, NEG_INF).astype(F32)
        mask = jnp.concatenate([mask1] * SW_GROUP, axis=0)
        starts = [pl.multiple_of(jnp.clip(n + d, 0, nb - 1) * SW_BLOCK, SW_BLOCK) for d in (-1, 0, 1)]
        q = q_ref[pl.ds(pl.multiple_of(n * SW_BLOCK, SW_BLOCK), SW_BLOCK), :]
        for kvh in range(SW_KV_HEADS):
            ks = slice(kvh * LANES, (kvh + 1) * LANES)
            kk = jnp.concatenate([k_ref[pl.ds(s, SW_BLOCK), ks] for s in starts], axis=0)
            vv = jnp.concatenate([v_ref[pl.ds(s, SW_BLOCK), ks] for s in starts], axis=0)
            parts, sinks = [], []
            for j in range(SW_GROUP):
                h = kvh * SW_GROUP + j
                qc = q[:, (h // 2) * LANES:(h // 2 + 1) * LANES]
                keep = first if h % 2 == 0 else jnp.logical_not(first)
                parts.append(jnp.where(keep, qc, jnp.zeros_like(qc)))
                sinks.append(jnp.full((SW_BLOCK, 1), sink_ref[h], F32))
            lhs = jnp.concatenate(parts, axis=0)
            sink = jnp.concatenate(sinks, axis=0)
            s = lax.dot_general(lhs, kk, (((1,), (1,)), ((), ())), preferred_element_type=F32) + mask
            m = jnp.maximum(jnp.max(s, axis=-1, keepdims=True), sink)
            p = jnp.exp(s - m)
            denom = jnp.sum(p, axis=-1, keepdims=True) + jnp.exp(sink - m)
            o = jnp.dot(p.astype(BF16), vv, preferred_element_type=F32) / denom
            for c in range(SW_GROUP // 2):
                even = o[(2 * c) * SW_BLOCK:(2 * c + 1) * SW_BLOCK]
                odd = o[(2 * c + 1) * SW_BLOCK:(2 * c + 2) * SW_BLOCK]
                col = (kvh * (SW_GROUP // 2) + c) * LANES
                o_ref[pl.ds(pl.multiple_of(n * SW_BLOCK, SW_BLOCK), SW_BLOCK), col:col + LANES] = (
                    jnp.where(first, even, odd).astype(BF16))
        return carry

    lax.fori_loop(0, nb, block, 0)


def _sw_attention(q, k2, v2, sink):
    b = q.shape[0]
    qs = pl.BlockSpec((None, SEQ, SW_WIDTH), lambda i: (i, 0, 0))
    kvs = pl.BlockSpec((None, SEQ, 2 * SW_KV_WIDTH), lambda i: (i, 0, 0))
    return pl.pallas_call(
        _sw_kernel, grid=(b,),
        in_specs=[pl.BlockSpec(memory_space=pltpu.SMEM), qs, kvs, kvs],
        out_specs=qs, out_shape=jax.ShapeDtypeStruct((b, SEQ, SW_WIDTH), BF16),
        compiler_params=_params("parallel"), name="sw_attention",
    )(sink.astype(F32), q, k2, v2)


def _merge_kernel(x_ref, yna_ref, ysw_ref, gate_ref, wna_ref, wsw_ref, wout_ref, g_ref, b_ref, o_ref):
    a = jnp.dot(yna_ref[...], wna_ref[...], preferred_element_type=F32)
    s = jnp.dot(ysw_ref[...], wsw_ref[...], preferred_element_type=F32)
    m = gate_ref[:, :D_MODEL].astype(F32) * a + gate_ref[:, D_MODEL:].astype(F32) * s
    z = DEEPNORM_ALPHA * x_ref[...] + jnp.dot(m.astype(BF16), wout_ref[...], preferred_element_type=F32)
    o_ref[...] = _layer_norm(z, g_ref[...], b_ref[...])


def _merge(x, y_na, y_sw, gates, w_na, w_sw, w_out, g, b):
    n = x.shape[0]
    row = lambda w: pl.BlockSpec((ROW_TILE, w), lambda i: (i, 0))
    full = lambda a: pl.BlockSpec(a.shape, lambda i: (0,) * a.ndim)
    g2, b2 = g.reshape(1, -1), b.reshape(1, -1)
    return pl.pallas_call(
        _merge_kernel, grid=(n // ROW_TILE,),
        in_specs=[row(D_MODEL), row(NA_WIDTH), row(SW_WIDTH), row(2 * D_MODEL),
                  full(w_na), full(w_sw), full(w_out), full(g2), full(b2)],
        out_specs=row(D_MODEL), out_shape=jax.ShapeDtypeStruct((n, D_MODEL), F32),
        compiler_params=_params("parallel"), name="merge_ln1",
    )(x, y_na, y_sw, gates, w_na, w_sw, w_out, g2, b2)


def _router_kernel(x_ref, w_ref, cw_ref, pos_ref, post_ref, cnt_ref):
    x, w = x_ref[...], w_ref[...]
    xh, wh = x.astype(BF16), w.astype(BF16)
    xl = (x - xh.astype(F32)).astype(BF16)
    wl = (w - wh.astype(F32)).astype(BF16)
    dot = functools.partial(jnp.dot, preferred_element_type=F32)
    logits = dot(xh, wh) + dot(xl, wh) + dot(xh, wl)
    lane = lax.broadcasted_iota(jnp.int32, logits.shape, 1).astype(F32)
    lg = jnp.where(lane < N_EXPERTS, logits, -jnp.inf)
    m1 = jnp.max(lg, axis=-1, keepdims=True)
    i1 = jnp.min(jnp.where(lg == m1, lane, float(LANES)), axis=-1, keepdims=True)
    lg2 = jnp.where(lane == i1, -jnp.inf, lg)
    m2 = jnp.max(lg2, axis=-1, keepdims=True)
    i2 = jnp.min(jnp.where(lg2 == m2, lane, float(LANES)), axis=-1, keepdims=True)
    e2 = jnp.exp(m2 - m1)
    denom = 1.0 + e2
    cw = jnp.where(lane == i1, 1.0 / denom, 0.0) + jnp.where(lane == i2, e2 / denom, 0.0)
    cw_ref[...] = cw

    t = cw.shape[0]
    sel = jnp.where(cw.T[:N_EXPERTS] > 0.0, 1.0, 0.0)
    tri = jnp.where(lax.broadcasted_iota(jnp.int32, (t, t), 0) <= lax.broadcasted_iota(jnp.int32, (t, t), 1),
                    1.0, 0.0).astype(BF16)
    incl = dot(sel.astype(BF16), tri)
    post = jnp.where(sel > 0.0, incl - 1.0, -1.0)
    post_ref[...] = post
    pos_ref[...] = jnp.concatenate([post, jnp.full((LANES - N_EXPERTS, t), -1.0, F32)], axis=0).T
    cnt_ref[...] = jnp.broadcast_to(incl[:, t - 1:t], (N_EXPERTS, LANES))


def _router(x, w_router):
    n = x.shape[0]
    nblk = n // ROW_TILE
    w = jnp.zeros((D_MODEL, LANES), F32).at[:, :N_EXPERTS].set(w_router)
    tok = pl.BlockSpec((ROW_TILE, LANES), lambda i: (i, 0))
    cw, pos, post, cnt = pl.pallas_call(
        _router_kernel, grid=(nblk,),
        in_specs=[pl.BlockSpec((ROW_TILE, D_MODEL), lambda i: (i, 0)),
                  pl.BlockSpec((D_MODEL, LANES), lambda i: (0, 0))],
        out_specs=[tok, tok, pl.BlockSpec((N_EXPERTS, ROW_TILE), lambda i: (0, i)),
                   pl.BlockSpec((N_EXPERTS, LANES), lambda i: (i, 0))],
        out_shape=[jax.ShapeDtypeStruct((n, LANES), F32), jax.ShapeDtypeStruct((n, LANES), F32),
                   jax.ShapeDtypeStruct((N_EXPERTS, n), F32),
                   jax.ShapeDtypeStruct((nblk * N_EXPERTS, LANES), F32)],
        compiler_params=_params("parallel"), name="router_top2",
    )(x, w)
    return cw, pos, post, cnt[:, 0].astype(jnp.int32).reshape(nblk, N_EXPERTS)


def _moe_layout(cnt, n_tiles):
    seg = (cnt + SEG_ALIGN - 1) // SEG_ALIGN * SEG_ALIGN
    rows_e = jnp.sum(seg, axis=0)
    tiles_e = (rows_e + MOE_CHUNK + MOE_ROW_TILE - 1) // MOE_ROW_TILE
    tile_end = jnp.cumsum(tiles_e)
    tile_off = tile_end - tiles_e
    seg_off = (tile_off * MOE_ROW_TILE)[None, :] + jnp.cumsum(seg, axis=0) - seg
    tile = jnp.arange(n_tiles, dtype=jnp.int32)
    tile_expert = jnp.minimum(jnp.sum(tile[:, None] >= tile_end[None, :], axis=1), N_EXPERTS - 1)
    occupied = jnp.clip(rows_e[tile_expert] - (tile - tile_off[tile_expert]) * MOE_ROW_TILE, 0, MOE_ROW_TILE)
    as_i32 = lambda t: t.astype(jnp.int32)
    return as_i32(seg_off).reshape(-1), as_i32(tile_expert), as_i32(occupied)


def _chunk_copies(off_ref, cnt_ref, hbm_ref, vmem_ref, sems, to_hbm):
    b = pl.program_id(0)
    out = []
    for e in range(N_EXPERTS):
        c = cnt_ref[b * N_EXPERTS + e]
        off = pl.multiple_of(off_ref[b * N_EXPERTS + e], SEG_ALIGN)
        for k in range(ROW_TILE // MOE_CHUNK):
            idx = e * (ROW_TILE // MOE_CHUNK) + k
            rows = hbm_ref.at[pl.ds(off + k * MOE_CHUNK, MOE_CHUNK)]
            src, dst = (vmem_ref.at[idx], rows) if to_hbm else (rows, vmem_ref.at[idx])
            out.append((c > k * MOE_CHUNK, e, k, idx, pltpu.make_async_copy(src, dst, sems.at[idx])))
    return out


def _dispatch_kernel(off_ref, cnt_ref, x_ref, post_ref, xs_in_ref, xs_ref, stage_ref, sems):
    del xs_in_ref
    xb = x_ref[...].astype(BF16)
    slot = lax.broadcasted_iota(jnp.int32, (MOE_CHUNK, ROW_TILE), 0).astype(F32)
    copies = _chunk_copies(off_ref, cnt_ref, xs_ref, stage_ref, sems, to_hbm=True)
    for pred, e, k, idx, copy in copies:
        @pl.when(pred)
        def _(e=e, k=k, idx=idx, copy=copy):
            onehot = jnp.where(post_ref[e:e + 1, :] == slot + float(k * MOE_CHUNK), 1.0, 0.0).astype(BF16)
            stage_ref[idx] = jnp.dot(onehot, xb, preferred_element_type=F32).astype(BF16)
            copy.start()
    for pred, _, _, _, copy in copies:
        @pl.when(pred)
        def _(copy=copy):
            copy.wait()


def _dispatch(x, post, seg_off, cnt, n_rows):
    n = x.shape[0]
    n_chunks = N_EXPERTS * (ROW_TILE // MOE_CHUNK)
    grid_spec = pltpu.PrefetchScalarGridSpec(
        num_scalar_prefetch=2, grid=(n // ROW_TILE,),
        in_specs=[pl.BlockSpec((ROW_TILE, D_MODEL), lambda i, *_: (i, 0)),
                  pl.BlockSpec((N_EXPERTS, ROW_TILE), lambda i, *_: (0, i)),
                  pl.BlockSpec(memory_space=pl.ANY)],
        out_specs=pl.BlockSpec(memory_space=pl.ANY),
        scratch_shapes=[pltpu.VMEM((n_chunks, MOE_CHUNK, D_MODEL), BF16),
                        pltpu.SemaphoreType.DMA((n_chunks,))])
    return pl.pallas_call(
        _dispatch_kernel, grid_spec=grid_spec,
        out_shape=jax.ShapeDtypeStruct((n_rows, D_MODEL), BF16),
        input_output_aliases={4: 0},
        compiler_params=_params("arbitrary"), name="moe_dispatch",
    )(seg_off, cnt.reshape(-1), x, post, jnp.zeros((n_rows, D_MODEL), BF16))


def _swiglu_partial(xb, wg, wu, wd):
    hg = jnp.dot(xb, wg, preferred_element_type=F32)
    hu = jnp.dot(xb, wu, preferred_element_type=F32)
    h = hg * _sigmoid(hg) * hu
    return jnp.dot(h.astype(BF16), wd, preferred_element_type=F32)


def _moe_ffn_kernel(expert_ref, occ_ref, x_ref, wg_ref, wu_ref, wd_ref, o_ref, acc_ref):
    del expert_ref
    i, j = pl.program_id(0), pl.program_id(1)
    occupied = occ_ref[i]

    @pl.when(j == 0)
    def _():
        acc_ref[...] = jnp.zeros_like(acc_ref)

    wg, wu, wd = wg_ref[...].astype(BF16), wu_ref[...].astype(BF16), wd_ref[...].astype(BF16)
    for s in range(MOE_ROW_TILE // MOE_CHUNK):
        rows = slice(s * MOE_CHUNK, (s + 1) * MOE_CHUNK)

        @pl.when(s * MOE_CHUNK < occupied)
        def _(rows=rows):
            acc_ref[rows, :] += _swiglu_partial(x_ref[rows, :], wg, wu, wd)

    @pl.when(j == pl.num_programs(1) - 1)
    def _():
        o_ref[...] = acc_ref[...].astype(BF16)


def _moe_ffn(xs, tile_expert, occupied, w_gate, w_up, w_down):
    n_rows = xs.shape[0]
    d_ff = w_gate.shape[-1]
    row = pl.BlockSpec((MOE_ROW_TILE, D_MODEL), lambda i, j, *_: (i, 0))
    up = pl.BlockSpec((None, D_MODEL, MOE_FF_TILE), lambda i, j, ex, occ: (ex[i], 0, j))
    down = pl.BlockSpec((None, MOE_FF_TILE, D_MODEL), lambda i, j, ex, occ: (ex[i], j, 0))
    grid_spec = pltpu.PrefetchScalarGridSpec(
        num_scalar_prefetch=2, grid=(n_rows // MOE_ROW_TILE, d_ff // MOE_FF_TILE),
        in_specs=[row, up, up, down], out_specs=row,
        scratch_shapes=[pltpu.VMEM((MOE_ROW_TILE, D_MODEL), F32)])
    return pl.pallas_call(
        _moe_ffn_kernel, grid_spec=grid_spec,
        out_shape=jax.ShapeDtypeStruct((n_rows, D_MODEL), BF16),
        compiler_params=_params("parallel", "arbitrary"), name="moe_swiglu",
    )(tile_expert, occupied, xs, w_gate, w_up, w_down)


def _ffn_kernel(x_ref, wg_ref, wu_ref, wd_ref, g_ref, b_ref, o_ref, xb_ref, acc_ref):
    j = pl.program_id(1)

    @pl.when(j == 0)
    def _():
        xb_ref[...] = x_ref[...].astype(BF16)
        acc_ref[...] = jnp.zeros_like(acc_ref)

    acc_ref[...] += _swiglu_partial(xb_ref[...], wg_ref[...], wu_ref[...], wd_ref[...])

    @pl.when(j == pl.num_programs(1) - 1)
    def _():
        o_ref[...] = _layer_norm(DEEPNORM_ALPHA * x_ref[...] + acc_ref[...], g_ref[...], b_ref[...])


def _ffn(x, w_gate, w_up, w_down, g, b):
    n = x.shape[0]
    d_ff = w_gate.shape[-1]
    tm = FFN_ROW_TILE
    row = pl.BlockSpec((tm, D_MODEL), lambda i, j: (i, 0))
    vec = pl.BlockSpec((1, D_MODEL), lambda i, j: (0, 0))
    up = pl.BlockSpec((D_MODEL, DENSE_FF_TILE), lambda i, j: (0, j))
    down = pl.BlockSpec((DENSE_FF_TILE, D_MODEL), lambda i, j: (j, 0))
    return pl.pallas_call(
        _ffn_kernel, grid=(n // tm, d_ff // DENSE_FF_TILE),
        in_specs=[row, up, up, down, vec, vec], out_specs=row,
        out_shape=jax.ShapeDtypeStruct((n, D_MODEL), F32),
        scratch_shapes=[pltpu.VMEM((tm, D_MODEL), BF16), pltpu.VMEM((tm, D_MODEL), F32)],
        compiler_params=_params("parallel", "arbitrary"), name="swiglu_ln2",
    )(x, w_gate, w_up, w_down, g.reshape(1, -1), b.reshape(1, -1))


def _combine_kernel(off_ref, cnt_ref, x_ref, cw_ref, pos_ref, g_ref, b_ref, ys_ref, o_ref,
                    buf_ref, acc_ref, sems):
    copies = _chunk_copies(off_ref, cnt_ref, ys_ref, buf_ref, sems, to_hbm=False)
    for pred, _, _, _, copy in copies:
        @pl.when(pred)
        def _(copy=copy):
            copy.start()

    acc_ref[...] = jnp.zeros_like(acc_ref)
    lane = lax.broadcasted_iota(jnp.int32, (ROW_TILE, LANES), 1)
    slot = lax.broadcasted_iota(jnp.int32, (ROW_TILE, MOE_CHUNK), 1).astype(F32)
    for pred, e, k, idx, copy in copies:
        @pl.when(pred)
        def _(e=e, k=k, idx=idx, copy=copy):
            copy.wait()
            weight = jnp.sum(jnp.where(lane == e, cw_ref[...], 0.0), axis=-1, keepdims=True)
            pos = jnp.sum(jnp.where(lane == e, pos_ref[...], 0.0), axis=-1, keepdims=True)
            onehot = jnp.where(pos == slot + float(k * MOE_CHUNK), 1.0, 0.0).astype(BF16)
            acc_ref[...] += weight * jnp.dot(onehot, buf_ref[idx], preferred_element_type=F32)

    o_ref[...] = _layer_norm(DEEPNORM_ALPHA * x_ref[...] + acc_ref[...], g_ref[...], b_ref[...])


def _combine(x, ys, cw, pos, seg_off, cnt, g, b):
    n = x.shape[0]
    n_chunks = N_EXPERTS * (ROW_TILE // MOE_CHUNK)
    tok = lambda w: pl.BlockSpec((ROW_TILE, w), lambda i, *_: (i, 0))
    vec = pl.BlockSpec((1, D_MODEL), lambda i, *_: (0, 0))
    grid_spec = pltpu.PrefetchScalarGridSpec(
        num_scalar_prefetch=2, grid=(n // ROW_TILE,),
        in_specs=[tok(D_MODEL), tok(LANES), tok(LANES), vec, vec, pl.BlockSpec(memory_space=pl.ANY)],
        out_specs=tok(D_MODEL),
        scratch_shapes=[pltpu.VMEM((n_chunks, MOE_CHUNK, D_MODEL), BF16),
                        pltpu.VMEM((ROW_TILE, D_MODEL), F32),
                        pltpu.SemaphoreType.DMA((n_chunks,))])
    return pl.pallas_call(
        _combine_kernel, grid_spec=grid_spec,
        out_shape=jax.ShapeDtypeStruct((n, D_MODEL), F32),
        compiler_params=_params("parallel"), name="moe_combine_ln2",
    )(seg_off, cnt.reshape(-1), x, cw, pos, g.reshape(1, -1), b.reshape(1, -1), ys)


def _moe(x, w_router, w_gate, w_up, w_down, g, b):
    n = x.shape[0]
    nblk = n // ROW_TILE
    max_rows = 2 * n + nblk * N_EXPERTS * (SEG_ALIGN - 1) + N_EXPERTS * (MOE_CHUNK + MOE_ROW_TILE - 1)
    n_tiles = max_rows // MOE_ROW_TILE
    cw, pos, post, cnt = _router(x, w_router)
    seg_off, tile_expert, occupied = _moe_layout(cnt, n_tiles)
    xs = _dispatch(x, post, seg_off, cnt, n_tiles * MOE_ROW_TILE)
    ys = _moe_ffn(xs, tile_expert, occupied, w_gate, w_up, w_down)
    return _combine(x, ys, cw, pos, seg_off, cnt, g, b)


def kernel(x, emb_ln_g, emb_ln_b, w_in, b_gate, na_rpb, sw_sink, w_branch_na, w_branch_sw, w_out,
           ln1_g, ln1_b, ffn_w_gate, ffn_w_up, ffn_w_down, moe_router, moe_w_gate, moe_w_up,
           moe_w_down, ln2_g, ln2_b):
    batch, seq, d = x.shape
    assert (seq, d) == (SEQ, D_MODEL)
    n = batch * seq
    tables = _rotary_tables()
    h = _ln(x.reshape(n, d), emb_ln_g, emb_ln_b)
    for layer in range(DEPTH):
        q_na, k_na, v_na, q_sw, k_sw, v_sw, gates = _proj(
            h, w_in[layer].astype(BF16), b_gate[layer], tables)
        seq3 = lambda t: t.reshape(batch, seq, t.shape[-1])
        y_na = _na_attention(seq3(q_na), seq3(k_na), seq3(v_na), _na_bias_tables(na_rpb[layer]))
        y_sw = _sw_attention(seq3(q_sw), seq3(k_sw), seq3(v_sw), sw_sink[layer])
        h = _merge(h, y_na.reshape(n, -1), y_sw.reshape(n, -1), gates,
                   w_branch_na[layer].astype(BF16), w_branch_sw[layer].astype(BF16),
                   w_out[layer].astype(BF16), ln1_g[layer], ln1_b[layer])
        i = layer // 2
        if layer % 2 == 0:
            h = _ffn(h, ffn_w_gate[i].astype(BF16), ffn_w_up[i].astype(BF16),
                     ffn_w_down[i].astype(BF16), ln2_g[layer], ln2_b[layer])
        else:
            h = _moe(h, moe_router[i], moe_w_gate[i], moe_w_up[i], moe_w_down[i],
                     ln2_g[layer], ln2_b[layer])
    return h.reshape(batch, seq, d)
```

```python
import functools

import numpy as np
import jax
import jax.numpy as jnp
from jax import lax
from jax.experimental import pallas as pl
from jax.experimental.pallas import tpu as pltpu

F32 = jnp.float32
BF16 = jnp.bfloat16

D_MODEL = 1024
SEQ = 2048
DEPTH = 2
HEAD_DIM = 64
NA_HEADS = 8
NA_WIDTH = NA_HEADS * HEAD_DIM
GRID_W = 64
GRID_ROWS = SEQ // GRID_W
NA_KH = 8
NA_KW = 16
SW_HEADS = 8
SW_KV_HEADS = 2
SW_GROUP = SW_HEADS // SW_KV_HEADS
SW_WIDTH = SW_HEADS * HEAD_DIM
SW_KV_WIDTH = SW_KV_HEADS * HEAD_DIM
SW_WINDOW = 128
SW_BLOCK = 128
ROT_DIM = HEAD_DIM // 4
ROPE_THETA = 500000.0
OFF_QNA = NA_WIDTH
OFF_KNA = 2 * NA_WIDTH
OFF_VNA = 3 * NA_WIDTH
OFF_QSW = OFF_VNA + SW_WIDTH
OFF_KSW = OFF_QSW + SW_KV_WIDTH
OFF_VSW = OFF_KSW + SW_KV_WIDTH
PROJ_COLS = OFF_VSW + 2 * D_MODEL
N_EXPERTS = 8
DEEPNORM_ALPHA = (2 * DEPTH) ** 0.25
LN_EPS = 1e-5
NEG_INF = -1e30
QK_SCALE = HEAD_DIM ** -0.5

LANES = 128
MXU_DIM = 256
V7X_VMEM_BYTES = 64 * 1024 * 1024
VMEM_LIMIT = V7X_VMEM_BYTES * 7 // 8

ROW_TILE = 512
FFN_ROW_TILE = 1024
DENSE_FF_TILE = 1408
MOE_ROW_TILE = 1536
MOE_FF_TILE = 512
MOE_CHUNK = 256
SEG_ALIGN = 16
NA_ROWS_PER_STEP = 4
SW_BLOCKS_PER_STEP = 4


def _layer_norm(z, g, b):
    mu = jnp.mean(z, axis=-1, keepdims=True)
    d = z - mu
    var = jnp.mean(d * d, axis=-1, keepdims=True)
    return d * lax.rsqrt(var + LN_EPS) * g + b


def _sigmoid(z):
    return 1.0 / (1.0 + jnp.exp(-z))


def _params(*sem):
    return pltpu.CompilerParams(dimension_semantics=sem, vmem_limit_bytes=VMEM_LIMIT)


def _ln_kernel(x_ref, g_ref, b_ref, o_ref):
    o_ref[...] = _layer_norm(x_ref[...], g_ref[...], b_ref[...])


def _ln(x, g, b):
    n = x.shape[0]
    row = pl.BlockSpec((ROW_TILE, D_MODEL), lambda i: (i, 0))
    vec = pl.BlockSpec((1, D_MODEL), lambda i: (0, 0))
    return pl.pallas_call(
        _ln_kernel, grid=(n // ROW_TILE,), in_specs=[row, vec, vec], out_specs=row,
        out_shape=jax.ShapeDtypeStruct((n, D_MODEL), F32), compiler_params=_params("parallel"),
        name="embed_ln",
    )(x, g.reshape(1, -1), b.reshape(1, -1))


def _rotary_tables():
    half = ROT_DIM // 2
    inv_freq = 1.0 / (ROPE_THETA ** (jnp.arange(0, ROT_DIM, 2, dtype=F32) / ROT_DIM))
    ang = jnp.arange(SEQ, dtype=jnp.int32).astype(F32)[:, None] * inv_freq[None, :]
    cos, sin = jnp.cos(ang), jnp.sin(ang)
    ones = jnp.ones((SEQ, HEAD_DIM - ROT_DIM), F32)
    zeros = jnp.zeros((SEQ, HEAD_DIM - ROT_DIM), F32)
    zh = jnp.zeros((SEQ, half), F32)
    cos_h = jnp.concatenate([cos, cos, ones], axis=1)
    sa_h = jnp.concatenate([-sin, zh, zeros], axis=1)
    sb_h = jnp.concatenate([zh, sin, zeros], axis=1)
    two = lambda t: jnp.concatenate([t, t], axis=1)
    return two(cos_h), two(sa_h), two(sb_h)


def _proj_kernel(x_ref, w_ref, bg_ref, cos_ref, sa_ref, sb_ref,
                 qna_ref, kna_ref, vna_ref, qsw_ref, ksw_ref, vsw_ref, gate_ref):
    xb = x_ref[...].astype(BF16)

    def mm(lo, hi):
        return jnp.dot(xb, w_ref[:, lo:hi], preferred_element_type=F32)

    qna_ref[...] = (mm(0, OFF_QNA) * QK_SCALE).astype(BF16)
    kna_ref[...] = mm(OFF_QNA, OFF_KNA).astype(BF16)
    vna_ref[...] = mm(OFF_KNA, OFF_VNA).astype(BF16)

    cos, sa, sb = cos_ref[...], sa_ref[...], sb_ref[...]
    half = ROT_DIM // 2

    def rot(t):
        return t * cos + pltpu.roll(t, LANES - half, 1) * sa + pltpu.roll(t, half, 1) * sb

    q = mm(OFF_VNA, OFF_QSW)
    for c in range(SW_WIDTH // LANES):
        sl = slice(c * LANES, (c + 1) * LANES)
        qsw_ref[:, sl] = (rot(q[:, sl]) * QK_SCALE).astype(BF16)

    kv = mm(OFF_QSW, OFF_VSW)
    k = rot(kv[:, :SW_KV_WIDTH])
    v = kv[:, SW_KV_WIDTH:]
    first = lax.broadcasted_iota(jnp.int32, k.shape, 1) < HEAD_DIM

    def dup(t):
        r = pltpu.roll(t, HEAD_DIM, 1)
        return jnp.concatenate([jnp.where(first, t, r), jnp.where(first, r, t)], axis=1)

    ksw_ref[...] = dup(k).astype(BF16)
    vsw_ref[...] = v.astype(BF16)

    gw = 512
    for c in range(2 * D_MODEL // gw):
        z = mm(OFF_VSW + c * gw, OFF_VSW + (c + 1) * gw) + bg_ref[:, c * gw:(c + 1) * gw]
        gate_ref[:, c * gw:(c + 1) * gw] = _sigmoid(z).astype(BF16)


def _proj(x, w_bf, b_gate, tables):
    n = x.shape[0]
    tiles_per_seq = SEQ // ROW_TILE
    row = lambda w: pl.BlockSpec((ROW_TILE, w), lambda i: (i, 0))
    full = lambda a: pl.BlockSpec(a.shape, lambda i: (0,) * a.ndim)
    tab = pl.BlockSpec((ROW_TILE, LANES), lambda i: (i % tiles_per_seq, 0))
    bg = b_gate.reshape(1, -1)
    widths = (NA_WIDTH, NA_WIDTH, NA_WIDTH, SW_WIDTH, 2 * SW_KV_WIDTH, SW_KV_WIDTH, 2 * D_MODEL)
    return pl.pallas_call(
        _proj_kernel, grid=(n // ROW_TILE,),
        in_specs=[row(D_MODEL), full(w_bf), full(bg), tab, tab, tab],
        out_specs=[row(w) for w in widths],
        out_shape=[jax.ShapeDtypeStruct((n, w), BF16) for w in widths],
        compiler_params=_params("parallel"), name="in_proj",
    )(x, w_bf, bg, *tables)


def _na_bias_tables(rpb):
    c = np.arange(GRID_W)
    qcs = np.clip(c - NA_KW // 2, 0, GRID_W - NA_KW)
    valid = (c[None, :] >= qcs[:, None]) & (c[None, :] < qcs[:, None] + NA_KW)
    n_dr, n_dc = 2 * NA_KH - 1, 2 * NA_KW - 1
    span = 2 * GRID_W - 1
    lead = GRID_W - NA_KW
    ext = jnp.pad(rpb.astype(F32), ((0, 0), (0, 0), (lead, span + 1 - lead - n_dc)))
    flat = jnp.broadcast_to(ext[:, :, None, :], (NA_HEADS, n_dr, GRID_W, span + 1))
    flat = flat.reshape(NA_HEADS, n_dr, GRID_W * (span + 1))[:, :, :GRID_W * span]
    t = flat.reshape(NA_HEADS, n_dr, GRID_W, span)[:, :, :, GRID_W - 1:]
    t = jnp.where(valid[None, None], t, NEG_INF)
    b = jnp.stack([t[:, v:v + NA_KH] for v in range(NA_KH)], axis=0)
    b = b.transpose(0, 1, 3, 2, 4)
    return b.reshape(NA_KH, 2, 4 * GRID_W, NA_KH * GRID_W)


def _na_kernel(q_ref, k_ref, v_ref, bias_ref, o_ref):
    lane = lax.broadcasted_iota(jnp.int32, (GRID_W, MXU_DIM), 1)
    head_mask = [(lane >= h * HEAD_DIM) & (lane < (h + 1) * HEAD_DIM) for h in range(4)]
    nk = NA_KH * GRID_W

    def rows(it, carry):
        chains = []
        for u in range(NA_ROWS_PER_STEP):
            r = it * NA_ROWS_PER_STEP + u
            rs = jnp.clip(r - NA_KH // 2, 0, GRID_ROWS - NA_KH)
            variant = rs - r + (NA_KH - 1)
            qrows = pl.ds(pl.multiple_of(r * GRID_W, GRID_W), GRID_W)
            krows = pl.ds(pl.multiple_of(rs * GRID_W, GRID_W), nk)
            q = q_ref[qrows, :]
            for g in range(2):
                gs = slice(g * MXU_DIM, (g + 1) * MXU_DIM)
                qg = q[:, gs]
                lhs = jnp.concatenate([jnp.where(m, qg, jnp.zeros_like(qg)) for m in head_mask], axis=0)
                chains.append(dict(lhs=lhs, k=k_ref[krows, gs], v=v_ref[krows, gs],
                                   bias=bias_ref[variant, g], out=(qrows, gs)))
        for c in chains:
            c["s"] = lax.dot_general(c["lhs"], c["k"], (((1,), (1,)), ((), ())),
                                     preferred_element_type=F32) + c["bias"]
        for c in chains:
            c["p"] = jnp.exp(c["s"] - jnp.max(c["s"], axis=-1, keepdims=True))
        for c in chains:
            c["l"] = jnp.sum(c["p"], axis=-1, keepdims=True)
        for c in chains:
            c["o"] = jnp.dot(c["p"].astype(BF16), c["v"], preferred_element_type=F32) / c["l"]
        for c in chains:
            og = jnp.zeros((GRID_W, MXU_DIM), F32)
            for h in range(4):
                og = og + jnp.where(head_mask[h], c["o"][h * GRID_W:(h + 1) * GRID_W], 0.0)
            o_ref[c["out"]] = og.astype(BF16)
        return carry

    lax.fori_loop(0, GRID_ROWS // NA_ROWS_PER_STEP, rows, 0)


def _na_attention(q, k, v, bias):
    b = q.shape[0]
    seq = pl.BlockSpec((None, SEQ, NA_WIDTH), lambda i: (i, 0, 0))
    return pl.pallas_call(
        _na_kernel, grid=(b,),
        in_specs=[seq, seq, seq, pl.BlockSpec(bias.shape, lambda i: (0, 0, 0, 0))],
        out_specs=seq, out_shape=jax.ShapeDtypeStruct((b, SEQ, NA_WIDTH), BF16),
        compiler_params=_params("parallel"), name="na_attention",
    )(q, k, v, bias)


def _sw_masks():
    nkeys = 3 * SW_BLOCK
    jk = np.arange(nkeys)[:, None]
    iq = np.arange(SW_BLOCK)[None, :]
    band = (jk >= iq) & (jk <= iq + 2 * SW_WINDOW)
    in_seq = [(jk >= SW_BLOCK), np.ones_like(band), (jk < 2 * SW_BLOCK)]
    return jnp.asarray(np.stack([np.where(band & ok, 0.0, NEG_INF) for ok in in_seq]), F32)


def _sw_kernel(sink_ref, q_ref, k_ref, v_ref, mask_ref, o_ref):
    nb = SEQ // SW_BLOCK
    first = lax.broadcasted_iota(jnp.int32, (SW_BLOCK, LANES), 1) < HEAD_DIM

    def blocks(it, carry):
        chains = []
        for u in range(SW_BLOCKS_PER_STEP):
            n = it * SW_BLOCKS_PER_STEP + u
            mask = mask_ref[jnp.where(n == 0, 0, jnp.where(n == nb - 1, 2, 1))]
            starts = [pl.multiple_of(jnp.clip(n + d, 0, nb - 1) * SW_BLOCK, SW_BLOCK) for d in (-1, 0, 1)]
            qrows = pl.ds(pl.multiple_of(n * SW_BLOCK, SW_BLOCK), SW_BLOCK)
            q = q_ref[qrows, :]
            vv = jnp.concatenate([v_ref[pl.ds(s, SW_BLOCK), :] for s in starts], axis=0)
            vt = vv.astype(F32).T.astype(BF16)
            for kvh in range(SW_KV_HEADS):
                ks = slice(kvh * LANES, (kvh + 1) * LANES)
                kk = jnp.concatenate([k_ref[pl.ds(s, SW_BLOCK), ks] for s in starts], axis=0)
                parts, sinks = [], []
                for j in range(SW_GROUP):
                    h = kvh * SW_GROUP + j
                    qc = q[:, (h // 2) * LANES:(h // 2 + 1) * LANES]
                    keep = first if h % 2 == 0 else jnp.logical_not(first)
                    parts.append(jnp.where(keep, qc, jnp.zeros_like(qc)))
                    sinks.append(jnp.full((1, SW_BLOCK), sink_ref[h], F32))
                chains.append(dict(k=kk, vt=vt[kvh * HEAD_DIM:(kvh + 1) * HEAD_DIM],
                                   q=jnp.concatenate(parts, axis=0), sink=jnp.concatenate(sinks, axis=1),
                                   mask=mask, qrows=qrows, kvh=kvh))
        for c in chains:
            s = lax.dot_general(c["k"], c["q"], (((1,), (1,)), ((), ())), preferred_element_type=F32)
            c["s"] = jnp.concatenate(
                [s[:, j * SW_BLOCK:(j + 1) * SW_BLOCK] + c["mask"] for j in range(SW_GROUP)], axis=1)
        for c in chains:
            c["m"] = jnp.maximum(jnp.max(c["s"], axis=0, keepdims=True), c["sink"])
        for c in chains:
            c["p"] = jnp.exp(c["s"] - c["m"])
        for c in chains:
            c["denom"] = jnp.sum(c["p"], axis=0, keepdims=True) + jnp.exp(c["sink"] - c["m"])
        for c in chains:
            c["o"] = jnp.dot(c["vt"], c["p"].astype(BF16), preferred_element_type=F32) / c["denom"]
        for c in chains:
            for pair in range(SW_GROUP // 2):
                even = c["o"][:, (2 * pair) * SW_BLOCK:(2 * pair + 1) * SW_BLOCK]
                odd = c["o"][:, (2 * pair + 1) * SW_BLOCK:(2 * pair + 2) * SW_BLOCK]
                col = (c["kvh"] * (SW_GROUP // 2) + pair) * LANES
                o_ref[c["qrows"], col:col + LANES] = jnp.concatenate([even, odd], axis=0).T.astype(BF16)
        return carry

    lax.fori_loop(0, nb // SW_BLOCKS_PER_STEP, blocks, 0)


def _sw_attention(q, k2, v, sink):
    b = q.shape[0]
    masks = _sw_masks()
    seq = lambda w: pl.BlockSpec((None, SEQ, w), lambda i: (i, 0, 0))
    return pl.pallas_call(
        _sw_kernel, grid=(b,),
        in_specs=[pl.BlockSpec(memory_space=pltpu.SMEM), seq(SW_WIDTH), seq(2 * SW_KV_WIDTH),
                  seq(SW_KV_WIDTH), pl.BlockSpec(masks.shape, lambda i: (0, 0, 0))],
        out_specs=seq(SW_WIDTH), out_shape=jax.ShapeDtypeStruct((b, SEQ, SW_WIDTH), BF16),
        compiler_params=_params("parallel"), name="sw_attention",
    )(sink.astype(F32), q, k2, v, masks)


def _merge_kernel(x_ref, yna_ref, ysw_ref, gate_ref, wna_ref, wsw_ref, wout_ref, g_ref, b_ref, o_ref):
    a = jnp.dot(yna_ref[...], wna_ref[...], preferred_element_type=F32)
    s = jnp.dot(ysw_ref[...], wsw_ref[...], preferred_element_type=F32)
    m = gate_ref[:, :D_MODEL].astype(F32) * a + gate_ref[:, D_MODEL:].astype(F32) * s
    z = DEEPNORM_ALPHA * x_ref[...] + jnp.dot(m.astype(BF16), wout_ref[...], preferred_element_type=F32)
    o_ref[...] = _layer_norm(z, g_ref[...], b_ref[...])


def _merge(x, y_na, y_sw, gates, w_na, w_sw, w_out, g, b):
    n = x.shape[0]
    row = lambda w: pl.BlockSpec((ROW_TILE, w), lambda i: (i, 0))
    full = lambda a: pl.BlockSpec(a.shape, lambda i: (0,) * a.ndim)
    g2, b2 = g.reshape(1, -1), b.reshape(1, -1)
    return pl.pallas_call(
        _merge_kernel, grid=(n // ROW_TILE,),
        in_specs=[row(D_MODEL), row(NA_WIDTH), row(SW_WIDTH), row(2 * D_MODEL),
                  full(w_na), full(w_sw), full(w_out), full(g2), full(b2)],
        out_specs=row(D_MODEL), out_shape=jax.ShapeDtypeStruct((n, D_MODEL), F32),
        compiler_params=_params("parallel"), name="merge_ln1",
    )(x, y_na, y_sw, gates, w_na, w_sw, w_out, g2, b2)


def _router_kernel(x_ref, w_ref, cw_ref, pos_ref, post_ref, cnt_ref):
    x, w = x_ref[...], w_ref[...]
    xh, wh = x.astype(BF16), w.astype(BF16)
    xl = (x - xh.astype(F32)).astype(BF16)
    wl = (w - wh.astype(F32)).astype(BF16)
    dot = functools.partial(jnp.dot, preferred_element_type=F32)
    logits = dot(xh, wh) + dot(xl, wh) + dot(xh, wl)
    lane = lax.broadcasted_iota(jnp.int32, logits.shape, 1).astype(F32)
    lg = jnp.where(lane < N_EXPERTS, logits, -jnp.inf)
    m1 = jnp.max(lg, axis=-1, keepdims=True)
    i1 = jnp.min(jnp.where(lg == m1, lane, float(LANES)), axis=-1, keepdims=True)
    lg2 = jnp.where(lane == i1, -jnp.inf, lg)
    m2 = jnp.max(lg2, axis=-1, keepdims=True)
    i2 = jnp.min(jnp.where(lg2 == m2, lane, float(LANES)), axis=-1, keepdims=True)
    e2 = jnp.exp(m2 - m1)
    denom = 1.0 + e2
    cw = jnp.where(lane == i1, 1.0 / denom, 0.0) + jnp.where(lane == i2, e2 / denom, 0.0)
    cw_ref[...] = cw

    t = cw.shape[0]
    sel = jnp.where(cw.T[:N_EXPERTS] > 0.0, 1.0, 0.0)
    tri = jnp.where(lax.broadcasted_iota(jnp.int32, (t, t), 0) <= lax.broadcasted_iota(jnp.int32, (t, t), 1),
                    1.0, 0.0).astype(BF16)
    incl = dot(sel.astype(BF16), tri)
    post = jnp.where(sel > 0.0, incl - 1.0, -1.0)
    post_ref[...] = post
    pos_ref[...] = jnp.concatenate([post, jnp.full((LANES - N_EXPERTS, t), -1.0, F32)], axis=0).T
    cnt_ref[...] = jnp.broadcast_to(incl[:, t - 1:t], (N_EXPERTS, LANES))


def _router(x, w_router):
    n = x.shape[0]
    nblk = n // ROW_TILE
    w = jnp.zeros((D_MODEL, LANES), F32).at[:, :N_EXPERTS].set(w_router)
    tok = pl.BlockSpec((ROW_TILE, LANES), lambda i: (i, 0))
    cw, pos, post, cnt = pl.pallas_call(
        _router_kernel, grid=(nblk,),
        in_specs=[pl.BlockSpec((ROW_TILE, D_MODEL), lambda i: (i, 0)),
                  pl.BlockSpec((D_MODEL, LANES), lambda i: (0, 0))],
        out_specs=[tok, tok, pl.BlockSpec((N_EXPERTS, ROW_TILE), lambda i: (0, i)),
                   pl.BlockSpec((N_EXPERTS, LANES), lambda i: (i, 0))],
        out_shape=[jax.ShapeDtypeStruct((n, LANES), F32), jax.ShapeDtypeStruct((n, LANES), F32),
                   jax.ShapeDtypeStruct((N_EXPERTS, n), F32),
                   jax.ShapeDtypeStruct((nblk * N_EXPERTS, LANES), F32)],
        compiler_params=_params("parallel"), name="router_top2",
    )(x, w)
    return cw, pos, post, cnt[:, 0].astype(jnp.int32).reshape(nblk, N_EXPERTS)


def _moe_layout(cnt, n_tiles):
    seg = (cnt + SEG_ALIGN - 1) // SEG_ALIGN * SEG_ALIGN
    rows_e = jnp.sum(seg, axis=0)
    tiles_e = (rows_e + MOE_CHUNK + MOE_ROW_TILE - 1) // MOE_ROW_TILE
    tile_end = jnp.cumsum(tiles_e)
    tile_off = tile_end - tiles_e
    seg_off = (tile_off * MOE_ROW_TILE)[None, :] + jnp.cumsum(seg, axis=0) - seg
    tile = jnp.arange(n_tiles, dtype=jnp.int32)
    tile_expert = jnp.minimum(jnp.sum(tile[:, None] >= tile_end[None, :], axis=1), N_EXPERTS - 1)
    occupied = jnp.clip(rows_e[tile_expert] - (tile - tile_off[tile_expert]) * MOE_ROW_TILE, 0, MOE_ROW_TILE)
    as_i32 = lambda t: t.astype(jnp.int32)
    return as_i32(seg_off).reshape(-1), as_i32(tile_expert), as_i32(occupied)


def _chunk_copies(off_ref, cnt_ref, hbm_ref, vmem_ref, sems, to_hbm):
    b = pl.program_id(0)
    out = []
    for e in range(N_EXPERTS):
        c = cnt_ref[b * N_EXPERTS + e]
        off = pl.multiple_of(off_ref[b * N_EXPERTS + e], SEG_ALIGN)
        for k in range(ROW_TILE // MOE_CHUNK):
            idx = k * N_EXPERTS + e
            rows = hbm_ref.at[pl.ds(off + k * MOE_CHUNK, MOE_CHUNK)]
            src, dst = (vmem_ref.at[idx], rows) if to_hbm else (rows, vmem_ref.at[idx])
            out.append((c > k * MOE_CHUNK, e, k, idx, pltpu.make_async_copy(src, dst, sems.at[idx])))
    return out


def _dispatch_kernel(off_ref, cnt_ref, x_ref, post_ref, xs_in_ref, xs_ref, stage_ref, sems):
    del xs_in_ref
    xb = x_ref[...].astype(BF16)
    slot = lax.broadcasted_iota(jnp.int32, (MOE_CHUNK, ROW_TILE), 0).astype(F32)
    copies = _chunk_copies(off_ref, cnt_ref, xs_ref, stage_ref, sems, to_hbm=True)
    for pred, e, k, idx, copy in copies:
        @pl.when(pred)
        def _(e=e, k=k, idx=idx, copy=copy):
            onehot = jnp.where(post_ref[e:e + 1, :] == slot + float(k * MOE_CHUNK), 1.0, 0.0).astype(BF16)
            stage_ref[idx] = jnp.dot(onehot, xb, preferred_element_type=F32).astype(BF16)
            copy.start()
    for pred, _, _, _, copy in copies:
        @pl.when(pred)
        def _(copy=copy):
            copy.wait()


def _dispatch(x, post, seg_off, cnt, n_rows):
    n = x.shape[0]
    n_chunks = N_EXPERTS * (ROW_TILE // MOE_CHUNK)
    grid_spec = pltpu.PrefetchScalarGridSpec(
        num_scalar_prefetch=2, grid=(n // ROW_TILE,),
        in_specs=[pl.BlockSpec((ROW_TILE, D_MODEL), lambda i, *_: (i, 0)),
                  pl.BlockSpec((N_EXPERTS, ROW_TILE), lambda i, *_: (0, i)),
                  pl.BlockSpec(memory_space=pl.ANY)],
        out_specs=pl.BlockSpec(memory_space=pl.ANY),
        scratch_shapes=[pltpu.VMEM((n_chunks, MOE_CHUNK, D_MODEL), BF16),
                        pltpu.SemaphoreType.DMA((n_chunks,))])
    return pl.pallas_call(
        _dispatch_kernel, grid_spec=grid_spec,
        out_shape=jax.ShapeDtypeStruct((n_rows, D_MODEL), BF16),
        input_output_aliases={4: 0},
        compiler_params=_params("arbitrary"), name="moe_dispatch",
    )(seg_off, cnt.reshape(-1), x, post, jnp.zeros((n_rows, D_MODEL), BF16))


def _swiglu_partial(xb, wg, wu, wd):
    hg = jnp.dot(xb, wg, preferred_element_type=F32)
    hu = jnp.dot(xb, wu, preferred_element_type=F32)
    h = hg * _sigmoid(hg) * hu
    return jnp.dot(h.astype(BF16), wd, preferred_element_type=F32)


def _moe_ffn_kernel(expert_ref, occ_ref, x_ref, wg_ref, wu_ref, wd_ref, o_ref, acc_ref):
    del expert_ref
    i, j = pl.program_id(0), pl.program_id(1)
    occupied = occ_ref[i]

    @pl.when(j == 0)
    def _():
        acc_ref[...] = jnp.zeros_like(acc_ref)

    def weights():
        return wg_ref[...].astype(BF16), wu_ref[...].astype(BF16), wd_ref[...].astype(BF16)

    @pl.when(occupied == MOE_ROW_TILE)
    def _():
        acc_ref[...] += _swiglu_partial(x_ref[...], *weights())

    @pl.when((occupied > 0) & (occupied < MOE_ROW_TILE))
    def _():
        wg, wu, wd = weights()
        for s in range(MOE_ROW_TILE // MOE_CHUNK):
            rows = slice(s * MOE_CHUNK, (s + 1) * MOE_CHUNK)

            @pl.when(s * MOE_CHUNK < occupied)
            def _(rows=rows):
                acc_ref[rows, :] += _swiglu_partial(x_ref[rows, :], wg, wu, wd)

    @pl.when(j == pl.num_programs(1) - 1)
    def _():
        o_ref[...] = acc_ref[...].astype(BF16)


def _moe_ffn(xs, tile_expert, occupied, w_gate, w_up, w_down):
    n_rows = xs.shape[0]
    d_ff = w_gate.shape[-1]
    row = pl.BlockSpec((MOE_ROW_TILE, D_MODEL), lambda i, j, *_: (i, 0))
    up = pl.BlockSpec((None, D_MODEL, MOE_FF_TILE), lambda i, j, ex, occ: (ex[i], 0, j))
    down = pl.BlockSpec((None, MOE_FF_TILE, D_MODEL), lambda i, j, ex, occ: (ex[i], j, 0))
    grid_spec = pltpu.PrefetchScalarGridSpec(
        num_scalar_prefetch=2, grid=(n_rows // MOE_ROW_TILE, d_ff // MOE_FF_TILE),
        in_specs=[row, up, up, down], out_specs=row,
        scratch_shapes=[pltpu.VMEM((MOE_ROW_TILE, D_MODEL), F32)])
    return pl.pallas_call(
        _moe_ffn_kernel, grid_spec=grid_spec,
        out_shape=jax.ShapeDtypeStruct((n_rows, D_MODEL), BF16),
        compiler_params=_params("parallel", "arbitrary"), name="moe_swiglu",
    )(tile_expert, occupied, xs, w_gate, w_up, w_down)


def _ffn_kernel(x_ref, wg_ref, wu_ref, wd_ref, g_ref, b_ref, o_ref, xb_ref, acc_ref):
    j = pl.program_id(1)

    @pl.when(j == 0)
    def _():
        xb_ref[...] = x_ref[...].astype(BF16)
        acc_ref[...] = jnp.zeros_like(acc_ref)

    acc_ref[...] += _swiglu_partial(xb_ref[...], wg_ref[...], wu_ref[...], wd_ref[...])

    @pl.when(j == pl.num_programs(1) - 1)
    def _():
        o_ref[...] = _layer_norm(DEEPNORM_ALPHA * x_ref[...] + acc_ref[...], g_ref[...], b_ref[...])


def _ffn(x, w_gate, w_up, w_down, g, b):
    n = x.shape[0]
    d_ff = w_gate.shape[-1]
    tm = FFN_ROW_TILE
    row = pl.BlockSpec((tm, D_MODEL), lambda i, j: (i, 0))
    vec = pl.BlockSpec((1, D_MODEL), lambda i, j: (0, 0))
    up = pl.BlockSpec((D_MODEL, DENSE_FF_TILE), lambda i, j: (0, j))
    down = pl.BlockSpec((DENSE_FF_TILE, D_MODEL), lambda i, j: (j, 0))
    return pl.pallas_call(
        _ffn_kernel, grid=(n // tm, d_ff // DENSE_FF_TILE),
        in_specs=[row, up, up, down, vec, vec], out_specs=row,
        out_shape=jax.ShapeDtypeStruct((n, D_MODEL), F32),
        scratch_shapes=[pltpu.VMEM((tm, D_MODEL), BF16), pltpu.VMEM((tm, D_MODEL), F32)],
        compiler_params=_params("parallel", "arbitrary"), name="swiglu_ln2",
    )(x, w_gate, w_up, w_down, g.reshape(1, -1), b.reshape(1, -1))


def _combine_kernel(off_ref, cnt_ref, x_ref, cw_ref, pos_ref, g_ref, b_ref, ys_ref, o_ref,
                    buf_ref, acc_ref, sems):
    copies = _chunk_copies(off_ref, cnt_ref, ys_ref, buf_ref, sems, to_hbm=False)
    for pred, _, k, _, copy in copies:
        if k == 0:
            copy.start()
        else:
            pl.when(pred)(copy.start)

    lane = lax.broadcasted_iota(jnp.int32, (ROW_TILE, LANES), 1)
    slot = lax.broadcasted_iota(jnp.int32, (ROW_TILE, MOE_CHUNK), 1).astype(F32)
    column = lambda ref, e: jnp.sum(jnp.where(lane == e, ref[...], 0.0), axis=-1, keepdims=True)
    weight = [column(cw_ref, e) for e in range(N_EXPERTS)]
    pos = [column(pos_ref, e) for e in range(N_EXPERTS)]

    def scatter_matrix(e, k):
        return jnp.where(pos[e] == slot + float(k * MOE_CHUNK), weight[e], 0.0).astype(BF16)

    first = jnp.concatenate([scatter_matrix(e, 0) for e in range(N_EXPERTS)], axis=1)
    for _, _, k, _, copy in copies:
        if k == 0:
            copy.wait()
    acc_ref[...] = jnp.dot(first, buf_ref[0:N_EXPERTS].reshape(N_EXPERTS * MOE_CHUNK, D_MODEL),
                           preferred_element_type=F32)
    for pred, e, k, idx, copy in copies:
        if k > 0:
            @pl.when(pred)
            def _(e=e, k=k, idx=idx, copy=copy):
                copy.wait()
                acc_ref[...] += jnp.dot(scatter_matrix(e, k), buf_ref[idx], preferred_element_type=F32)

    o_ref[...] = _layer_norm(DEEPNORM_ALPHA * x_ref[...] + acc_ref[...], g_ref[...], b_ref[...])


def _combine(x, ys, cw, pos, seg_off, cnt, g, b):
    n = x.shape[0]
    n_chunks = N_EXPERTS * (ROW_TILE // MOE_CHUNK)
    tok = lambda w: pl.BlockSpec((ROW_TILE, w), lambda i, *_: (i, 0))
    vec = pl.BlockSpec((1, D_MODEL), lambda i, *_: (0, 0))
    grid_spec = pltpu.PrefetchScalarGridSpec(
        num_scalar_prefetch=2, grid=(n // ROW_TILE,),
        in_specs=[tok(D_MODEL), tok(LANES), tok(LANES), vec, vec, pl.BlockSpec(memory_space=pl.ANY)],
        out_specs=tok(D_MODEL),
        scratch_shapes=[pltpu.VMEM((n_chunks, MOE_CHUNK, D_MODEL), BF16),
                        pltpu.VMEM((ROW_TILE, D_MODEL), F32),
                        pltpu.SemaphoreType.DMA((n_chunks,))])
    return pl.pallas_call(
        _combine_kernel, grid_spec=grid_spec,
        out_shape=jax.ShapeDtypeStruct((n, D_MODEL), F32),
        compiler_params=_params("parallel"), name="moe_combine_ln2",
    )(seg_off, cnt.reshape(-1), x, cw, pos, g.reshape(1, -1), b.reshape(1, -1), ys)


def _moe(x, w_router, w_gate, w_up, w_down, g, b):
    n = x.shape[0]
    nblk = n // ROW_TILE
    max_rows = 2 * n + nblk * N_EXPERTS * (SEG_ALIGN - 1) + N_EXPERTS * (MOE_CHUNK + MOE_ROW_TILE - 1)
    n_tiles = max_rows // MOE_ROW_TILE
    cw, pos, post, cnt = _router(x, w_router)
    seg_off, tile_expert, occupied = _moe_layout(cnt, n_tiles)
    xs = _dispatch(x, post, seg_off, cnt, n_tiles * MOE_ROW_TILE)
    ys = _moe_ffn(xs, tile_expert, occupied, w_gate, w_up, w_down)
    return _combine(x, ys, cw, pos, seg_off, cnt, g, b)


def kernel(x, emb_ln_g, emb_ln_b, w_in, b_gate, na_rpb, sw_sink, w_branch_na, w_branch_sw, w_out,
           ln1_g, ln1_b, ffn_w_gate, ffn_w_up, ffn_w_down, moe_router, moe_w_gate, moe_w_up,
           moe_w_down, ln2_g, ln2_b):
    batch, seq, d = x.shape
    assert (seq, d) == (SEQ, D_MODEL)
    n = batch * seq
    tables = _rotary_tables()
    h = _ln(x.reshape(n, d), emb_ln_g, emb_ln_b)
    for layer in range(DEPTH):
        q_na, k_na, v_na, q_sw, k_sw, v_sw, gates = _proj(
            h, w_in[layer].astype(BF16), b_gate[layer], tables)
        seq3 = lambda t: t.reshape(batch, seq, t.shape[-1])
        y_na = _na_attention(seq3(q_na), seq3(k_na), seq3(v_na), _na_bias_tables(na_rpb[layer]))
        y_sw = _sw_attention(seq3(q_sw), seq3(k_sw), seq3(v_sw), sw_sink[layer])
        h = _merge(h, y_na.reshape(n, -1), y_sw.reshape(n, -1), gates,
                   w_branch_na[layer].astype(BF16), w_branch_sw[layer].astype(BF16),
                   w_out[layer].astype(BF16), ln1_g[layer], ln1_b[layer])
        i = layer // 2
        if layer % 2 == 0:
            h = _ffn(h, ffn_w_gate[i].astype(BF16), ffn_w_up[i].astype(BF16),
                     ffn_w_down[i].astype(BF16), ln2_g[layer], ln2_b[layer])
        else:
            h = _moe(h, moe_router[i], moe_w_gate[i], moe_w_up[i], moe_w_down[i],
                     ln2_g[layer], ln2_b[layer])
    return h.reshape(batch, seq, d)
```

```python
import functools

import numpy as np
import jax
import jax.numpy as jnp
from jax import lax
from jax.experimental import pallas as pl
from jax.experimental.pallas import tpu as pltpu

F32 = jnp.float32
BF16 = jnp.bfloat16

D_MODEL = 1024
SEQ = 2048
DEPTH = 2
HEAD_DIM = 64
NA_HEADS = 8
NA_WIDTH = NA_HEADS * HEAD_DIM
GRID_W = 64
GRID_ROWS = SEQ // GRID_W
NA_KH = 8
NA_KW = 16
SW_HEADS = 8
SW_KV_HEADS = 2
SW_GROUP = SW_HEADS // SW_KV_HEADS
SW_WIDTH = SW_HEADS * HEAD_DIM
SW_KV_WIDTH = SW_KV_HEADS * HEAD_DIM
SW_WINDOW = 128
SW_BLOCK = 128
ROT_DIM = HEAD_DIM // 4
ROPE_THETA = 500000.0
OFF_QNA = NA_WIDTH
OFF_KNA = 2 * NA_WIDTH
OFF_VNA = 3 * NA_WIDTH
OFF_QSW = OFF_VNA + SW_WIDTH
OFF_KSW = OFF_QSW + SW_KV_WIDTH
OFF_VSW = OFF_KSW + SW_KV_WIDTH
PROJ_COLS = OFF_VSW + 2 * D_MODEL
N_EXPERTS = 8
DEEPNORM_ALPHA = (2 * DEPTH) ** 0.25
LN_EPS = 1e-5
NEG_INF = -1e30
QK_SCALE = HEAD_DIM ** -0.5

LANES = 128
MXU_DIM = 256
V7X_VMEM_BYTES = 64 * 1024 * 1024
VMEM_LIMIT = V7X_VMEM_BYTES * 7 // 8

ROW_TILE = 512
FFN_ROW_TILE = 1024
DENSE_FF_TILE = 1408
MOE_ROW_TILE = 1536
MOE_FF_TILE = 512
MOE_CHUNK = 256
SEG_ALIGN = 16
NA_ROWS_PER_STEP = 4
SW_BLOCKS_PER_STEP = 4


def _layer_norm(z, g, b):
    mu = jnp.mean(z, axis=-1, keepdims=True)
    d = z - mu
    var = jnp.mean(d * d, axis=-1, keepdims=True)
    return d * lax.rsqrt(var + LN_EPS) * g + b


def _sigmoid(z):
    return 1.0 / (1.0 + jnp.exp(-z))


def _params(*sem):
    return pltpu.CompilerParams(dimension_semantics=sem, vmem_limit_bytes=VMEM_LIMIT)


def _rotary_tables():
    half = ROT_DIM // 2
    inv_freq = 1.0 / (ROPE_THETA ** (jnp.arange(0, ROT_DIM, 2, dtype=F32) / ROT_DIM))
    ang = jnp.arange(SEQ, dtype=jnp.int32).astype(F32)[:, None] * inv_freq[None, :]
    cos, sin = jnp.cos(ang), jnp.sin(ang)
    ones = jnp.ones((SEQ, HEAD_DIM - ROT_DIM), F32)
    zeros = jnp.zeros((SEQ, HEAD_DIM - ROT_DIM), F32)
    zh = jnp.zeros((SEQ, half), F32)
    cos_h = jnp.concatenate([cos, cos, ones], axis=1)
    sa_h = jnp.concatenate([-sin, zh, zeros], axis=1)
    sb_h = jnp.concatenate([zh, sin, zeros], axis=1)
    two = lambda t: jnp.concatenate([t, t], axis=1)
    return two(cos_h), two(sa_h), two(sb_h)


def _proj_kernel(*refs, embed_ln):
    if embed_ln:
        x_ref, g_ref, b_ref, w_ref, bg_ref, cos_ref, sa_ref, sb_ref, xn_ref, *outs = refs
        x = _layer_norm(x_ref[...], g_ref[...], b_ref[...])
        xn_ref[...] = x
    else:
        x_ref, w_ref, bg_ref, cos_ref, sa_ref, sb_ref, *outs = refs
        x = x_ref[...]
    qna_ref, kna_ref, vna_ref, qsw_ref, ksw_ref, vsw_ref, gate_ref = outs
    xb = x.astype(BF16)

    def mm(lo, hi):
        return jnp.dot(xb, w_ref[:, lo:hi], preferred_element_type=F32)

    qna_ref[...] = (mm(0, OFF_QNA) * QK_SCALE).astype(BF16)
    kna_ref[...] = mm(OFF_QNA, OFF_KNA).astype(BF16)
    vna_ref[...] = mm(OFF_KNA, OFF_VNA).astype(BF16)

    cos, sa, sb = cos_ref[...], sa_ref[...], sb_ref[...]
    half = ROT_DIM // 2

    def rot(t):
        return t * cos + pltpu.roll(t, LANES - half, 1) * sa + pltpu.roll(t, half, 1) * sb

    q = mm(OFF_VNA, OFF_QSW)
    for c in range(SW_WIDTH // LANES):
        sl = slice(c * LANES, (c + 1) * LANES)
        qsw_ref[:, sl] = (rot(q[:, sl]) * QK_SCALE).astype(BF16)

    kv = mm(OFF_QSW, OFF_VSW)
    k = rot(kv[:, :SW_KV_WIDTH])
    v = kv[:, SW_KV_WIDTH:]
    first = lax.broadcasted_iota(jnp.int32, k.shape, 1) < HEAD_DIM

    def dup(t):
        r = pltpu.roll(t, HEAD_DIM, 1)
        return jnp.concatenate([jnp.where(first, t, r), jnp.where(first, r, t)], axis=1)

    ksw_ref[...] = dup(k).astype(BF16)
    vsw_ref[...] = v.astype(BF16)

    gw = 512
    for c in range(2 * D_MODEL // gw):
        z = mm(OFF_VSW + c * gw, OFF_VSW + (c + 1) * gw) + bg_ref[:, c * gw:(c + 1) * gw]
        gate_ref[:, c * gw:(c + 1) * gw] = _sigmoid(z).astype(BF16)


def _proj(x, w_bf, b_gate, tables, embed_ln=None):
    n = x.shape[0]
    tiles_per_seq = SEQ // ROW_TILE
    row = lambda w: pl.BlockSpec((ROW_TILE, w), lambda i: (i, 0))
    full = lambda a: pl.BlockSpec(a.shape, lambda i: (0,) * a.ndim)
    tab = pl.BlockSpec((ROW_TILE, LANES), lambda i: (i % tiles_per_seq, 0))
    bg = b_gate.reshape(1, -1)
    widths = (NA_WIDTH, NA_WIDTH, NA_WIDTH, SW_WIDTH, 2 * SW_KV_WIDTH, SW_KV_WIDTH, 2 * D_MODEL)
    ln = [t.reshape(1, -1) for t in embed_ln] if embed_ln else []
    out_specs = [row(w) for w in widths]
    out_shape = [jax.ShapeDtypeStruct((n, w), BF16) for w in widths]
    if embed_ln:
        out_specs = [row(D_MODEL)] + out_specs
        out_shape = [jax.ShapeDtypeStruct((n, D_MODEL), F32)] + out_shape
    outs = pl.pallas_call(
        functools.partial(_proj_kernel, embed_ln=bool(embed_ln)), grid=(n // ROW_TILE,),
        in_specs=[row(D_MODEL)] + [full(t) for t in ln] + [full(w_bf), full(bg), tab, tab, tab],
        out_specs=out_specs, out_shape=out_shape,
        compiler_params=_params("parallel"), name="in_proj",
    )(x, *ln, w_bf, bg, *tables)
    return outs if embed_ln else [x] + list(outs)


def _na_bias_tables(rpb):
    c = np.arange(GRID_W)
    qcs = np.clip(c - NA_KW // 2, 0, GRID_W - NA_KW)
    valid = (c[None, :] >= qcs[:, None]) & (c[None, :] < qcs[:, None] + NA_KW)
    n_dr, n_dc = 2 * NA_KH - 1, 2 * NA_KW - 1
    span = 2 * GRID_W - 1
    lead = GRID_W - NA_KW
    ext = jnp.pad(rpb.astype(F32), ((0, 0), (0, 0), (lead, span + 1 - lead - n_dc)))
    flat = jnp.broadcast_to(ext[:, :, None, :], (NA_HEADS, n_dr, GRID_W, span + 1))
    flat = flat.reshape(NA_HEADS, n_dr, GRID_W * (span + 1))[:, :, :GRID_W * span]
    t = flat.reshape(NA_HEADS, n_dr, GRID_W, span)[:, :, :, GRID_W - 1:]
    t = jnp.where(valid[None, None], t, NEG_INF)
    pairs = jnp.concatenate([t[:, :-1], t[:, 1:]], axis=-1)
    pairs = pairs.reshape(2, 4, n_dr - 1, GRID_W, 2 * GRID_W).transpose(0, 2, 1, 3, 4)
    return pairs.reshape(2, n_dr - 1, 4 * GRID_W, 2 * GRID_W)


def _na_kernel(q_ref, k_ref, v_ref, bias_ref, o_ref):
    lane = lax.broadcasted_iota(jnp.int32, (GRID_W, MXU_DIM), 1)
    head_mask = [(lane >= h * HEAD_DIM) & (lane < (h + 1) * HEAD_DIM) for h in range(4)]
    nk = NA_KH * GRID_W

    def rows(it, carry):
        chains = []
        for u in range(NA_ROWS_PER_STEP):
            r = it * NA_ROWS_PER_STEP + u
            rs = jnp.clip(r - NA_KH // 2, 0, GRID_ROWS - NA_KH)
            variant = rs - r + (NA_KH - 1)
            qrows = pl.ds(pl.multiple_of(r * GRID_W, GRID_W), GRID_W)
            krows = pl.ds(pl.multiple_of(rs * GRID_W, GRID_W), nk)
            q = q_ref[qrows, :]
            for g in range(2):
                gs = slice(g * MXU_DIM, (g + 1) * MXU_DIM)
                qg = q[:, gs]
                lhs = jnp.concatenate([jnp.where(m, qg, jnp.zeros_like(qg)) for m in head_mask], axis=0)
                bias = jnp.concatenate([bias_ref[g, variant + 2 * a] for a in range(NA_KH // 2)], axis=1)
                chains.append(dict(lhs=lhs, k=k_ref[krows, gs], v=v_ref[krows, gs], bias=bias,
                                   out=(qrows, gs)))
        for c in chains:
            c["s"] = lax.dot_general(c["lhs"], c["k"], (((1,), (1,)), ((), ())),
                                     preferred_element_type=F32) + c["bias"]
        for c in chains:
            c["p"] = jnp.exp(c["s"] - jnp.max(c["s"], axis=-1, keepdims=True))
        for c in chains:
            c["l"] = jnp.sum(c["p"], axis=-1, keepdims=True)
        for c in chains:
            c["o"] = jnp.dot(c["p"].astype(BF16), c["v"], preferred_element_type=F32) / c["l"]
        for c in chains:
            og = jnp.zeros((GRID_W, MXU_DIM), F32)
            for h in range(4):
                og = og + jnp.where(head_mask[h], c["o"][h * GRID_W:(h + 1) * GRID_W], 0.0)
            o_ref[c["out"]] = og.astype(BF16)
        return carry

    lax.fori_loop(0, GRID_ROWS // NA_ROWS_PER_STEP, rows, 0)


def _na_attention(q, k, v, bias):
    b = q.shape[0]
    seq = pl.BlockSpec((None, SEQ, NA_WIDTH), lambda i: (i, 0, 0))
    return pl.pallas_call(
        _na_kernel, grid=(b,),
        in_specs=[seq, seq, seq, pl.BlockSpec(bias.shape, lambda i: (0, 0, 0, 0))],
        out_specs=seq, out_shape=jax.ShapeDtypeStruct((b, SEQ, NA_WIDTH), BF16),
        compiler_params=_params("parallel"), name="na_attention",
    )(q, k, v, bias)


def _sw_masks():
    nkeys = 3 * SW_BLOCK
    jk = np.arange(nkeys)[:, None]
    iq = np.arange(SW_BLOCK)[None, :]
    band = (jk >= iq) & (jk <= iq + 2 * SW_WINDOW)
    in_seq = [(jk >= SW_BLOCK), np.ones_like(band), (jk < 2 * SW_BLOCK)]
    return jnp.asarray(np.stack([np.where(band & ok, 0.0, NEG_INF) for ok in in_seq]), F32)


def _sw_kernel(sink_ref, q_ref, k_ref, v_ref, mask_ref, o_ref):
    nb = SEQ // SW_BLOCK
    first = lax.broadcasted_iota(jnp.int32, (SW_BLOCK, LANES), 1) < HEAD_DIM

    def blocks(it, carry):
        chains = []
        for u in range(SW_BLOCKS_PER_STEP):
            n = it * SW_BLOCKS_PER_STEP + u
            mask = mask_ref[jnp.where(n == 0, 0, jnp.where(n == nb - 1, 2, 1))]
            starts = [pl.multiple_of(jnp.clip(n + d, 0, nb - 1) * SW_BLOCK, SW_BLOCK) for d in (-1, 0, 1)]
            qrows = pl.ds(pl.multiple_of(n * SW_BLOCK, SW_BLOCK), SW_BLOCK)
            q = q_ref[qrows, :]
            vv = jnp.concatenate([v_ref[pl.ds(s, SW_BLOCK), :] for s in starts], axis=0)
            vt = vv.astype(F32).T.astype(BF16)
            for kvh in range(SW_KV_HEADS):
                ks = slice(kvh * LANES, (kvh + 1) * LANES)
                kk = jnp.concatenate([k_ref[pl.ds(s, SW_BLOCK), ks] for s in starts], axis=0)
                parts, sinks = [], []
                for j in range(SW_GROUP):
                    h = kvh * SW_GROUP + j
                    qc = q[:, (h // 2) * LANES:(h // 2 + 1) * LANES]
                    keep = first if h % 2 == 0 else jnp.logical_not(first)
                    parts.append(jnp.where(keep, qc, jnp.zeros_like(qc)))
                    sinks.append(jnp.full((1, SW_BLOCK), sink_ref[h], F32))
                chains.append(dict(k=kk, vt=vt[kvh * HEAD_DIM:(kvh + 1) * HEAD_DIM],
                                   q=jnp.concatenate(parts, axis=0), sink=jnp.concatenate(sinks, axis=1),
                                   mask=mask, qrows=qrows, kvh=kvh))
        for c in chains:
            s = lax.dot_general(c["k"], c["q"], (((1,), (1,)), ((), ())), preferred_element_type=F32)
            c["s"] = jnp.concatenate(
                [s[:, j * SW_BLOCK:(j + 1) * SW_BLOCK] + c["mask"] for j in range(SW_GROUP)], axis=1)
        for c in chains:
            c["m"] = jnp.maximum(jnp.max(c["s"], axis=0, keepdims=True), c["sink"])
        for c in chains:
            c["p"] = jnp.exp(c["s"] - c["m"])
        for c in chains:
            c["denom"] = jnp.sum(c["p"], axis=0, keepdims=True) + jnp.exp(c["sink"] - c["m"])
        for c in chains:
            c["o"] = jnp.dot(c["vt"], c["p"].astype(BF16), preferred_element_type=F32) / c["denom"]
        for c in chains:
            for pair in range(SW_GROUP // 2):
                even = c["o"][:, (2 * pair) * SW_BLOCK:(2 * pair + 1) * SW_BLOCK]
                odd = c["o"][:, (2 * pair + 1) * SW_BLOCK:(2 * pair + 2) * SW_BLOCK]
                col = (c["kvh"] * (SW_GROUP // 2) + pair) * LANES
                o_ref[c["qrows"], col:col + LANES] = jnp.concatenate([even, odd], axis=0).T.astype(BF16)
        return carry

    lax.fori_loop(0, nb // SW_BLOCKS_PER_STEP, blocks, 0)


def _sw_attention(q, k2, v, sink):
    b = q.shape[0]
    masks = _sw_masks()
    seq = lambda w: pl.BlockSpec((None, SEQ, w), lambda i: (i, 0, 0))
    return pl.pallas_call(
        _sw_kernel, grid=(b,),
        in_specs=[pl.BlockSpec(memory_space=pltpu.SMEM), seq(SW_WIDTH), seq(2 * SW_KV_WIDTH),
                  seq(SW_KV_WIDTH), pl.BlockSpec(masks.shape, lambda i: (0, 0, 0))],
        out_specs=seq(SW_WIDTH), out_shape=jax.ShapeDtypeStruct((b, SEQ, SW_WIDTH), BF16),
        compiler_params=_params("parallel"), name="sw_attention",
    )(sink.astype(F32), q, k2, v, masks)


def _merge_kernel(x_ref, yna_ref, ysw_ref, gate_ref, wna_ref, wsw_ref, wout_ref, g_ref, b_ref, o_ref):
    a = jnp.dot(yna_ref[...], wna_ref[...], preferred_element_type=F32)
    s = jnp.dot(ysw_ref[...], wsw_ref[...], preferred_element_type=F32)
    m = gate_ref[:, :D_MODEL].astype(F32) * a + gate_ref[:, D_MODEL:].astype(F32) * s
    z = DEEPNORM_ALPHA * x_ref[...] + jnp.dot(m.astype(BF16), wout_ref[...], preferred_element_type=F32)
    o_ref[...] = _layer_norm(z, g_ref[...], b_ref[...])


def _merge(x, y_na, y_sw, gates, w_na, w_sw, w_out, g, b):
    n = x.shape[0]
    row = lambda w: pl.BlockSpec((ROW_TILE, w), lambda i: (i, 0))
    full = lambda a: pl.BlockSpec(a.shape, lambda i: (0,) * a.ndim)
    g2, b2 = g.reshape(1, -1), b.reshape(1, -1)
    return pl.pallas_call(
        _merge_kernel, grid=(n // ROW_TILE,),
        in_specs=[row(D_MODEL), row(NA_WIDTH), row(SW_WIDTH), row(2 * D_MODEL),
                  full(w_na), full(w_sw), full(w_out), full(g2), full(b2)],
        out_specs=row(D_MODEL), out_shape=jax.ShapeDtypeStruct((n, D_MODEL), F32),
        compiler_params=_params("parallel"), name="merge_ln1",
    )(x, y_na, y_sw, gates, w_na, w_sw, w_out, g2, b2)


def _router_kernel(x_ref, w_ref, cw_ref, pos_ref, post_ref, cnt_ref):
    x, w = x_ref[...], w_ref[...]
    xh, wh = x.astype(BF16), w.astype(BF16)
    xl = (x - xh.astype(F32)).astype(BF16)
    wl = (w - wh.astype(F32)).astype(BF16)
    dot = functools.partial(jnp.dot, preferred_element_type=F32)
    logits = dot(xh, wh) + dot(xl, wh) + dot(xh, wl)
    lane = lax.broadcasted_iota(jnp.int32, logits.shape, 1).astype(F32)
    lg = jnp.where(lane < N_EXPERTS, logits, -jnp.inf)
    m1 = jnp.max(lg, axis=-1, keepdims=True)
    i1 = jnp.min(jnp.where(lg == m1, lane, float(LANES)), axis=-1, keepdims=True)
    lg2 = jnp.where(lane == i1, -jnp.inf, lg)
    m2 = jnp.max(lg2, axis=-1, keepdims=True)
    i2 = jnp.min(jnp.where(lg2 == m2, lane, float(LANES)), axis=-1, keepdims=True)
    e2 = jnp.exp(m2 - m1)
    denom = 1.0 + e2
    cw = jnp.where(lane == i1, 1.0 / denom, 0.0) + jnp.where(lane == i2, e2 / denom, 0.0)
    cw_ref[...] = cw

    t = cw.shape[0]
    sel = jnp.where(cw.T[:N_EXPERTS] > 0.0, 1.0, 0.0)
    tri = jnp.where(lax.broadcasted_iota(jnp.int32, (t, t), 0) <= lax.broadcasted_iota(jnp.int32, (t, t), 1),
                    1.0, 0.0).astype(BF16)
    incl = dot(sel.astype(BF16), tri)
    post = jnp.where(sel > 0.0, incl - 1.0, -1.0)
    post_ref[...] = post
    pos_ref[...] = jnp.concatenate([post, jnp.full((LANES - N_EXPERTS, t), -1.0, F32)], axis=0).T
    cnt_ref[...] = jnp.broadcast_to(incl[:, t - 1:t], (N_EXPERTS, LANES))


def _router(x, w_router):
    n = x.shape[0]
    nblk = n // ROW_TILE
    w = jnp.zeros((D_MODEL, LANES), F32).at[:, :N_EXPERTS].set(w_router)
    tok = pl.BlockSpec((ROW_TILE, LANES), lambda i: (i, 0))
    cw, pos, post, cnt = pl.pallas_call(
        _router_kernel, grid=(nblk,),
        in_specs=[pl.BlockSpec((ROW_TILE, D_MODEL), lambda i: (i, 0)),
                  pl.BlockSpec((D_MODEL, LANES), lambda i: (0, 0))],
        out_specs=[tok, tok, pl.BlockSpec((N_EXPERTS, ROW_TILE), lambda i: (0, i)),
                   pl.BlockSpec((N_EXPERTS, LANES), lambda i: (i, 0))],
        out_shape=[jax.ShapeDtypeStruct((n, LANES), F32), jax.ShapeDtypeStruct((n, LANES), F32),
                   jax.ShapeDtypeStruct((N_EXPERTS, n), F32),
                   jax.ShapeDtypeStruct((nblk * N_EXPERTS, LANES), F32)],
        compiler_params=_params("parallel"), name="router_top2",
    )(x, w)
    return cw, pos, post, cnt[:, 0].astype(jnp.int32).reshape(nblk, N_EXPERTS)


def _moe_layout(cnt, n_tiles):
    seg = (cnt + SEG_ALIGN - 1) // SEG_ALIGN * SEG_ALIGN
    rows_e = jnp.sum(seg, axis=0)
    tiles_e = (rows_e + MOE_CHUNK + MOE_ROW_TILE - 1) // MOE_ROW_TILE
    tile_end = jnp.cumsum(tiles_e)
    tile_off = tile_end - tiles_e
    seg_off = (tile_off * MOE_ROW_TILE)[None, :] + jnp.cumsum(seg, axis=0) - seg
    tile = jnp.arange(n_tiles, dtype=jnp.int32)
    tile_expert = jnp.minimum(jnp.sum(tile[:, None] >= tile_end[None, :], axis=1), N_EXPERTS - 1)
    occupied = jnp.clip(rows_e[tile_expert] - (tile - tile_off[tile_expert]) * MOE_ROW_TILE, 0, MOE_ROW_TILE)
    as_i32 = lambda t: t.astype(jnp.int32)
    return as_i32(seg_off).reshape(-1), as_i32(tile_expert), as_i32(occupied)


def _chunk_copies(off_ref, cnt_ref, block, hbm_ref, vmem_ref, sems, to_hbm):
    slot = block % 2
    out = []
    for e in range(N_EXPERTS):
        c = cnt_ref[block * N_EXPERTS + e]
        off = pl.multiple_of(off_ref[block * N_EXPERTS + e], SEG_ALIGN)
        for k in range(ROW_TILE // MOE_CHUNK):
            idx = k * N_EXPERTS + e
            rows = hbm_ref.at[pl.ds(off + k * MOE_CHUNK, MOE_CHUNK)]
            staged = vmem_ref.at[slot, idx]
            src, dst = (staged, rows) if to_hbm else (rows, staged)
            out.append((c > k * MOE_CHUNK, e, k, idx, pltpu.make_async_copy(src, dst, sems.at[slot, idx])))
    return out


def _dispatch_kernel(off_ref, cnt_ref, x_ref, post_ref, xs_in_ref, xs_ref, stage_ref, sems):
    del xs_in_ref
    b, nblk = pl.program_id(0), pl.num_programs(0)
    copies_of = lambda blk: _chunk_copies(off_ref, cnt_ref, blk, xs_ref, stage_ref, sems, to_hbm=True)

    def wait_all(blk):
        for pred, _, _, _, copy in copies_of(blk):
            pl.when(pred)(copy.wait)

    xb = x_ref[...].astype(BF16)
    slot_row = lax.broadcasted_iota(jnp.int32, (MOE_CHUNK, ROW_TILE), 0).astype(F32)
    copies = copies_of(b)
    for pred, e, k, idx, _ in copies:
        @pl.when(pred)
        def _(e=e, k=k, idx=idx):
            onehot = jnp.where(post_ref[e:e + 1, :] == slot_row + float(k * MOE_CHUNK), 1.0, 0.0).astype(BF16)
            stage_ref[b % 2, idx] = jnp.dot(onehot, xb, preferred_element_type=F32).astype(BF16)
    pl.when(b >= 1)(lambda: wait_all(b - 1))
    for pred, _, _, _, copy in copies:
        pl.when(pred)(copy.start)
    pl.when(b == nblk - 1)(lambda: wait_all(b))


def _dispatch(x, post, seg_off, cnt, n_rows):
    n = x.shape[0]
    n_chunks = N_EXPERTS * (ROW_TILE // MOE_CHUNK)
    grid_spec = pltpu.PrefetchScalarGridSpec(
        num_scalar_prefetch=2, grid=(n // ROW_TILE,),
        in_specs=[pl.BlockSpec((ROW_TILE, D_MODEL), lambda i, *_: (i, 0)),
                  pl.BlockSpec((N_EXPERTS, ROW_TILE), lambda i, *_: (0, i)),
                  pl.BlockSpec(memory_space=pl.ANY)],
        out_specs=pl.BlockSpec(memory_space=pl.ANY),
        scratch_shapes=[pltpu.VMEM((2, n_chunks, MOE_CHUNK, D_MODEL), BF16),
                        pltpu.SemaphoreType.DMA((2, n_chunks))])
    return pl.pallas_call(
        _dispatch_kernel, grid_spec=grid_spec,
        out_shape=jax.ShapeDtypeStruct((n_rows, D_MODEL), BF16),
        input_output_aliases={4: 0},
        compiler_params=_params("arbitrary"), name="moe_dispatch",
    )(seg_off, cnt.reshape(-1), x, post, jnp.zeros((n_rows, D_MODEL), BF16))


def _swiglu_partial(xb, wg, wu, wd):
    hg = jnp.dot(xb, wg, preferred_element_type=F32)
    hu = jnp.dot(xb, wu, preferred_element_type=F32)
    h = hg * _sigmoid(hg) * hu
    return jnp.dot(h.astype(BF16), wd, preferred_element_type=F32)


def _moe_ffn_kernel(expert_ref, occ_ref, x_ref, wg_ref, wu_ref, wd_ref, o_ref, acc_ref):
    del expert_ref
    i, j = pl.program_id(0), pl.program_id(1)
    occupied = occ_ref[i]

    @pl.when(j == 0)
    def _():
        acc_ref[...] = jnp.zeros_like(acc_ref)

    def weights():
        return wg_ref[...].astype(BF16), wu_ref[...].astype(BF16), wd_ref[...].astype(BF16)

    @pl.when(occupied == MOE_ROW_TILE)
    def _():
        acc_ref[...] += _swiglu_partial(x_ref[...], *weights())

    @pl.when((occupied > 0) & (occupied < MOE_ROW_TILE))
    def _():
        wg, wu, wd = weights()
        for s in range(MOE_ROW_TILE // MOE_CHUNK):
            rows = slice(s * MOE_CHUNK, (s + 1) * MOE_CHUNK)

            @pl.when(s * MOE_CHUNK < occupied)
            def _(rows=rows):
                acc_ref[rows, :] += _swiglu_partial(x_ref[rows, :], wg, wu, wd)

    @pl.when(j == pl.num_programs(1) - 1)
    def _():
        o_ref[...] = acc_ref[...].astype(BF16)


def _moe_ffn(xs, tile_expert, occupied, w_gate, w_up, w_down):
    n_rows = xs.shape[0]
    d_ff = w_gate.shape[-1]
    row = pl.BlockSpec((MOE_ROW_TILE, D_MODEL), lambda i, j, *_: (i, 0))
    up = pl.BlockSpec((None, D_MODEL, MOE_FF_TILE), lambda i, j, ex, occ: (ex[i], 0, j))
    down = pl.BlockSpec((None, MOE_FF_TILE, D_MODEL), lambda i, j, ex, occ: (ex[i], j, 0))
    grid_spec = pltpu.PrefetchScalarGridSpec(
        num_scalar_prefetch=2, grid=(n_rows // MOE_ROW_TILE, d_ff // MOE_FF_TILE),
        in_specs=[row, up, up, down], out_specs=row,
        scratch_shapes=[pltpu.VMEM((MOE_ROW_TILE, D_MODEL), F32)])
    return pl.pallas_call(
        _moe_ffn_kernel, grid_spec=grid_spec,
        out_shape=jax.ShapeDtypeStruct((n_rows, D_MODEL), BF16),
        compiler_params=_params("parallel", "arbitrary"), name="moe_swiglu",
    )(tile_expert, occupied, xs, w_gate, w_up, w_down)


def _ffn_kernel(x_ref, wg_ref, wu_ref, wd_ref, g_ref, b_ref, o_ref, xb_ref, acc_ref):
    j = pl.program_id(1)

    @pl.when(j == 0)
    def _():
        xb_ref[...] = x_ref[...].astype(BF16)
        acc_ref[...] = jnp.zeros_like(acc_ref)

    acc_ref[...] += _swiglu_partial(xb_ref[...], wg_ref[...], wu_ref[...], wd_ref[...])

    @pl.when(j == pl.num_programs(1) - 1)
    def _():
        o_ref[...] = _layer_norm(DEEPNORM_ALPHA * x_ref[...] + acc_ref[...], g_ref[...], b_ref[...])


def _ffn(x, w_gate, w_up, w_down, g, b):
    n = x.shape[0]
    d_ff = w_gate.shape[-1]
    tm = FFN_ROW_TILE
    row = pl.BlockSpec((tm, D_MODEL), lambda i, j: (i, 0))
    vec = pl.BlockSpec((1, D_MODEL), lambda i, j: (0, 0))
    up = pl.BlockSpec((D_MODEL, DENSE_FF_TILE), lambda i, j: (0, j))
    down = pl.BlockSpec((DENSE_FF_TILE, D_MODEL), lambda i, j: (j, 0))
    return pl.pallas_call(
        _ffn_kernel, grid=(n // tm, d_ff // DENSE_FF_TILE),
        in_specs=[row, up, up, down, vec, vec], out_specs=row,
        out_shape=jax.ShapeDtypeStruct((n, D_MODEL), F32),
        scratch_shapes=[pltpu.VMEM((tm, D_MODEL), BF16), pltpu.VMEM((tm, D_MODEL), F32)],
        compiler_params=_params("parallel", "arbitrary"), name="swiglu_ln2",
    )(x, w_gate, w_up, w_down, g.reshape(1, -1), b.reshape(1, -1))


def _combine_kernel(off_ref, cnt_ref, x_ref, cw_ref, pos_ref, g_ref, b_ref, ys_ref, o_ref,
                    buf_ref, acc_ref, sems):
    b, nblk = pl.program_id(0), pl.num_programs(0)
    copies_of = lambda blk: _chunk_copies(off_ref, cnt_ref, blk, ys_ref, buf_ref, sems, to_hbm=False)

    def start_all(blk):
        for pred, _, k, _, copy in copies_of(blk):
            if k == 0:
                copy.start()
            else:
                pl.when(pred)(copy.start)

    pl.when(b == 0)(lambda: start_all(b))
    pl.when(b + 1 < nblk)(lambda: start_all(b + 1))

    lane = lax.broadcasted_iota(jnp.int32, (ROW_TILE, LANES), 1)
    slot_col = lax.broadcasted_iota(jnp.int32, (ROW_TILE, MOE_CHUNK), 1).astype(F32)
    column = lambda ref, e: jnp.sum(jnp.where(lane == e, ref[...], 0.0), axis=-1, keepdims=True)
    weight = [column(cw_ref, e) for e in range(N_EXPERTS)]
    pos = [column(pos_ref, e) for e in range(N_EXPERTS)]

    def scatter_matrix(e, k):
        return jnp.where(pos[e] == slot_col + float(k * MOE_CHUNK), weight[e], 0.0).astype(BF16)

    first = jnp.concatenate([scatter_matrix(e, 0) for e in range(N_EXPERTS)], axis=1)
    copies = copies_of(b)
    for _, _, k, _, copy in copies:
        if k == 0:
            copy.wait()
    firsts = buf_ref[b % 2, 0:N_EXPERTS].reshape(N_EXPERTS * MOE_CHUNK, D_MODEL)
    acc_ref[...] = jnp.dot(first, firsts, preferred_element_type=F32)
    for pred, e, k, idx, copy in copies:
        if k > 0:
            @pl.when(pred)
            def _(e=e, k=k, idx=idx, copy=copy):
                copy.wait()
                acc_ref[...] += jnp.dot(scatter_matrix(e, k), buf_ref[b % 2, idx], preferred_element_type=F32)

    o_ref[...] = _layer_norm(DEEPNORM_ALPHA * x_ref[...] + acc_ref[...], g_ref[...], b_ref[...])


def _combine(x, ys, cw, pos, seg_off, cnt, g, b):
    n = x.shape[0]
    n_chunks = N_EXPERTS * (ROW_TILE // MOE_CHUNK)
    tok = lambda w: pl.BlockSpec((ROW_TILE, w), lambda i, *_: (i, 0))
    vec = pl.BlockSpec((1, D_MODEL), lambda i, *_: (0, 0))
    grid_spec = pltpu.PrefetchScalarGridSpec(
        num_scalar_prefetch=2, grid=(n // ROW_TILE,),
        in_specs=[tok(D_MODEL), tok(LANES), tok(LANES), vec, vec, pl.BlockSpec(memory_space=pl.ANY)],
        out_specs=tok(D_MODEL),
        scratch_shapes=[pltpu.VMEM((2, n_chunks, MOE_CHUNK, D_MODEL), BF16),
                        pltpu.VMEM((ROW_TILE, D_MODEL), F32),
                        pltpu.SemaphoreType.DMA((2, n_chunks))])
    return pl.pallas_call(
        _combine_kernel, grid_spec=grid_spec,
        out_shape=jax.ShapeDtypeStruct((n, D_MODEL), F32),
        compiler_params=_params("arbitrary"), name="moe_combine_ln2",
    )(seg_off, cnt.reshape(-1), x, cw, pos, g.reshape(1, -1), b.reshape(1, -1), ys)


def _moe(x, w_router, w_gate, w_up, w_down, g, b):
    n = x.shape[0]
    nblk = n // ROW_TILE
    max_rows = 2 * n + nblk * N_EXPERTS * (SEG_ALIGN - 1) + N_EXPERTS * (MOE_CHUNK + MOE_ROW_TILE - 1)
    n_tiles = max_rows // MOE_ROW_TILE
    cw, pos, post, cnt = _router(x, w_router)
    seg_off, tile_expert, occupied = _moe_layout(cnt, n_tiles)
    xs = _dispatch(x, post, seg_off, cnt, n_tiles * MOE_ROW_TILE)
    ys = _moe_ffn(xs, tile_expert, occupied, w_gate, w_up, w_down)
    return _combine(x, ys, cw, pos, seg_off, cnt, g, b)


def kernel(x, emb_ln_g, emb_ln_b, w_in, b_gate, na_rpb, sw_sink, w_branch_na, w_branch_sw, w_out,
           ln1_g, ln1_b, ffn_w_gate, ffn_w_up, ffn_w_down, moe_router, moe_w_gate, moe_w_up,
           moe_w_down, ln2_g, ln2_b):
    batch, seq, d = x.shape
    assert (seq, d) == (SEQ, D_MODEL)
    n = batch * seq
    tables = _rotary_tables()
    h = x.reshape(n, d)
    for layer in range(DEPTH):
        h, q_na, k_na, v_na, q_sw, k_sw, v_sw, gates = _proj(
            h, w_in[layer].astype(BF16), b_gate[layer], tables,
            embed_ln=(emb_ln_g, emb_ln_b) if layer == 0 else None)
        seq3 = lambda t: t.reshape(batch, seq, t.shape[-1])
        y_na = _na_attention(seq3(q_na), seq3(k_na), seq3(v_na), _na_bias_tables(na_rpb[layer]))
        y_sw = _sw_attention(seq3(q_sw), seq3(k_sw), seq3(v_sw), sw_sink[layer])
        h = _merge(h, y_na.reshape(n, -1), y_sw.reshape(n, -1), gates,
                   w_branch_na[layer].astype(BF16), w_branch_sw[layer].astype(BF16),
                   w_out[layer].astype(BF16), ln1_g[layer], ln1_b[layer])
        i = layer // 2
        if layer % 2 == 0:
            h = _ffn(h, ffn_w_gate[i].astype(BF16), ffn_w_up[i].astype(BF16),
                     ffn_w_down[i].astype(BF16), ln2_g[layer], ln2_b[layer])
        else:
            h = _moe(h, moe_router[i], moe_w_gate[i], moe_w_up[i], moe_w_down[i],
                     ln2_g[layer], ln2_b[layer])
    return h.reshape(batch, seq, d)
```

```python
import functools

import numpy as np
import jax
import jax.numpy as jnp
from jax import lax
from jax.experimental import pallas as pl
from jax.experimental.pallas import tpu as pltpu

F32 = jnp.float32
BF16 = jnp.bfloat16

D_MODEL = 1024
SEQ = 2048
DEPTH = 2
HEAD_DIM = 64
NA_HEADS = 8
NA_WIDTH = NA_HEADS * HEAD_DIM
GRID_W = 64
GRID_ROWS = SEQ // GRID_W
NA_KH = 8
NA_KW = 16
SW_HEADS = 8
SW_KV_HEADS = 2
SW_GROUP = SW_HEADS // SW_KV_HEADS
SW_WIDTH = SW_HEADS * HEAD_DIM
SW_KV_WIDTH = SW_KV_HEADS * HEAD_DIM
SW_WINDOW = 128
SW_BLOCK = 128
ROT_DIM = HEAD_DIM // 4
ROPE_THETA = 500000.0
OFF_QNA = NA_WIDTH
OFF_KNA = 2 * NA_WIDTH
OFF_VNA = 3 * NA_WIDTH
OFF_QSW = OFF_VNA + SW_WIDTH
OFF_KSW = OFF_QSW + SW_KV_WIDTH
OFF_VSW = OFF_KSW + SW_KV_WIDTH
PROJ_COLS = OFF_VSW + 2 * D_MODEL
N_EXPERTS = 8
DEEPNORM_ALPHA = (2 * DEPTH) ** 0.25
LN_EPS = 1e-5
NEG_INF = -1e30
QK_SCALE = HEAD_DIM ** -0.5

LANES = 128
MXU_DIM = 256
V7X_VMEM_BYTES = 64 * 1024 * 1024
VMEM_LIMIT = V7X_VMEM_BYTES * 7 // 8

ROW_TILE = 512
MOE_ROW_TILE = 1536
MOE_FF_TILE = 512
MOE_CHUNK = 256
SEG_ALIGN = 16
NA_ROWS_PER_STEP = 4
SW_BLOCKS_PER_STEP = 4


def _layer_norm(z, g, b):
    mu = jnp.mean(z, axis=-1, keepdims=True)
    d = z - mu
    var = jnp.mean(d * d, axis=-1, keepdims=True)
    return d * lax.rsqrt(var + LN_EPS) * g + b


def _sigmoid(z):
    return 1.0 / (1.0 + jnp.exp(-z))


def _params(*sem):
    return pltpu.CompilerParams(dimension_semantics=sem, vmem_limit_bytes=VMEM_LIMIT)


def _rotary_tables():
    half = ROT_DIM // 2
    inv_freq = 1.0 / (ROPE_THETA ** (jnp.arange(0, ROT_DIM, 2, dtype=F32) / ROT_DIM))
    ang = jnp.arange(SEQ, dtype=jnp.int32).astype(F32)[:, None] * inv_freq[None, :]
    cos, sin = jnp.cos(ang), jnp.sin(ang)
    ones = jnp.ones((SEQ, HEAD_DIM - ROT_DIM), F32)
    zeros = jnp.zeros((SEQ, HEAD_DIM - ROT_DIM), F32)
    zh = jnp.zeros((SEQ, half), F32)
    cos_h = jnp.concatenate([cos, cos, ones], axis=1)
    sa_h = jnp.concatenate([-sin, zh, zeros], axis=1)
    sb_h = jnp.concatenate([zh, sin, zeros], axis=1)
    two = lambda t: jnp.concatenate([t, t], axis=1)
    return two(cos_h), two(sa_h), two(sb_h)


def _proj_kernel(*refs, embed_ln):
    if embed_ln:
        x_ref, g_ref, b_ref, w_ref, bg_ref, cos_ref, sa_ref, sb_ref, xn_ref, *outs = refs
        x = _layer_norm(x_ref[...], g_ref[...], b_ref[...])
        xn_ref[...] = x
    else:
        x_ref, w_ref, bg_ref, cos_ref, sa_ref, sb_ref, *outs = refs
        x = x_ref[...]
    qna_ref, kna_ref, vna_ref, qsw_ref, ksw_ref, vsw_ref, gate_ref = outs
    xb = x.astype(BF16)

    def mm(lo, hi):
        return jnp.dot(xb, w_ref[:, lo:hi], preferred_element_type=F32)

    qna_ref[...] = (mm(0, OFF_QNA) * QK_SCALE).astype(BF16)
    kna_ref[...] = mm(OFF_QNA, OFF_KNA).astype(BF16)
    vna_ref[...] = mm(OFF_KNA, OFF_VNA).astype(BF16)

    cos, sa, sb = cos_ref[...], sa_ref[...], sb_ref[...]
    half = ROT_DIM // 2

    def rot(t):
        return t * cos + pltpu.roll(t, LANES - half, 1) * sa + pltpu.roll(t, half, 1) * sb

    q = mm(OFF_VNA, OFF_QSW)
    for c in range(SW_WIDTH // LANES):
        sl = slice(c * LANES, (c + 1) * LANES)
        qsw_ref[:, sl] = (rot(q[:, sl]) * QK_SCALE).astype(BF16)

    kv = mm(OFF_QSW, OFF_VSW)
    k = rot(kv[:, :SW_KV_WIDTH])
    v = kv[:, SW_KV_WIDTH:]
    first = lax.broadcasted_iota(jnp.int32, k.shape, 1) < HEAD_DIM

    def dup(t):
        r = pltpu.roll(t, HEAD_DIM, 1)
        return jnp.concatenate([jnp.where(first, t, r), jnp.where(first, r, t)], axis=1)

    ksw_ref[...] = dup(k).astype(BF16)
    vsw_ref[...] = v.astype(BF16)

    gw = 512
    for c in range(2 * D_MODEL // gw):
        z = mm(OFF_VSW + c * gw, OFF_VSW + (c + 1) * gw) + bg_ref[:, c * gw:(c + 1) * gw]
        gate_ref[:, c * gw:(c + 1) * gw] = _sigmoid(z).astype(BF16)


def _proj(x, w_bf, b_gate, tables, embed_ln=None):
    n = x.shape[0]
    tiles_per_seq = SEQ // ROW_TILE
    row = lambda w: pl.BlockSpec((ROW_TILE, w), lambda i: (i, 0))
    full = lambda a: pl.BlockSpec(a.shape, lambda i: (0,) * a.ndim)
    tab = pl.BlockSpec((ROW_TILE, LANES), lambda i: (i % tiles_per_seq, 0))
    bg = b_gate.reshape(1, -1)
    widths = (NA_WIDTH, NA_WIDTH, NA_WIDTH, SW_WIDTH, 2 * SW_KV_WIDTH, SW_KV_WIDTH, 2 * D_MODEL)
    ln = [t.reshape(1, -1) for t in embed_ln] if embed_ln else []
    out_specs = [row(w) for w in widths]
    out_shape = [jax.ShapeDtypeStruct((n, w), BF16) for w in widths]
    if embed_ln:
        out_specs = [row(D_MODEL)] + out_specs
        out_shape = [jax.ShapeDtypeStruct((n, D_MODEL), F32)] + out_shape
    outs = pl.pallas_call(
        functools.partial(_proj_kernel, embed_ln=bool(embed_ln)), grid=(n // ROW_TILE,),
        in_specs=[row(D_MODEL)] + [full(t) for t in ln] + [full(w_bf), full(bg), tab, tab, tab],
        out_specs=out_specs, out_shape=out_shape,
        compiler_params=_params("parallel"), name="in_proj",
    )(x, *ln, w_bf, bg, *tables)
    return outs if embed_ln else [x] + list(outs)


def _na_bias_tables(rpb):
    c = np.arange(GRID_W)
    qcs = np.clip(c - NA_KW // 2, 0, GRID_W - NA_KW)
    valid = (c[None, :] >= qcs[:, None]) & (c[None, :] < qcs[:, None] + NA_KW)
    n_dr, n_dc = 2 * NA_KH - 1, 2 * NA_KW - 1
    span = 2 * GRID_W - 1
    lead = GRID_W - NA_KW
    ext = jnp.pad(rpb.astype(F32), ((0, 0), (0, 0), (lead, span + 1 - lead - n_dc)))
    flat = jnp.broadcast_to(ext[:, :, None, :], (NA_HEADS, n_dr, GRID_W, span + 1))
    flat = flat.reshape(NA_HEADS, n_dr, GRID_W * (span + 1))[:, :, :GRID_W * span]
    t = flat.reshape(NA_HEADS, n_dr, GRID_W, span)[:, :, :, GRID_W - 1:]
    t = jnp.where(valid[None, None], t, NEG_INF)
    pairs = jnp.concatenate([t[:, :-1], t[:, 1:]], axis=-1)
    pairs = pairs.reshape(2, 4, n_dr - 1, GRID_W, 2 * GRID_W).transpose(0, 2, 1, 3, 4)
    return pairs.reshape(2, n_dr - 1, 4 * GRID_W, 2 * GRID_W)


def _na_kernel(q_ref, k_ref, v_ref, bias_ref, o_ref):
    lane = lax.broadcasted_iota(jnp.int32, (GRID_W, MXU_DIM), 1)
    head_mask = [(lane >= h * HEAD_DIM) & (lane < (h + 1) * HEAD_DIM) for h in range(4)]
    nk = NA_KH * GRID_W

    def rows(it, carry):
        chains = []
        for u in range(NA_ROWS_PER_STEP):
            r = it * NA_ROWS_PER_STEP + u
            rs = jnp.clip(r - NA_KH // 2, 0, GRID_ROWS - NA_KH)
            variant = rs - r + (NA_KH - 1)
            qrows = pl.ds(pl.multiple_of(r * GRID_W, GRID_W), GRID_W)
            krows = pl.ds(pl.multiple_of(rs * GRID_W, GRID_W), nk)
            q = q_ref[qrows, :]
            for g in range(2):
                gs = slice(g * MXU_DIM, (g + 1) * MXU_DIM)
                qg = q[:, gs]
                lhs = jnp.concatenate([jnp.where(m, qg, jnp.zeros_like(qg)) for m in head_mask], axis=0)
                bias = jnp.concatenate([bias_ref[g, variant + 2 * a] for a in range(NA_KH // 2)], axis=1)
                chains.append(dict(lhs=lhs, k=k_ref[krows, gs], v=v_ref[krows, gs], bias=bias,
                                   out=(qrows, gs)))
        for c in chains:
            c["s"] = lax.dot_general(c["lhs"], c["k"], (((1,), (1,)), ((), ())),
                                     preferred_element_type=F32) + c["bias"]
        for c in chains:
            c["p"] = jnp.exp(c["s"] - jnp.max(c["s"], axis=-1, keepdims=True))
        for c in chains:
            c["l"] = jnp.sum(c["p"], axis=-1, keepdims=True)
        for c in chains:
            c["o"] = jnp.dot(c["p"].astype(BF16), c["v"], preferred_element_type=F32) / c["l"]
        for c in chains:
            og = jnp.zeros((GRID_W, MXU_DIM), F32)
            for h in range(4):
                og = og + jnp.where(head_mask[h], c["o"][h * GRID_W:(h + 1) * GRID_W], 0.0)
            o_ref[c["out"]] = og.astype(BF16)
        return carry

    lax.fori_loop(0, GRID_ROWS // NA_ROWS_PER_STEP, rows, 0)


def _na_attention(q, k, v, bias):
    b = q.shape[0]
    seq = pl.BlockSpec((None, SEQ, NA_WIDTH), lambda i: (i, 0, 0))
    return pl.pallas_call(
        _na_kernel, grid=(b,),
        in_specs=[seq, seq, seq, pl.BlockSpec(bias.shape, lambda i: (0, 0, 0, 0))],
        out_specs=seq, out_shape=jax.ShapeDtypeStruct((b, SEQ, NA_WIDTH), BF16),
        compiler_params=_params("parallel"), name="na_attention",
    )(q, k, v, bias)


def _sw_masks():
    nkeys = 3 * SW_BLOCK
    jk = np.arange(nkeys)[:, None]
    iq = np.arange(SW_BLOCK)[None, :]
    band = (jk >= iq) & (jk <= iq + 2 * SW_WINDOW)
    in_seq = [(jk >= SW_BLOCK), np.ones_like(band), (jk < 2 * SW_BLOCK)]
    return jnp.asarray(np.stack([np.where(band & ok, 0.0, NEG_INF) for ok in in_seq]), F32)


def _sw_kernel(sink_ref, q_ref, k_ref, v_ref, mask_ref, o_ref):
    nb = SEQ // SW_BLOCK
    first = lax.broadcasted_iota(jnp.int32, (SW_BLOCK, LANES), 1) < HEAD_DIM
    ones = jnp.ones((8, 3 * SW_BLOCK), BF16)

    def blocks(it, carry):
        chains = []
        for u in range(SW_BLOCKS_PER_STEP):
            n = it * SW_BLOCKS_PER_STEP + u
            mask = mask_ref[jnp.where(n == 0, 0, jnp.where(n == nb - 1, 2, 1))]
            starts = [pl.multiple_of(jnp.clip(n + d, 0, nb - 1) * SW_BLOCK, SW_BLOCK) for d in (-1, 0, 1)]
            qrows = pl.ds(pl.multiple_of(n * SW_BLOCK, SW_BLOCK), SW_BLOCK)
            q = q_ref[qrows, :]
            vv = jnp.concatenate([v_ref[pl.ds(s, SW_BLOCK), :] for s in starts], axis=0)
            vt = vv.astype(F32).T.astype(BF16)
            for kvh in range(SW_KV_HEADS):
                ks = slice(kvh * LANES, (kvh + 1) * LANES)
                kk = jnp.concatenate([k_ref[pl.ds(s, SW_BLOCK), ks] for s in starts], axis=0)
                parts, sinks = [], []
                for j in range(SW_GROUP):
                    h = kvh * SW_GROUP + j
                    qc = q[:, (h // 2) * LANES:(h // 2 + 1) * LANES]
                    keep = first if h % 2 == 0 else jnp.logical_not(first)
                    parts.append(jnp.where(keep, qc, jnp.zeros_like(qc)))
                    sinks.append(jnp.full((1, SW_BLOCK), sink_ref[h], F32))
                vt_ones = jnp.concatenate([vt[kvh * HEAD_DIM:(kvh + 1) * HEAD_DIM], ones], axis=0)
                chains.append(dict(k=kk, vt=vt_ones, q=jnp.concatenate(parts, axis=0),
                                   sink=jnp.concatenate(sinks, axis=1), mask=mask, qrows=qrows, kvh=kvh))
        for c in chains:
            s = lax.dot_general(c["k"], c["q"], (((1,), (1,)), ((), ())), preferred_element_type=F32)
            prev_mask, next_mask = c["mask"][:SW_BLOCK], c["mask"][2 * SW_BLOCK:]
            cols = []
            for j in range(SW_GROUP):
                sj = s[:, j * SW_BLOCK:(j + 1) * SW_BLOCK]
                cols.append(jnp.concatenate([sj[:SW_BLOCK] + prev_mask, sj[SW_BLOCK:2 * SW_BLOCK],
                                             sj[2 * SW_BLOCK:] + next_mask], axis=0))
            c["s"] = jnp.concatenate(cols, axis=1)
        for c in chains:
            c["m"] = jnp.maximum(jnp.max(c["s"], axis=0, keepdims=True), c["sink"])
        for c in chains:
            c["p"] = jnp.exp(c["s"] - c["m"]).astype(BF16)
        for c in chains:
            o = jnp.dot(c["vt"], c["p"], preferred_element_type=F32)
            denom = o[HEAD_DIM:HEAD_DIM + 1] + jnp.exp(c["sink"] - c["m"])
            c["o"] = o[:HEAD_DIM] / denom
        for c in chains:
            for pair in range(SW_GROUP // 2):
                even = c["o"][:, (2 * pair) * SW_BLOCK:(2 * pair + 1) * SW_BLOCK]
                odd = c["o"][:, (2 * pair + 1) * SW_BLOCK:(2 * pair + 2) * SW_BLOCK]
                col = (c["kvh"] * (SW_GROUP // 2) + pair) * LANES
                o_ref[c["qrows"], col:col + LANES] = jnp.concatenate([even, odd], axis=0).T.astype(BF16)
        return carry

    lax.fori_loop(0, nb // SW_BLOCKS_PER_STEP, blocks, 0)


def _sw_attention(q, k2, v, sink):
    b = q.shape[0]
    masks = _sw_masks()
    seq = lambda w: pl.BlockSpec((None, SEQ, w), lambda i: (i, 0, 0))
    return pl.pallas_call(
        _sw_kernel, grid=(b,),
        in_specs=[pl.BlockSpec(memory_space=pltpu.SMEM), seq(SW_WIDTH), seq(2 * SW_KV_WIDTH),
                  seq(SW_KV_WIDTH), pl.BlockSpec(masks.shape, lambda i: (0, 0, 0))],
        out_specs=seq(SW_WIDTH), out_shape=jax.ShapeDtypeStruct((b, SEQ, SW_WIDTH), BF16),
        compiler_params=_params("parallel"), name="sw_attention",
    )(sink.astype(F32), q, k2, v, masks)


def _merge_kernel(x_ref, yna_ref, ysw_ref, gate_ref, wna_ref, wsw_ref, wout_ref, g_ref, b_ref, o_ref):
    a = jnp.dot(yna_ref[...], wna_ref[...], preferred_element_type=F32)
    s = jnp.dot(ysw_ref[...], wsw_ref[...], preferred_element_type=F32)
    m = gate_ref[:, :D_MODEL].astype(F32) * a + gate_ref[:, D_MODEL:].astype(F32) * s
    z = DEEPNORM_ALPHA * x_ref[...] + jnp.dot(m.astype(BF16), wout_ref[...], preferred_element_type=F32)
    o_ref[...] = _layer_norm(z, g_ref[...], b_ref[...])


def _merge(x, y_na, y_sw, gates, w_na, w_sw, w_out, g, b):
    n = x.shape[0]
    row = lambda w: pl.BlockSpec((ROW_TILE, w), lambda i: (i, 0))
    full = lambda a: pl.BlockSpec(a.shape, lambda i: (0,) * a.ndim)
    g2, b2 = g.reshape(1, -1), b.reshape(1, -1)
    return pl.pallas_call(
        _merge_kernel, grid=(n // ROW_TILE,),
        in_specs=[row(D_MODEL), row(NA_WIDTH), row(SW_WIDTH), row(2 * D_MODEL),
                  full(w_na), full(w_sw), full(w_out), full(g2), full(b2)],
        out_specs=row(D_MODEL), out_shape=jax.ShapeDtypeStruct((n, D_MODEL), F32),
        compiler_params=_params("parallel"), name="merge_ln1",
    )(x, y_na, y_sw, gates, w_na, w_sw, w_out, g2, b2)


def _router_kernel(x_ref, w_ref, cw_ref, pos_ref, post_ref, cnt_ref):
    x, w = x_ref[...], w_ref[...]
    xh, wh = x.astype(BF16), w.astype(BF16)
    xl = (x - xh.astype(F32)).astype(BF16)
    wl = (w - wh.astype(F32)).astype(BF16)
    dot = functools.partial(jnp.dot, preferred_element_type=F32)
    logits = dot(xh, wh) + dot(xl, wh) + dot(xh, wl)
    lane = lax.broadcasted_iota(jnp.int32, logits.shape, 1).astype(F32)
    lg = jnp.where(lane < N_EXPERTS, logits, -jnp.inf)
    m1 = jnp.max(lg, axis=-1, keepdims=True)
    i1 = jnp.min(jnp.where(lg == m1, lane, float(LANES)), axis=-1, keepdims=True)
    lg2 = jnp.where(lane == i1, -jnp.inf, lg)
    m2 = jnp.max(lg2, axis=-1, keepdims=True)
    i2 = jnp.min(jnp.where(lg2 == m2, lane, float(LANES)), axis=-1, keepdims=True)
    e2 = jnp.exp(m2 - m1)
    denom = 1.0 + e2
    cw = jnp.where(lane == i1, 1.0 / denom, 0.0) + jnp.where(lane == i2, e2 / denom, 0.0)
    cw_ref[...] = cw

    t = cw.shape[0]
    sel = jnp.where(cw.T[:N_EXPERTS] > 0.0, 1.0, 0.0)
    tri = jnp.where(lax.broadcasted_iota(jnp.int32, (t, t), 0) <= lax.broadcasted_iota(jnp.int32, (t, t), 1),
                    1.0, 0.0).astype(BF16)
    incl = dot(sel.astype(BF16), tri)
    post = jnp.where(sel > 0.0, incl - 1.0, -1.0)
    post_ref[...] = post
    pos_ref[...] = jnp.concatenate([post, jnp.full((LANES - N_EXPERTS, t), -1.0, F32)], axis=0).T
    cnt_ref[...] = jnp.broadcast_to(incl[:, t - 1:t], (N_EXPERTS, LANES))


def _router(x, w_router):
    n = x.shape[0]
    nblk = n // ROW_TILE
    w = jnp.zeros((D_MODEL, LANES), F32).at[:, :N_EXPERTS].set(w_router)
    tok = pl.BlockSpec((ROW_TILE, LANES), lambda i: (i, 0))
    cw, pos, post, cnt = pl.pallas_call(
        _router_kernel, grid=(nblk,),
        in_specs=[pl.BlockSpec((ROW_TILE, D_MODEL), lambda i: (i, 0)),
                  pl.BlockSpec((D_MODEL, LANES), lambda i: (0, 0))],
        out_specs=[tok, tok, pl.BlockSpec((N_EXPERTS, ROW_TILE), lambda i: (0, i)),
                   pl.BlockSpec((N_EXPERTS, LANES), lambda i: (i, 0))],
        out_shape=[jax.ShapeDtypeStruct((n, LANES), F32), jax.ShapeDtypeStruct((n, LANES), F32),
                   jax.ShapeDtypeStruct((N_EXPERTS, n), F32),
                   jax.ShapeDtypeStruct((nblk * N_EXPERTS, LANES), F32)],
        compiler_params=_params("parallel"), name="router_top2",
    )(x, w)
    return cw, pos, post, cnt[:, 0].astype(jnp.int32).reshape(nblk, N_EXPERTS)


def _moe_layout(cnt, n_tiles):
    seg = (cnt + SEG_ALIGN - 1) // SEG_ALIGN * SEG_ALIGN
    rows_e = jnp.sum(seg, axis=0)
    tiles_e = (rows_e + MOE_CHUNK + MOE_ROW_TILE - 1) // MOE_ROW_TILE
    tile_end = jnp.cumsum(tiles_e)
    tile_off = tile_end - tiles_e
    seg_off = (tile_off * MOE_ROW_TILE)[None, :] + jnp.cumsum(seg, axis=0) - seg
    tile = jnp.arange(n_tiles, dtype=jnp.int32)
    tile_expert = jnp.minimum(jnp.sum(tile[:, None] >= tile_end[None, :], axis=1), N_EXPERTS - 1)
    occupied = jnp.clip(rows_e[tile_expert] - (tile - tile_off[tile_expert]) * MOE_ROW_TILE, 0, MOE_ROW_TILE)
    as_i32 = lambda t: t.astype(jnp.int32)
    return as_i32(seg_off).reshape(-1), as_i32(tile_expert), as_i32(occupied)


def _chunk_copies(off_ref, cnt_ref, block, hbm_ref, vmem_ref, sems, to_hbm):
    slot = block % 2
    out = []
    for e in range(N_EXPERTS):
        c = cnt_ref[block * N_EXPERTS + e]
        off = pl.multiple_of(off_ref[block * N_EXPERTS + e], SEG_ALIGN)
        for k in range(ROW_TILE // MOE_CHUNK):
            idx = k * N_EXPERTS + e
            rows = hbm_ref.at[pl.ds(off + k * MOE_CHUNK, MOE_CHUNK)]
            staged = vmem_ref.at[slot, idx]
            src, dst = (staged, rows) if to_hbm else (rows, staged)
            out.append((c > k * MOE_CHUNK, e, k, idx, pltpu.make_async_copy(src, dst, sems.at[slot, idx])))
    return out


def _dispatch_kernel(off_ref, cnt_ref, x_ref, post_ref, xs_in_ref, xs_ref, stage_ref, sems):
    del xs_in_ref
    b, nblk = pl.program_id(0), pl.num_programs(0)
    copies_of = lambda blk: _chunk_copies(off_ref, cnt_ref, blk, xs_ref, stage_ref, sems, to_hbm=True)

    def wait_all(blk):
        for pred, _, _, _, copy in copies_of(blk):
            pl.when(pred)(copy.wait)

    xb = x_ref[...].astype(BF16)
    slot_row = lax.broadcasted_iota(jnp.int32, (MOE_CHUNK, ROW_TILE), 0).astype(F32)
    copies = copies_of(b)
    for pred, e, k, idx, _ in copies:
        @pl.when(pred)
        def _(e=e, k=k, idx=idx):
            onehot = jnp.where(post_ref[e:e + 1, :] == slot_row + float(k * MOE_CHUNK), 1.0, 0.0).astype(BF16)
            stage_ref[b % 2, idx] = jnp.dot(onehot, xb, preferred_element_type=F32).astype(BF16)
    pl.when(b >= 1)(lambda: wait_all(b - 1))
    for pred, _, _, _, copy in copies:
        pl.when(pred)(copy.start)
    pl.when(b == nblk - 1)(lambda: wait_all(b))


def _dispatch(x, post, seg_off, cnt, n_rows):
    n = x.shape[0]
    n_chunks = N_EXPERTS * (ROW_TILE // MOE_CHUNK)
    grid_spec = pltpu.PrefetchScalarGridSpec(
        num_scalar_prefetch=2, grid=(n // ROW_TILE,),
        in_specs=[pl.BlockSpec((ROW_TILE, D_MODEL), lambda i, *_: (i, 0)),
                  pl.BlockSpec((N_EXPERTS, ROW_TILE), lambda i, *_: (0, i)),
                  pl.BlockSpec(memory_space=pl.ANY)],
        out_specs=pl.BlockSpec(memory_space=pl.ANY),
        scratch_shapes=[pltpu.VMEM((2, n_chunks, MOE_CHUNK, D_MODEL), BF16),
                        pltpu.SemaphoreType.DMA((2, n_chunks))])
    return pl.pallas_call(
        _dispatch_kernel, grid_spec=grid_spec,
        out_shape=jax.ShapeDtypeStruct((n_rows, D_MODEL), BF16),
        input_output_aliases={4: 0},
        compiler_params=_params("arbitrary"), name="moe_dispatch",
    )(seg_off, cnt.reshape(-1), x, post, jnp.zeros((n_rows, D_MODEL), BF16))


def _swiglu_partial(xb, wg, wu, wd):
    hg = jnp.dot(xb, wg, preferred_element_type=F32)
    hu = jnp.dot(xb, wu, preferred_element_type=F32)
    h = hg * _sigmoid(hg) * hu
    return jnp.dot(h.astype(BF16), wd, preferred_element_type=F32)


def _moe_ffn_kernel(expert_ref, occ_ref, x_ref, wg_ref, wu_ref, wd_ref, o_ref, acc_ref):
    del expert_ref
    i, j = pl.program_id(0), pl.program_id(1)
    occupied = occ_ref[i]

    @pl.when(j == 0)
    def _():
        acc_ref[...] = jnp.zeros_like(acc_ref)

    def weights():
        return wg_ref[...].astype(BF16), wu_ref[...].astype(BF16), wd_ref[...].astype(BF16)

    @pl.when(occupied == MOE_ROW_TILE)
    def _():
        acc_ref[...] += _swiglu_partial(x_ref[...], *weights())

    @pl.when((occupied > 0) & (occupied < MOE_ROW_TILE))
    def _():
        wg, wu, wd = weights()
        for s in range(MOE_ROW_TILE // MOE_CHUNK):
            rows = slice(s * MOE_CHUNK, (s + 1) * MOE_CHUNK)

            @pl.when(s * MOE_CHUNK < occupied)
            def _(rows=rows):
                acc_ref[rows, :] += _swiglu_partial(x_ref[rows, :], wg, wu, wd)

    @pl.when(j == pl.num_programs(1) - 1)
    def _():
        o_ref[...] = acc_ref[...].astype(BF16)


def _moe_ffn(xs, tile_expert, occupied, w_gate, w_up, w_down):
    n_rows = xs.shape[0]
    d_ff = w_gate.shape[-1]
    row = pl.BlockSpec((MOE_ROW_TILE, D_MODEL), lambda i, j, *_: (i, 0))
    up = pl.BlockSpec((None, D_MODEL, MOE_FF_TILE), lambda i, j, ex, occ: (ex[i], 0, j))
    down = pl.BlockSpec((None, MOE_FF_TILE, D_MODEL), lambda i, j, ex, occ: (ex[i], j, 0))
    grid_spec = pltpu.PrefetchScalarGridSpec(
        num_scalar_prefetch=2, grid=(n_rows // MOE_ROW_TILE, d_ff // MOE_FF_TILE),
        in_specs=[row, up, up, down], out_specs=row,
        scratch_shapes=[pltpu.VMEM((MOE_ROW_TILE, D_MODEL), F32)])
    return pl.pallas_call(
        _moe_ffn_kernel, grid_spec=grid_spec,
        out_shape=jax.ShapeDtypeStruct((n_rows, D_MODEL), BF16),
        compiler_params=_params("parallel", "arbitrary"), name="moe_swiglu",
    )(tile_expert, occupied, xs, w_gate, w_up, w_down)


def _ffn_kernel(x_ref, wg_ref, wu_ref, wd_ref, g_ref, b_ref, o_ref):
    x = x_ref[...]
    y = _swiglu_partial(x.astype(BF16), wg_ref[...], wu_ref[...], wd_ref[...])
    o_ref[...] = _layer_norm(DEEPNORM_ALPHA * x + y, g_ref[...], b_ref[...])


def _ffn(x, w_gate, w_up, w_down, g, b):
    n = x.shape[0]
    row = pl.BlockSpec((ROW_TILE, D_MODEL), lambda i: (i, 0))
    const = lambda a: pl.BlockSpec(a.shape, lambda i: (0,) * a.ndim, pipeline_mode=pl.Buffered(1))
    g2, b2 = g.reshape(1, -1), b.reshape(1, -1)
    return pl.pallas_call(
        _ffn_kernel, grid=(n // ROW_TILE,),
        in_specs=[row, const(w_gate), const(w_up), const(w_down), const(g2), const(b2)], out_specs=row,
        out_shape=jax.ShapeDtypeStruct((n, D_MODEL), F32),
        compiler_params=_params("parallel"), name="swiglu_ln2",
    )(x, w_gate, w_up, w_down, g2, b2)


def _combine_kernel(off_ref, cnt_ref, x_ref, cw_ref, pos_ref, g_ref, b_ref, ys_ref, o_ref,
                    buf_ref, acc_ref, sems):
    b, nblk = pl.program_id(0), pl.num_programs(0)
    copies_of = lambda blk: _chunk_copies(off_ref, cnt_ref, blk, ys_ref, buf_ref, sems, to_hbm=False)

    def start_all(blk):
        for pred, _, k, _, copy in copies_of(blk):
            if k == 0:
                copy.start()
            else:
                pl.when(pred)(copy.start)

    pl.when(b == 0)(lambda: start_all(b))
    pl.when(b + 1 < nblk)(lambda: start_all(b + 1))

    lane = lax.broadcasted_iota(jnp.int32, (ROW_TILE, LANES), 1)
    slot_col = lax.broadcasted_iota(jnp.int32, (ROW_TILE, MOE_CHUNK), 1).astype(F32)
    column = lambda ref, e: jnp.sum(jnp.where(lane == e, ref[...], 0.0), axis=-1, keepdims=True)
    weight = [column(cw_ref, e) for e in range(N_EXPERTS)]
    pos = [column(pos_ref, e) for e in range(N_EXPERTS)]

    def scatter_matrix(e, k):
        return jnp.where(pos[e] == slot_col + float(k * MOE_CHUNK), weight[e], 0.0).astype(BF16)

    first = jnp.concatenate([scatter_matrix(e, 0) for e in range(N_EXPERTS)], axis=1)
    copies = copies_of(b)
    for _, _, k, _, copy in copies:
        if k == 0:
            copy.wait()
    firsts = buf_ref[b % 2, 0:N_EXPERTS].reshape(N_EXPERTS * MOE_CHUNK, D_MODEL)
    acc_ref[...] = jnp.dot(first, firsts, preferred_element_type=F32)
    for pred, e, k, idx, copy in copies:
        if k > 0:
            @pl.when(pred)
            def _(e=e, k=k, idx=idx, copy=copy):
                copy.wait()
                acc_ref[...] += jnp.dot(scatter_matrix(e, k), buf_ref[b % 2, idx], preferred_element_type=F32)

    o_ref[...] = _layer_norm(DEEPNORM_ALPHA * x_ref[...] + acc_ref[...], g_ref[...], b_ref[...])


def _combine(x, ys, cw, pos, seg_off, cnt, g, b):
    n = x.shape[0]
    n_chunks = N_EXPERTS * (ROW_TILE // MOE_CHUNK)
    tok = lambda w: pl.BlockSpec((ROW_TILE, w), lambda i, *_: (i, 0))
    vec = pl.BlockSpec((1, D_MODEL), lambda i, *_: (0, 0))
    grid_spec = pltpu.PrefetchScalarGridSpec(
        num_scalar_prefetch=2, grid=(n // ROW_TILE,),
        in_specs=[tok(D_MODEL), tok(LANES), tok(LANES), vec, vec, pl.BlockSpec(memory_space=pl.ANY)],
        out_specs=tok(D_MODEL),
        scratch_shapes=[pltpu.VMEM((2, n_chunks, MOE_CHUNK, D_MODEL), BF16),
                        pltpu.VMEM((ROW_TILE, D_MODEL), F32),
                        pltpu.SemaphoreType.DMA((2, n_chunks))])
    return pl.pallas_call(
        _combine_kernel, grid_spec=grid_spec,
        out_shape=jax.ShapeDtypeStruct((n, D_MODEL), F32),
        compiler_params=_params("arbitrary"), name="moe_combine_ln2",
    )(seg_off, cnt.reshape(-1), x, cw, pos, g.reshape(1, -1), b.reshape(1, -1), ys)


def _moe(x, w_router, w_gate, w_up, w_down, g, b):
    n = x.shape[0]
    nblk = n // ROW_TILE
    max_rows = 2 * n + nblk * N_EXPERTS * (SEG_ALIGN - 1) + N_EXPERTS * (MOE_CHUNK + MOE_ROW_TILE - 1)
    n_tiles = max_rows // MOE_ROW_TILE
    cw, pos, post, cnt = _router(x, w_router)
    seg_off, tile_expert, occupied = _moe_layout(cnt, n_tiles)
    xs = _dispatch(x, post, seg_off, cnt, n_tiles * MOE_ROW_TILE)
    ys = _moe_ffn(xs, tile_expert, occupied, w_gate, w_up, w_down)
    return _combine(x, ys, cw, pos, seg_off, cnt, g, b)


def kernel(x, emb_ln_g, emb_ln_b, w_in, b_gate, na_rpb, sw_sink, w_branch_na, w_branch_sw, w_out,
           ln1_g, ln1_b, ffn_w_gate, ffn_w_up, ffn_w_down, moe_router, moe_w_gate, moe_w_up,
           moe_w_down, ln2_g, ln2_b):
    batch, seq, d = x.shape
    assert (seq, d) == (SEQ, D_MODEL)
    n = batch * seq
    tables = _rotary_tables()
    h = x.reshape(n, d)
    for layer in range(DEPTH):
        h, q_na, k_na, v_na, q_sw, k_sw, v_sw, gates = _proj(
            h, w_in[layer].astype(BF16), b_gate[layer], tables,
            embed_ln=(emb_ln_g, emb_ln_b) if layer == 0 else None)
        seq3 = lambda t: t.reshape(batch, seq, t.shape[-1])
        y_na = _na_attention(seq3(q_na), seq3(k_na), seq3(v_na), _na_bias_tables(na_rpb[layer]))
        y_sw = _sw_attention(seq3(q_sw), seq3(k_sw), seq3(v_sw), sw_sink[layer])
        h = _merge(h, y_na.reshape(n, -1), y_sw.reshape(n, -1), gates,
                   w_branch_na[layer].astype(BF16), w_branch_sw[layer].astype(BF16),
                   w_out[layer].astype(BF16), ln1_g[layer], ln1_b[layer])
        i = layer // 2
        if layer % 2 == 0:
            h = _ffn(h, ffn_w_gate[i].astype(BF16), ffn_w_up[i].astype(BF16),
                     ffn_w_down[i].astype(BF16), ln2_g[layer], ln2_b[layer])
        else:
            h = _moe(h, moe_router[i], moe_w_gate[i], moe_w_up[i], moe_w_down[i],
                     ln2_g[layer], ln2_b[layer])
    return h.reshape(batch, seq, d)
```

```python
import functools

import numpy as np
import jax
import jax.numpy as jnp
from jax import lax
from jax.experimental import pallas as pl
from jax.experimental.pallas import tpu as pltpu

F32 = jnp.float32
BF16 = jnp.bfloat16

D_MODEL = 1024
SEQ = 2048
DEPTH = 2
HEAD_DIM = 64
NA_HEADS = 8
NA_WIDTH = NA_HEADS * HEAD_DIM
GRID_W = 64
GRID_ROWS = SEQ // GRID_W
NA_KH = 8
NA_KW = 16
SW_HEADS = 8
SW_KV_HEADS = 2
SW_GROUP = SW_HEADS // SW_KV_HEADS
SW_WIDTH = SW_HEADS * HEAD_DIM
SW_KV_WIDTH = SW_KV_HEADS * HEAD_DIM
SW_WINDOW = 128
SW_BLOCK = 128
ROT_DIM = HEAD_DIM // 4
ROPE_THETA = 500000.0
OFF_QNA = NA_WIDTH
OFF_KNA = 2 * NA_WIDTH
OFF_VNA = 3 * NA_WIDTH
OFF_QSW = OFF_VNA + SW_WIDTH
OFF_KSW = OFF_QSW + SW_KV_WIDTH
OFF_VSW = OFF_KSW + SW_KV_WIDTH
PROJ_COLS = OFF_VSW + 2 * D_MODEL
N_EXPERTS = 8
DEEPNORM_ALPHA = (2 * DEPTH) ** 0.25
LN_EPS = 1e-5
NEG_INF = -1e30
QK_SCALE = HEAD_DIM ** -0.5

LANES = 128
MXU_DIM = 256
V7X_VMEM_BYTES = 64 * 1024 * 1024
VMEM_LIMIT = V7X_VMEM_BYTES * 7 // 8

ROW_TILE = 512
MOE_ROW_TILE = 1536
MOE_FF_TILE = 512
MOE_FF_STEPS = 7
MOE_OUT_TILE = 256
MOE_OUT_STEPS = D_MODEL // MOE_OUT_TILE
MOE_CHUNK = 256
SEG_ALIGN = 16
NA_ROWS_PER_STEP = 4
SW_BLOCKS_PER_STEP = 4


def _layer_norm(z, g, b):
    mu = jnp.mean(z, axis=-1, keepdims=True)
    d = z - mu
    var = jnp.mean(d * d, axis=-1, keepdims=True)
    return d * lax.rsqrt(var + LN_EPS) * g + b


def _sigmoid(z):
    return 1.0 / (1.0 + jnp.exp(-z))


def _params(*sem):
    return pltpu.CompilerParams(dimension_semantics=sem, vmem_limit_bytes=VMEM_LIMIT)


def _rotary_tables():
    half = ROT_DIM // 2
    inv_freq = 1.0 / (ROPE_THETA ** (jnp.arange(0, ROT_DIM, 2, dtype=F32) / ROT_DIM))
    ang = jnp.arange(SEQ, dtype=jnp.int32).astype(F32)[:, None] * inv_freq[None, :]
    cos, sin = jnp.cos(ang), jnp.sin(ang)
    ones = jnp.ones((SEQ, HEAD_DIM - ROT_DIM), F32)
    zeros = jnp.zeros((SEQ, HEAD_DIM - ROT_DIM), F32)
    zh = jnp.zeros((SEQ, half), F32)
    cos_h = jnp.concatenate([cos, cos, ones], axis=1)
    sa_h = jnp.concatenate([-sin, zh, zeros], axis=1)
    sb_h = jnp.concatenate([zh, sin, zeros], axis=1)
    two = lambda t: jnp.concatenate([t, t], axis=1)
    return two(cos_h), two(sa_h), two(sb_h)


def _proj_kernel(*refs, embed_ln):
    if embed_ln:
        x_ref, g_ref, b_ref, w_ref, bg_ref, cos_ref, sa_ref, sb_ref, xn_ref, *outs = refs
        x = _layer_norm(x_ref[...], g_ref[...], b_ref[...])
        xn_ref[...] = x
    else:
        x_ref, w_ref, bg_ref, cos_ref, sa_ref, sb_ref, *outs = refs
        x = x_ref[...]
    qna_ref, kna_ref, vna_ref, qsw_ref, ksw_ref, vsw_ref, gate_ref = outs
    xb = x.astype(BF16)

    def mm(lo, hi):
        return jnp.dot(xb, w_ref[:, lo:hi], preferred_element_type=F32)

    qna_ref[...] = (mm(0, OFF_QNA) * QK_SCALE).astype(BF16)
    kna_ref[...] = mm(OFF_QNA, OFF_KNA).astype(BF16)
    vna_ref[...] = mm(OFF_KNA, OFF_VNA).astype(BF16)

    cos, sa, sb = cos_ref[...], sa_ref[...], sb_ref[...]
    half = ROT_DIM // 2

    def rot(t):
        return t * cos + pltpu.roll(t, LANES - half, 1) * sa + pltpu.roll(t, half, 1) * sb

    q = mm(OFF_VNA, OFF_QSW)
    for c in range(SW_WIDTH // LANES):
        sl = slice(c * LANES, (c + 1) * LANES)
        qsw_ref[:, sl] = (rot(q[:, sl]) * QK_SCALE).astype(BF16)

    kv = mm(OFF_QSW, OFF_VSW)
    k = rot(kv[:, :SW_KV_WIDTH])
    v = kv[:, SW_KV_WIDTH:]
    first = lax.broadcasted_iota(jnp.int32, k.shape, 1) < HEAD_DIM

    def dup(t):
        r = pltpu.roll(t, HEAD_DIM, 1)
        return jnp.concatenate([jnp.where(first, t, r), jnp.where(first, r, t)], axis=1)

    ksw_ref[...] = dup(k).astype(BF16)
    vsw_ref[...] = v.astype(BF16)

    gw = 512
    for c in range(2 * D_MODEL // gw):
        z = mm(OFF_VSW + c * gw, OFF_VSW + (c + 1) * gw) + bg_ref[:, c * gw:(c + 1) * gw]
        gate_ref[:, c * gw:(c + 1) * gw] = _sigmoid(z).astype(BF16)


def _proj(x, w_bf, b_gate, tables, embed_ln=None):
    n = x.shape[0]
    tiles_per_seq = SEQ // ROW_TILE
    row = lambda w: pl.BlockSpec((ROW_TILE, w), lambda i: (i, 0))
    full = lambda a: pl.BlockSpec(a.shape, lambda i: (0,) * a.ndim)
    tab = pl.BlockSpec((ROW_TILE, LANES), lambda i: (i % tiles_per_seq, 0))
    bg = b_gate.reshape(1, -1)
    widths = (NA_WIDTH, NA_WIDTH, NA_WIDTH, SW_WIDTH, 2 * SW_KV_WIDTH, SW_KV_WIDTH, 2 * D_MODEL)
    ln = [t.reshape(1, -1) for t in embed_ln] if embed_ln else []
    out_specs = [row(w) for w in widths]
    out_shape = [jax.ShapeDtypeStruct((n, w), BF16) for w in widths]
    if embed_ln:
        out_specs = [row(D_MODEL)] + out_specs
        out_shape = [jax.ShapeDtypeStruct((n, D_MODEL), F32)] + out_shape
    outs = pl.pallas_call(
        functools.partial(_proj_kernel, embed_ln=bool(embed_ln)), grid=(n // ROW_TILE,),
        in_specs=[row(D_MODEL)] + [full(t) for t in ln] + [full(w_bf), full(bg), tab, tab, tab],
        out_specs=out_specs, out_shape=out_shape,
        compiler_params=_params("parallel"), name="in_proj",
    )(x, *ln, w_bf, bg, *tables)
    return outs if embed_ln else [x] + list(outs)


def _na_bias_tables(rpb):
    c = np.arange(GRID_W)
    qcs = np.clip(c - NA_KW // 2, 0, GRID_W - NA_KW)
    valid = (c[None, :] >= qcs[:, None]) & (c[None, :] < qcs[:, None] + NA_KW)
    n_dr, n_dc = 2 * NA_KH - 1, 2 * NA_KW - 1
    span = 2 * GRID_W - 1
    lead = GRID_W - NA_KW
    ext = jnp.pad(rpb.astype(F32), ((0, 0), (0, 0), (lead, span + 1 - lead - n_dc)))
    flat = jnp.broadcast_to(ext[:, :, None, :], (NA_HEADS, n_dr, GRID_W, span + 1))
    flat = flat.reshape(NA_HEADS, n_dr, GRID_W * (span + 1))[:, :, :GRID_W * span]
    t = flat.reshape(NA_HEADS, n_dr, GRID_W, span)[:, :, :, GRID_W - 1:]
    t = jnp.where(valid[None, None], t, NEG_INF)
    pairs = jnp.concatenate([t[:, :-1], t[:, 1:]], axis=-1)
    pairs = pairs.reshape(2, 4, n_dr - 1, GRID_W, 2 * GRID_W).transpose(0, 2, 1, 3, 4)
    return pairs.reshape(2, n_dr - 1, 4 * GRID_W, 2 * GRID_W)


def _na_kernel(q_ref, k_ref, v_ref, bias_ref, o_ref):
    lane = lax.broadcasted_iota(jnp.int32, (GRID_W, MXU_DIM), 1)
    head_mask = [(lane >= h * HEAD_DIM) & (lane < (h + 1) * HEAD_DIM) for h in range(4)]
    nk = NA_KH * GRID_W

    def rows(it, carry):
        chains = []
        for u in range(NA_ROWS_PER_STEP):
            r = it * NA_ROWS_PER_STEP + u
            rs = jnp.clip(r - NA_KH // 2, 0, GRID_ROWS - NA_KH)
            variant = rs - r + (NA_KH - 1)
            qrows = pl.ds(pl.multiple_of(r * GRID_W, GRID_W), GRID_W)
            krows = pl.ds(pl.multiple_of(rs * GRID_W, GRID_W), nk)
            q = q_ref[qrows, :]
            for g in range(2):
                gs = slice(g * MXU_DIM, (g + 1) * MXU_DIM)
                qg = q[:, gs]
                lhs = jnp.concatenate([jnp.where(m, qg, jnp.zeros_like(qg)) for m in head_mask], axis=0)
                bias = jnp.concatenate([bias_ref[g, variant + 2 * a] for a in range(NA_KH // 2)], axis=1)
                chains.append(dict(lhs=lhs, k=k_ref[krows, gs], v=v_ref[krows, gs], bias=bias,
                                   out=(qrows, gs)))
        for c in chains:
            c["s"] = lax.dot_general(c["lhs"], c["k"], (((1,), (1,)), ((), ())),
                                     preferred_element_type=F32) + c["bias"]
        for c in chains:
            c["p"] = jnp.exp(c["s"] - jnp.max(c["s"], axis=-1, keepdims=True))
        for c in chains:
            c["l"] = jnp.sum(c["p"], axis=-1, keepdims=True)
        for c in chains:
            c["o"] = jnp.dot(c["p"].astype(BF16), c["v"], preferred_element_type=F32) / c["l"]
        for c in chains:
            og = jnp.zeros((GRID_W, MXU_DIM), F32)
            for h in range(4):
                og = og + jnp.where(head_mask[h], c["o"][h * GRID_W:(h + 1) * GRID_W], 0.0)
            o_ref[c["out"]] = og.astype(BF16)
        return carry

    lax.fori_loop(0, GRID_ROWS // NA_ROWS_PER_STEP, rows, 0)


def _na_attention(q, k, v, bias):
    b = q.shape[0]
    seq = pl.BlockSpec((None, SEQ, NA_WIDTH), lambda i: (i, 0, 0))
    return pl.pallas_call(
        _na_kernel, grid=(b,),
        in_specs=[seq, seq, seq, pl.BlockSpec(bias.shape, lambda i: (0, 0, 0, 0))],
        out_specs=seq, out_shape=jax.ShapeDtypeStruct((b, SEQ, NA_WIDTH), BF16),
        compiler_params=_params("parallel"), name="na_attention",
    )(q, k, v, bias)


def _sw_masks():
    nkeys = 3 * SW_BLOCK
    jk = np.arange(nkeys)[:, None]
    iq = np.arange(SW_BLOCK)[None, :]
    band = (jk >= iq) & (jk <= iq + 2 * SW_WINDOW)
    in_seq = [(jk >= SW_BLOCK), np.ones_like(band), (jk < 2 * SW_BLOCK)]
    return jnp.asarray(np.stack([np.where(band & ok, 0.0, NEG_INF) for ok in in_seq]), F32)


def _sw_kernel(sink_ref, q_ref, k_ref, v_ref, mask_ref, o_ref):
    nb = SEQ // SW_BLOCK
    first = lax.broadcasted_iota(jnp.int32, (SW_BLOCK, LANES), 1) < HEAD_DIM
    ones = jnp.ones((8, 3 * SW_BLOCK), BF16)

    def blocks(it, carry):
        chains = []
        for u in range(SW_BLOCKS_PER_STEP):
            n = it * SW_BLOCKS_PER_STEP + u
            mask = mask_ref[jnp.where(n == 0, 0, jnp.where(n == nb - 1, 2, 1))]
            starts = [pl.multiple_of(jnp.clip(n + d, 0, nb - 1) * SW_BLOCK, SW_BLOCK) for d in (-1, 0, 1)]
            qrows = pl.ds(pl.multiple_of(n * SW_BLOCK, SW_BLOCK), SW_BLOCK)
            q = q_ref[qrows, :]
            vv = jnp.concatenate([v_ref[pl.ds(s, SW_BLOCK), :] for s in starts], axis=0)
            vt = vv.astype(F32).T.astype(BF16)
            for kvh in range(SW_KV_HEADS):
                ks = slice(kvh * LANES, (kvh + 1) * LANES)
                kk = jnp.concatenate([k_ref[pl.ds(s, SW_BLOCK), ks] for s in starts], axis=0)
                parts, sinks = [], []
                for j in range(SW_GROUP):
                    h = kvh * SW_GROUP + j
                    qc = q[:, (h // 2) * LANES:(h // 2 + 1) * LANES]
                    keep = first if h % 2 == 0 else jnp.logical_not(first)
                    parts.append(jnp.where(keep, qc, jnp.zeros_like(qc)))
                    sinks.append(jnp.full((1, SW_BLOCK), sink_ref[h], F32))
                vt_ones = jnp.concatenate([vt[kvh * HEAD_DIM:(kvh + 1) * HEAD_DIM], ones], axis=0)
                chains.append(dict(k=kk, vt=vt_ones, q=jnp.concatenate(parts, axis=0),
                                   sink=jnp.concatenate(sinks, axis=1), mask=mask, qrows=qrows, kvh=kvh))
        for c in chains:
            s = lax.dot_general(c["k"], c["q"], (((1,), (1,)), ((), ())), preferred_element_type=F32)
            prev_mask, next_mask = c["mask"][:SW_BLOCK], c["mask"][2 * SW_BLOCK:]
            cols = []
            for j in range(SW_GROUP):
                sj = s[:, j * SW_BLOCK:(j + 1) * SW_BLOCK]
                cols.append(jnp.concatenate([sj[:SW_BLOCK] + prev_mask, sj[SW_BLOCK:2 * SW_BLOCK],
                                             sj[2 * SW_BLOCK:] + next_mask], axis=0))
            c["s"] = jnp.concatenate(cols, axis=1)
        for c in chains:
            c["m"] = jnp.maximum(jnp.max(c["s"], axis=0, keepdims=True), c["sink"])
        for c in chains:
            c["p"] = jnp.exp(c["s"] - c["m"]).astype(BF16)
        for c in chains:
            o = jnp.dot(c["vt"], c["p"], preferred_element_type=F32)
            denom = o[HEAD_DIM:HEAD_DIM + 1] + jnp.exp(c["sink"] - c["m"])
            c["o"] = o[:HEAD_DIM] / denom
        for c in chains:
            for pair in range(SW_GROUP // 2):
                even = c["o"][:, (2 * pair) * SW_BLOCK:(2 * pair + 1) * SW_BLOCK]
                odd = c["o"][:, (2 * pair + 1) * SW_BLOCK:(2 * pair + 2) * SW_BLOCK]
                col = (c["kvh"] * (SW_GROUP // 2) + pair) * LANES
                o_ref[c["qrows"], col:col + LANES] = jnp.concatenate([even, odd], axis=0).T.astype(BF16)
        return carry

    lax.fori_loop(0, nb // SW_BLOCKS_PER_STEP, blocks, 0)


def _sw_attention(q, k2, v, sink):
    b = q.shape[0]
    masks = _sw_masks()
    seq = lambda w: pl.BlockSpec((None, SEQ, w), lambda i: (i, 0, 0))
    return pl.pallas_call(
        _sw_kernel, grid=(b,),
        in_specs=[pl.BlockSpec(memory_space=pltpu.SMEM), seq(SW_WIDTH), seq(2 * SW_KV_WIDTH),
                  seq(SW_KV_WIDTH), pl.BlockSpec(masks.shape, lambda i: (0, 0, 0))],
        out_specs=seq(SW_WIDTH), out_shape=jax.ShapeDtypeStruct((b, SEQ, SW_WIDTH), BF16),
        compiler_params=_params("parallel"), name="sw_attention",
    )(sink.astype(F32), q, k2, v, masks)


def _merge_kernel(x_ref, yna_ref, ysw_ref, gate_ref, wna_ref, wsw_ref, wout_ref, g_ref, b_ref, o_ref):
    a = jnp.dot(yna_ref[...], wna_ref[...], preferred_element_type=F32)
    s = jnp.dot(ysw_ref[...], wsw_ref[...], preferred_element_type=F32)
    m = gate_ref[:, :D_MODEL].astype(F32) * a + gate_ref[:, D_MODEL:].astype(F32) * s
    z = DEEPNORM_ALPHA * x_ref[...] + jnp.dot(m.astype(BF16), wout_ref[...], preferred_element_type=F32)
    o_ref[...] = _layer_norm(z, g_ref[...], b_ref[...])


def _merge(x, y_na, y_sw, gates, w_na, w_sw, w_out, g, b):
    n = x.shape[0]
    row = lambda w: pl.BlockSpec((ROW_TILE, w), lambda i: (i, 0))
    full = lambda a: pl.BlockSpec(a.shape, lambda i: (0,) * a.ndim)
    g2, b2 = g.reshape(1, -1), b.reshape(1, -1)
    return pl.pallas_call(
        _merge_kernel, grid=(n // ROW_TILE,),
        in_specs=[row(D_MODEL), row(NA_WIDTH), row(SW_WIDTH), row(2 * D_MODEL),
                  full(w_na), full(w_sw), full(w_out), full(g2), full(b2)],
        out_specs=row(D_MODEL), out_shape=jax.ShapeDtypeStruct((n, D_MODEL), F32),
        compiler_params=_params("parallel"), name="merge_ln1",
    )(x, y_na, y_sw, gates, w_na, w_sw, w_out, g2, b2)


def _router_kernel(x_ref, w_ref, cw_ref, pos_ref, post_ref, cnt_ref):
    x, w = x_ref[...], w_ref[...]
    xh, wh = x.astype(BF16), w.astype(BF16)
    xl = (x - xh.astype(F32)).astype(BF16)
    wl = (w - wh.astype(F32)).astype(BF16)
    dot = functools.partial(jnp.dot, preferred_element_type=F32)
    logits = dot(xh, wh) + dot(xl, wh) + dot(xh, wl)
    lane = lax.broadcasted_iota(jnp.int32, logits.shape, 1).astype(F32)
    lg = jnp.where(lane < N_EXPERTS, logits, -jnp.inf)
    m1 = jnp.max(lg, axis=-1, keepdims=True)
    i1 = jnp.min(jnp.where(lg == m1, lane, float(LANES)), axis=-1, keepdims=True)
    lg2 = jnp.where(lane == i1, -jnp.inf, lg)
    m2 = jnp.max(lg2, axis=-1, keepdims=True)
    i2 = jnp.min(jnp.where(lg2 == m2, lane, float(LANES)), axis=-1, keepdims=True)
    e2 = jnp.exp(m2 - m1)
    denom = 1.0 + e2
    cw = jnp.where(lane == i1, 1.0 / denom, 0.0) + jnp.where(lane == i2, e2 / denom, 0.0)
    cw_ref[...] = cw

    t = cw.shape[0]
    sel = jnp.where(cw.T[:N_EXPERTS] > 0.0, 1.0, 0.0)
    tri = jnp.where(lax.broadcasted_iota(jnp.int32, (t, t), 0) <= lax.broadcasted_iota(jnp.int32, (t, t), 1),
                    1.0, 0.0).astype(BF16)
    incl = dot(sel.astype(BF16), tri)
    post = jnp.where(sel > 0.0, incl - 1.0, -1.0)
    post_ref[...] = post
    pos_ref[...] = jnp.concatenate([post, jnp.full((LANES - N_EXPERTS, t), -1.0, F32)], axis=0).T
    cnt_ref[...] = jnp.broadcast_to(incl[:, t - 1:t], (N_EXPERTS, LANES))


def _router(x, w_router):
    n = x.shape[0]
    nblk = n // ROW_TILE
    w = jnp.zeros((D_MODEL, LANES), F32).at[:, :N_EXPERTS].set(w_router)
    tok = pl.BlockSpec((ROW_TILE, LANES), lambda i: (i, 0))
    cw, pos, post, cnt = pl.pallas_call(
        _router_kernel, grid=(nblk,),
        in_specs=[pl.BlockSpec((ROW_TILE, D_MODEL), lambda i: (i, 0)),
                  pl.BlockSpec((D_MODEL, LANES), lambda i: (0, 0))],
        out_specs=[tok, tok, pl.BlockSpec((N_EXPERTS, ROW_TILE), lambda i: (0, i)),
                   pl.BlockSpec((N_EXPERTS, LANES), lambda i: (i, 0))],
        out_shape=[jax.ShapeDtypeStruct((n, LANES), F32), jax.ShapeDtypeStruct((n, LANES), F32),
                   jax.ShapeDtypeStruct((N_EXPERTS, n), F32),
                   jax.ShapeDtypeStruct((nblk * N_EXPERTS, LANES), F32)],
        compiler_params=_params("parallel"), name="router_top2",
    )(x, w)
    return cw, pos, post, cnt[:, 0].astype(jnp.int32).reshape(nblk, N_EXPERTS)


def _moe_layout(cnt, n_tiles):
    seg = (cnt + SEG_ALIGN - 1) // SEG_ALIGN * SEG_ALIGN
    rows_e = jnp.sum(seg, axis=0)
    tiles_e = (rows_e + MOE_CHUNK + MOE_ROW_TILE - 1) // MOE_ROW_TILE
    tile_end = jnp.cumsum(tiles_e)
    tile_off = tile_end - tiles_e
    seg_off = (tile_off * MOE_ROW_TILE)[None, :] + jnp.cumsum(seg, axis=0) - seg
    tile = jnp.arange(n_tiles, dtype=jnp.int32)
    tile_expert = jnp.minimum(jnp.sum(tile[:, None] >= tile_end[None, :], axis=1), N_EXPERTS - 1)
    occupied = jnp.clip(rows_e[tile_expert] - (tile - tile_off[tile_expert]) * MOE_ROW_TILE, 0, MOE_ROW_TILE)
    as_i32 = lambda t: t.astype(jnp.int32)
    return as_i32(seg_off).reshape(-1), as_i32(tile_expert), as_i32(occupied)


def _chunk_copies(off_ref, cnt_ref, block, hbm_ref, vmem_ref, sems, to_hbm):
    slot = block % 2
    out = []
    for e in range(N_EXPERTS):
        c = cnt_ref[block * N_EXPERTS + e]
        off = pl.multiple_of(off_ref[block * N_EXPERTS + e], SEG_ALIGN)
        for k in range(ROW_TILE // MOE_CHUNK):
            idx = k * N_EXPERTS + e
            rows = hbm_ref.at[pl.ds(off + k * MOE_CHUNK, MOE_CHUNK)]
            staged = vmem_ref.at[slot, idx]
            src, dst = (staged, rows) if to_hbm else (rows, staged)
            out.append((c > k * MOE_CHUNK, e, k, idx, pltpu.make_async_copy(src, dst, sems.at[slot, idx])))
    return out


def _dispatch_kernel(off_ref, cnt_ref, x_ref, post_ref, xs_in_ref, xs_ref, stage_ref, sems):
    del xs_in_ref
    b, nblk = pl.program_id(0), pl.num_programs(0)
    copies_of = lambda blk: _chunk_copies(off_ref, cnt_ref, blk, xs_ref, stage_ref, sems, to_hbm=True)

    def wait_all(blk):
        for pred, _, _, _, copy in copies_of(blk):
            pl.when(pred)(copy.wait)

    xb = x_ref[...].astype(BF16)
    slot_row = lax.broadcasted_iota(jnp.int32, (MOE_CHUNK, ROW_TILE), 0).astype(F32)
    copies = copies_of(b)
    for pred, e, k, idx, _ in copies:
        @pl.when(pred)
        def _(e=e, k=k, idx=idx):
            onehot = jnp.where(post_ref[e:e + 1, :] == slot_row + float(k * MOE_CHUNK), 1.0, 0.0).astype(BF16)
            stage_ref[b % 2, idx] = jnp.dot(onehot, xb, preferred_element_type=F32).astype(BF16)
    pl.when(b >= 1)(lambda: wait_all(b - 1))
    for pred, _, _, _, copy in copies:
        pl.when(pred)(copy.start)
    pl.when(b == nblk - 1)(lambda: wait_all(b))


def _dispatch(x, post, seg_off, cnt, n_rows):
    n = x.shape[0]
    n_chunks = N_EXPERTS * (ROW_TILE // MOE_CHUNK)
    grid_spec = pltpu.PrefetchScalarGridSpec(
        num_scalar_prefetch=2, grid=(n // ROW_TILE,),
        in_specs=[pl.BlockSpec((ROW_TILE, D_MODEL), lambda i, *_: (i, 0)),
                  pl.BlockSpec((N_EXPERTS, ROW_TILE), lambda i, *_: (0, i)),
                  pl.BlockSpec(memory_space=pl.ANY)],
        out_specs=pl.BlockSpec(memory_space=pl.ANY),
        scratch_shapes=[pltpu.VMEM((2, n_chunks, MOE_CHUNK, D_MODEL), BF16),
                        pltpu.SemaphoreType.DMA((2, n_chunks))])
    return pl.pallas_call(
        _dispatch_kernel, grid_spec=grid_spec,
        out_shape=jax.ShapeDtypeStruct((n_rows, D_MODEL), BF16),
        input_output_aliases={4: 0},
        compiler_params=_params("arbitrary"), name="moe_dispatch",
    )(seg_off, cnt.reshape(-1), x, post, jnp.zeros((n_rows, D_MODEL), BF16))


def _swiglu_partial(xb, wg, wu, wd):
    hg = jnp.dot(xb, wg, preferred_element_type=F32)
    hu = jnp.dot(xb, wu, preferred_element_type=F32)
    h = hg * _sigmoid(hg) * hu
    return jnp.dot(h.astype(BF16), wd, preferred_element_type=F32)


def _moe_ffn_kernel(expert_ref, occ_ref, x_ref, wg_ref, wu_ref, wd_ref, o_ref, h_ref):
    del expert_ref
    i, j = pl.program_id(0), pl.program_id(1)
    occupied = occ_ref[i]
    full = occupied == MOE_ROW_TILE
    partial = (occupied > 0) & (occupied < MOE_ROW_TILE)
    fill = j < MOE_FF_STEPS
    all_rows = slice(0, MOE_ROW_TILE)
    chunks = [slice(s * MOE_CHUNK, (s + 1) * MOE_CHUNK) for s in range(MOE_ROW_TILE // MOE_CHUNK)]

    def gate_up(rows, wg, wu):
        x = x_ref[rows, :]
        hg = jnp.dot(x, wg, preferred_element_type=F32)
        hu = jnp.dot(x, wu, preferred_element_type=F32)
        h_ref[jnp.minimum(j, MOE_FF_STEPS - 1), rows, :] = (hg * _sigmoid(hg) * hu).astype(BF16)

    def down(rows, wd):
        h = jnp.concatenate([h_ref[t, rows, :] for t in range(MOE_FF_STEPS)], axis=1)
        o_ref[rows, :] = jnp.dot(h, wd, preferred_element_type=F32).astype(BF16)

    @pl.when(fill & full)
    def _():
        gate_up(all_rows, wg_ref[...].astype(BF16), wu_ref[...].astype(BF16))

    @pl.when(fill & partial)
    def _():
        wg, wu = wg_ref[...].astype(BF16), wu_ref[...].astype(BF16)
        for rows in chunks:
            pl.when(rows.start < occupied)(functools.partial(gate_up, rows, wg, wu))

    @pl.when(jnp.logical_not(fill) & full)
    def _():
        down(all_rows, wd_ref[...].astype(BF16))

    @pl.when(jnp.logical_not(fill) & jnp.logical_not(full))
    def _():
        o_ref[...] = jnp.zeros_like(o_ref)

    @pl.when(jnp.logical_not(fill) & partial)
    def _():
        wd = wd_ref[...].astype(BF16)
        for rows in chunks:
            pl.when(rows.start < occupied)(functools.partial(down, rows, wd))


def _moe_ffn(xs, tile_expert, occupied, w_gate, w_up, w_down):
    n_rows = xs.shape[0]
    d_ff = w_gate.shape[-1]
    assert d_ff == MOE_FF_STEPS * MOE_FF_TILE
    ff_step = lambda j: jnp.minimum(j, MOE_FF_STEPS - 1)
    out_step = lambda j: jnp.maximum(j - MOE_FF_STEPS, 0)
    row = pl.BlockSpec((MOE_ROW_TILE, D_MODEL), lambda i, j, *_: (i, 0))
    up = pl.BlockSpec((None, D_MODEL, MOE_FF_TILE), lambda i, j, ex, occ: (ex[i], 0, ff_step(j)))
    down = pl.BlockSpec((None, d_ff, MOE_OUT_TILE), lambda i, j, ex, occ: (ex[i], 0, out_step(j)))
    out = pl.BlockSpec((MOE_ROW_TILE, MOE_OUT_TILE), lambda i, j, *_: (i, out_step(j)))
    grid_spec = pltpu.PrefetchScalarGridSpec(
        num_scalar_prefetch=2, grid=(n_rows // MOE_ROW_TILE, MOE_FF_STEPS + MOE_OUT_STEPS),
        in_specs=[row, up, up, down], out_specs=out,
        scratch_shapes=[pltpu.VMEM((MOE_FF_STEPS, MOE_ROW_TILE, MOE_FF_TILE), BF16)])
    return pl.pallas_call(
        _moe_ffn_kernel, grid_spec=grid_spec,
        out_shape=jax.ShapeDtypeStruct((n_rows, D_MODEL), BF16),
        compiler_params=_params("parallel", "arbitrary"), name="moe_swiglu",
    )(tile_expert, occupied, xs, w_gate, w_up, w_down)


def _ffn_kernel(x_ref, wg_ref, wu_ref, wd_ref, g_ref, b_ref, o_ref):
    x = x_ref[...]
    y = _swiglu_partial(x.astype(BF16), wg_ref[...], wu_ref[...], wd_ref[...])
    o_ref[...] = _layer_norm(DEEPNORM_ALPHA * x + y, g_ref[...], b_ref[...])


def _ffn(x, w_gate, w_up, w_down, g, b):
    n = x.shape[0]
    row = pl.BlockSpec((ROW_TILE, D_MODEL), lambda i: (i, 0))
    const = lambda a: pl.BlockSpec(a.shape, lambda i: (0,) * a.ndim, pipeline_mode=pl.Buffered(1))
    g2, b2 = g.reshape(1, -1), b.reshape(1, -1)
    return pl.pallas_call(
        _ffn_kernel, grid=(n // ROW_TILE,),
        in_specs=[row, const(w_gate), const(w_up), const(w_down), const(g2), const(b2)], out_specs=row,
        out_shape=jax.ShapeDtypeStruct((n, D_MODEL), F32),
        compiler_params=_params("parallel"), name="swiglu_ln2",
    )(x, w_gate, w_up, w_down, g2, b2)


def _combine_kernel(off_ref, cnt_ref, x_ref, cw_ref, pos_ref, g_ref, b_ref, ys_ref, o_ref,
                    buf_ref, acc_ref, sems):
    b, nblk = pl.program_id(0), pl.num_programs(0)
    copies_of = lambda blk: _chunk_copies(off_ref, cnt_ref, blk, ys_ref, buf_ref, sems, to_hbm=False)

    def start_all(blk):
        for pred, _, k, _, copy in copies_of(blk):
            if k == 0:
                copy.start()
            else:
                pl.when(pred)(copy.start)

    pl.when(b == 0)(lambda: start_all(b))
    pl.when(b + 1 < nblk)(lambda: start_all(b + 1))

    lane = lax.broadcasted_iota(jnp.int32, (ROW_TILE, LANES), 1)
    slot_col = lax.broadcasted_iota(jnp.int32, (ROW_TILE, MOE_CHUNK), 1).astype(F32)
    column = lambda ref, e: jnp.sum(jnp.where(lane == e, ref[...], 0.0), axis=-1, keepdims=True)
    weight = [column(cw_ref, e) for e in range(N_EXPERTS)]
    pos = [column(pos_ref, e) for e in range(N_EXPERTS)]

    def scatter_matrix(e, k):
        return jnp.where(pos[e] == slot_col + float(k * MOE_CHUNK), weight[e], 0.0).astype(BF16)

    first = jnp.concatenate([scatter_matrix(e, 0) for e in range(N_EXPERTS)], axis=1)
    copies = copies_of(b)
    for _, _, k, _, copy in copies:
        if k == 0:
            copy.wait()
    firsts = buf_ref[b % 2, 0:N_EXPERTS].reshape(N_EXPERTS * MOE_CHUNK, D_MODEL)
    acc_ref[...] = jnp.dot(first, firsts, preferred_element_type=F32)
    for pred, e, k, idx, copy in copies:
        if k > 0:
            @pl.when(pred)
            def _(e=e, k=k, idx=idx, copy=copy):
                copy.wait()
                acc_ref[...] += jnp.dot(scatter_matrix(e, k), buf_ref[b % 2, idx], preferred_element_type=F32)

    o_ref[...] = _layer_norm(DEEPNORM_ALPHA * x_ref[...] + acc_ref[...], g_ref[...], b_ref[...])


def _combine(x, ys, cw, pos, seg_off, cnt, g, b):
    n = x.shape[0]
    n_chunks = N_EXPERTS * (ROW_TILE // MOE_CHUNK)
    tok = lambda w: pl.BlockSpec((ROW_TILE, w), lambda i, *_: (i, 0))
    vec = pl.BlockSpec((1, D_MODEL), lambda i, *_: (0, 0))
    grid_spec = pltpu.PrefetchScalarGridSpec(
        num_scalar_prefetch=2, grid=(n // ROW_TILE,),
        in_specs=[tok(D_MODEL), tok(LANES), tok(LANES), vec, vec, pl.BlockSpec(memory_space=pl.ANY)],
        out_specs=tok(D_MODEL),
        scratch_shapes=[pltpu.VMEM((2, n_chunks, MOE_CHUNK, D_MODEL), BF16),
                        pltpu.VMEM((ROW_TILE, D_MODEL), F32),
                        pltpu.SemaphoreType.DMA((2, n_chunks))])
    return pl.pallas_call(
        _combine_kernel, grid_spec=grid_spec,
        out_shape=jax.ShapeDtypeStruct((n, D_MODEL), F32),
        compiler_params=_params("arbitrary"), name="moe_combine_ln2",
    )(seg_off, cnt.reshape(-1), x, cw, pos, g.reshape(1, -1), b.reshape(1, -1), ys)


def _moe(x, w_router, w_gate, w_up, w_down, g, b):
    n = x.shape[0]
    nblk = n // ROW_TILE
    max_rows = 2 * n + nblk * N_EXPERTS * (SEG_ALIGN - 1) + N_EXPERTS * (MOE_CHUNK + MOE_ROW_TILE - 1)
    n_tiles = max_rows // MOE_ROW_TILE
    cw, pos, post, cnt = _router(x, w_router)
    seg_off, tile_expert, occupied = _moe_layout(cnt, n_tiles)
    xs = _dispatch(x, post, seg_off, cnt, n_tiles * MOE_ROW_TILE)
    ys = _moe_ffn(xs, tile_expert, occupied, w_gate, w_up, w_down)
    return _combine(x, ys, cw, pos, seg_off, cnt, g, b)


def kernel(x, emb_ln_g, emb_ln_b, w_in, b_gate, na_rpb, sw_sink, w_branch_na, w_branch_sw, w_out,
           ln1_g, ln1_b, ffn_w_gate, ffn_w_up, ffn_w_down, moe_router, moe_w_gate, moe_w_up,
           moe_w_down, ln2_g, ln2_b):
    batch, seq, d = x.shape
    assert (seq, d) == (SEQ, D_MODEL)
    n = batch * seq
    tables = _rotary_tables()
    h = x.reshape(n, d)
    for layer in range(DEPTH):
        h, q_na, k_na, v_na, q_sw, k_sw, v_sw, gates = _proj(
            h, w_in[layer].astype(BF16), b_gate[layer], tables,
            embed_ln=(emb_ln_g, emb_ln_b) if layer == 0 else None)
        seq3 = lambda t: t.reshape(batch, seq, t.shape[-1])
        y_na = _na_attention(seq3(q_na), seq3(k_na), seq3(v_na), _na_bias_tables(na_rpb[layer]))
        y_sw = _sw_attention(seq3(q_sw), seq3(k_sw), seq3(v_sw), sw_sink[layer])
        h = _merge(h, y_na.reshape(n, -1), y_sw.reshape(n, -1), gates,
                   w_branch_na[layer].astype(BF16), w_branch_sw[layer].astype(BF16),
                   w_out[layer].astype(BF16), ln1_g[layer], ln1_b[layer])
        i = layer // 2
        if layer % 2 == 0:
            h = _ffn(h, ffn_w_gate[i].astype(BF16), ffn_w_up[i].astype(BF16),
                     ffn_w_down[i].astype(BF16), ln2_g[layer], ln2_b[layer])
        else:
            h = _moe(h, moe_router[i], moe_w_gate[i], moe_w_up[i], moe_w_down[i],
                     ln2_g[layer], ln2_b[layer])
    return h.reshape(batch, seq, d)
```

```python
import functools

import numpy as np
import jax
import jax.numpy as jnp
from jax import lax
from jax.experimental import pallas as pl
from jax.experimental.pallas import tpu as pltpu

F32 = jnp.float32
BF16 = jnp.bfloat16

D_MODEL = 1024
SEQ = 2048
DEPTH = 2
HEAD_DIM = 64
NA_HEADS = 8
NA_WIDTH = NA_HEADS * HEAD_DIM
GRID_W = 64
GRID_ROWS = SEQ // GRID_W
NA_KH = 8
NA_KW = 16
SW_HEADS = 8
SW_KV_HEADS = 2
SW_GROUP = SW_HEADS // SW_KV_HEADS
SW_WIDTH = SW_HEADS * HEAD_DIM
SW_KV_WIDTH = SW_KV_HEADS * HEAD_DIM
SW_WINDOW = 128
SW_BLOCK = 128
ROT_DIM = HEAD_DIM // 4
ROPE_THETA = 500000.0
OFF_QNA = NA_WIDTH
OFF_KNA = 2 * NA_WIDTH
OFF_VNA = 3 * NA_WIDTH
OFF_QSW = OFF_VNA + SW_WIDTH
OFF_KSW = OFF_QSW + SW_KV_WIDTH
OFF_VSW = OFF_KSW + SW_KV_WIDTH
PROJ_COLS = OFF_VSW + 2 * D_MODEL
N_EXPERTS = 8
DEEPNORM_ALPHA = (2 * DEPTH) ** 0.25
LN_EPS = 1e-5
NEG_INF = -1e30
QK_SCALE = HEAD_DIM ** -0.5
LOG2E = float(np.log2(np.e))

LANES = 128
MXU_DIM = 256
V7X_VMEM_BYTES = 64 * 1024 * 1024
VMEM_LIMIT = V7X_VMEM_BYTES * 7 // 8

PROJ_ROW_TILE = 1024
ROW_TILE = 512
MOE_ROW_TILE = 1536
MOE_FF_TILE = 512
MOE_CHUNK = 256
SEG_ALIGN = 16
NA_ROWS_PER_STEP = 4
SW_BLOCKS_PER_STEP = 4


def _layer_norm(z, g, b):
    mu = jnp.mean(z, axis=-1, keepdims=True)
    d = z - mu
    var = jnp.mean(d * d, axis=-1, keepdims=True)
    return d * lax.rsqrt(var + LN_EPS) * g + b


def _sigmoid(z):
    return 1.0 / (1.0 + jnp.exp(-z))


def _params(*sem):
    return pltpu.CompilerParams(dimension_semantics=sem, vmem_limit_bytes=VMEM_LIMIT)


def _rotary_tables():
    half = ROT_DIM // 2
    inv_freq = 1.0 / (ROPE_THETA ** (jnp.arange(0, ROT_DIM, 2, dtype=F32) / ROT_DIM))
    ang = jnp.arange(SEQ, dtype=jnp.int32).astype(F32)[:, None] * inv_freq[None, :]
    cos, sin = jnp.cos(ang), jnp.sin(ang)
    ones = jnp.ones((SEQ, HEAD_DIM - ROT_DIM), F32)
    zeros = jnp.zeros((SEQ, HEAD_DIM - ROT_DIM), F32)
    zh = jnp.zeros((SEQ, half), F32)
    cos_h = jnp.concatenate([cos, cos, ones], axis=1)
    sa_h = jnp.concatenate([-sin, zh, zeros], axis=1)
    sb_h = jnp.concatenate([zh, sin, zeros], axis=1)
    two = lambda t: jnp.concatenate([t, t], axis=1)
    return two(cos_h), two(sa_h), two(sb_h)


def _proj_kernel(*refs, embed_ln):
    if embed_ln:
        x_ref, g_ref, b_ref, w_ref, bg_ref, cos_ref, sa_ref, sb_ref, xn_ref, *outs = refs
        x = _layer_norm(x_ref[...], g_ref[...], b_ref[...])
        xn_ref[...] = x
    else:
        x_ref, w_ref, bg_ref, cos_ref, sa_ref, sb_ref, *outs = refs
        x = x_ref[...]
    qna_ref, kna_ref, vna_ref, qsw_ref, ksw_ref, vsw_ref, gate_ref = outs
    xb = x.astype(BF16)

    def mm(lo, hi):
        return jnp.dot(xb, w_ref[:, lo:hi], preferred_element_type=F32)

    qna_ref[...] = (mm(0, OFF_QNA) * (QK_SCALE * LOG2E)).astype(BF16)
    kna_ref[...] = mm(OFF_QNA, OFF_KNA).astype(BF16)
    vna_ref[...] = mm(OFF_KNA, OFF_VNA).astype(BF16)

    cos, sa, sb = cos_ref[...], sa_ref[...], sb_ref[...]
    half = ROT_DIM // 2

    def rot(t):
        return t * cos + pltpu.roll(t, LANES - half, 1) * sa + pltpu.roll(t, half, 1) * sb

    q = mm(OFF_VNA, OFF_QSW)
    for c in range(SW_WIDTH // LANES):
        sl = slice(c * LANES, (c + 1) * LANES)
        qsw_ref[:, sl] = (rot(q[:, sl]) * (QK_SCALE * LOG2E)).astype(BF16)

    kv = mm(OFF_QSW, OFF_VSW)
    k = rot(kv[:, :SW_KV_WIDTH])
    v = kv[:, SW_KV_WIDTH:]
    first = lax.broadcasted_iota(jnp.int32, k.shape, 1) < HEAD_DIM

    def dup(t):
        r = pltpu.roll(t, HEAD_DIM, 1)
        return jnp.concatenate([jnp.where(first, t, r), jnp.where(first, r, t)], axis=1)

    ksw_ref[...] = dup(k).astype(BF16)
    vsw_ref[...] = v.astype(BF16)

    gw = 512
    for c in range(2 * D_MODEL // gw):
        z = mm(OFF_VSW + c * gw, OFF_VSW + (c + 1) * gw) + bg_ref[:, c * gw:(c + 1) * gw]
        gate_ref[:, c * gw:(c + 1) * gw] = _sigmoid(z).astype(BF16)


def _proj(x, w_bf, b_gate, tables, embed_ln=None):
    n = x.shape[0]
    tiles_per_seq = SEQ // PROJ_ROW_TILE
    row = lambda w: pl.BlockSpec((PROJ_ROW_TILE, w), lambda i: (i, 0))
    full = lambda a: pl.BlockSpec(a.shape, lambda i: (0,) * a.ndim)
    tab = pl.BlockSpec((PROJ_ROW_TILE, LANES), lambda i: (i % tiles_per_seq, 0))
    bg = b_gate.reshape(1, -1)
    widths = (NA_WIDTH, NA_WIDTH, NA_WIDTH, SW_WIDTH, 2 * SW_KV_WIDTH, SW_KV_WIDTH, 2 * D_MODEL)
    ln = [t.reshape(1, -1) for t in embed_ln] if embed_ln else []
    out_specs = [row(w) for w in widths]
    out_shape = [jax.ShapeDtypeStruct((n, w), BF16) for w in widths]
    if embed_ln:
        out_specs = [row(D_MODEL)] + out_specs
        out_shape = [jax.ShapeDtypeStruct((n, D_MODEL), F32)] + out_shape
    outs = pl.pallas_call(
        functools.partial(_proj_kernel, embed_ln=bool(embed_ln)), grid=(n // PROJ_ROW_TILE,),
        in_specs=[row(D_MODEL)] + [full(t) for t in ln] + [full(w_bf), full(bg), tab, tab, tab],
        out_specs=out_specs, out_shape=out_shape,
        compiler_params=_params("parallel"), name="in_proj",
    )(x, *ln, w_bf, bg, *tables)
    return outs if embed_ln else [x] + list(outs)


def _na_bias_tables(rpb):
    c = np.arange(GRID_W)
    qcs = np.clip(c - NA_KW // 2, 0, GRID_W - NA_KW)
    valid = (c[None, :] >= qcs[:, None]) & (c[None, :] < qcs[:, None] + NA_KW)
    n_dr, n_dc = 2 * NA_KH - 1, 2 * NA_KW - 1
    span = 2 * GRID_W - 1
    lead = GRID_W - NA_KW
    ext = jnp.pad(rpb.astype(F32), ((0, 0), (0, 0), (lead, span + 1 - lead - n_dc)))
    flat = jnp.broadcast_to(ext[:, :, None, :], (NA_HEADS, n_dr, GRID_W, span + 1))
    flat = flat.reshape(NA_HEADS, n_dr, GRID_W * (span + 1))[:, :, :GRID_W * span]
    t = flat.reshape(NA_HEADS, n_dr, GRID_W, span)[:, :, :, GRID_W - 1:]
    t = jnp.where(valid[None, None], t * LOG2E, NEG_INF)
    pairs = jnp.concatenate([t[:, :-1], t[:, 1:]], axis=-1)
    pairs = pairs.reshape(2, 4, n_dr - 1, GRID_W, 2 * GRID_W).transpose(0, 2, 1, 3, 4)
    return pairs.reshape(2, n_dr - 1, 4 * GRID_W, 2 * GRID_W)


def _na_kernel(q_ref, k_ref, v_ref, bias_ref, o_ref):
    lane = lax.broadcasted_iota(jnp.int32, (GRID_W, MXU_DIM), 1)
    head_mask = [(lane >= h * HEAD_DIM) & (lane < (h + 1) * HEAD_DIM) for h in range(4)]
    nk = NA_KH * GRID_W

    def rows(it, carry):
        chains = []
        for u in range(NA_ROWS_PER_STEP):
            r = it * NA_ROWS_PER_STEP + u
            rs = jnp.clip(r - NA_KH // 2, 0, GRID_ROWS - NA_KH)
            variant = rs - r + (NA_KH - 1)
            qrows = pl.ds(pl.multiple_of(r * GRID_W, GRID_W), GRID_W)
            krows = pl.ds(pl.multiple_of(rs * GRID_W, GRID_W), nk)
            q = q_ref[qrows, :]
            for g in range(2):
                gs = slice(g * MXU_DIM, (g + 1) * MXU_DIM)
                qg = q[:, gs]
                lhs = jnp.concatenate([jnp.where(m, qg, jnp.zeros_like(qg)) for m in head_mask], axis=0)
                bias = jnp.concatenate([bias_ref[g, variant + 2 * a] for a in range(NA_KH // 2)], axis=1)
                chains.append(dict(lhs=lhs, k=k_ref[krows, gs], v=v_ref[krows, gs], bias=bias,
                                   out=(qrows, gs)))
        for c in chains:
            c["s"] = lax.dot_general(c["lhs"], c["k"], (((1,), (1,)), ((), ())),
                                     preferred_element_type=F32) + c["bias"]
        for c in chains:
            c["p"] = jnp.exp2(c["s"] - jnp.max(c["s"], axis=-1, keepdims=True))
        for c in chains:
            c["l"] = jnp.sum(c["p"], axis=-1, keepdims=True)
        for c in chains:
            c["o"] = jnp.dot(c["p"].astype(BF16), c["v"], preferred_element_type=F32) / c["l"]
        for c in chains:
            og = jnp.zeros((GRID_W, MXU_DIM), F32)
            for h in range(4):
                og = og + jnp.where(head_mask[h], c["o"][h * GRID_W:(h + 1) * GRID_W], 0.0)
            o_ref[c["out"]] = og.astype(BF16)
        return carry

    lax.fori_loop(0, GRID_ROWS // NA_ROWS_PER_STEP, rows, 0)


def _na_attention(q, k, v, bias):
    b = q.shape[0]
    seq = pl.BlockSpec((None, SEQ, NA_WIDTH), lambda i: (i, 0, 0))
    return pl.pallas_call(
        _na_kernel, grid=(b,),
        in_specs=[seq, seq, seq, pl.BlockSpec(bias.shape, lambda i: (0, 0, 0, 0))],
        out_specs=seq, out_shape=jax.ShapeDtypeStruct((b, SEQ, NA_WIDTH), BF16),
        compiler_params=_params("parallel"), name="na_attention",
    )(q, k, v, bias)


def _sw_masks():
    nkeys = 3 * SW_BLOCK
    jk = np.arange(nkeys)[:, None]
    iq = np.arange(SW_BLOCK)[None, :]
    band = (jk >= iq) & (jk <= iq + 2 * SW_WINDOW)
    in_seq = [(jk >= SW_BLOCK), np.ones_like(band), (jk < 2 * SW_BLOCK)]
    return jnp.asarray(np.stack([np.where(band & ok, 0.0, NEG_INF) for ok in in_seq]), F32)


def _sw_kernel(sink_ref, q_ref, k_ref, v_ref, mask_ref, o_ref):
    nb = SEQ // SW_BLOCK
    first = lax.broadcasted_iota(jnp.int32, (SW_BLOCK, LANES), 1) < HEAD_DIM
    ones = jnp.ones((8, 3 * SW_BLOCK), BF16)

    def blocks(it, carry):
        chains = []
        for u in range(SW_BLOCKS_PER_STEP):
            n = it * SW_BLOCKS_PER_STEP + u
            mask = mask_ref[jnp.where(n == 0, 0, jnp.where(n == nb - 1, 2, 1))]
            starts = [pl.multiple_of(jnp.clip(n + d, 0, nb - 1) * SW_BLOCK, SW_BLOCK) for d in (-1, 0, 1)]
            qrows = pl.ds(pl.multiple_of(n * SW_BLOCK, SW_BLOCK), SW_BLOCK)
            q = q_ref[qrows, :]
            vv = jnp.concatenate([v_ref[pl.ds(s, SW_BLOCK), :] for s in starts], axis=0)
            vt = vv.astype(F32).T.astype(BF16)
            for kvh in range(SW_KV_HEADS):
                ks = slice(kvh * LANES, (kvh + 1) * LANES)
                kk = jnp.concatenate([k_ref[pl.ds(s, SW_BLOCK), ks] for s in starts], axis=0)
                parts, sinks = [], []
                for j in range(SW_GROUP):
                    h = kvh * SW_GROUP + j
                    qc = q[:, (h // 2) * LANES:(h // 2 + 1) * LANES]
                    keep = first if h % 2 == 0 else jnp.logical_not(first)
                    parts.append(jnp.where(keep, qc, jnp.zeros_like(qc)))
                    sinks.append(jnp.full((1, SW_BLOCK), sink_ref[h] * LOG2E, F32))
                vt_ones = jnp.concatenate([vt[kvh * HEAD_DIM:(kvh + 1) * HEAD_DIM], ones], axis=0)
                chains.append(dict(k=kk, vt=vt_ones, q=jnp.concatenate(parts, axis=0),
                                   sink=jnp.concatenate(sinks, axis=1), mask=mask, qrows=qrows, kvh=kvh))
        for c in chains:
            s = lax.dot_general(c["k"], c["q"], (((1,), (1,)), ((), ())), preferred_element_type=F32)
            prev_mask, next_mask = c["mask"][:SW_BLOCK], c["mask"][2 * SW_BLOCK:]
            cols = []
            for j in range(SW_GROUP):
                sj = s[:, j * SW_BLOCK:(j + 1) * SW_BLOCK]
                cols.append(jnp.concatenate([sj[:SW_BLOCK] + prev_mask, sj[SW_BLOCK:2 * SW_BLOCK],
                                             sj[2 * SW_BLOCK:] + next_mask], axis=0))
            c["s"] = jnp.concatenate(cols, axis=1)
        for c in chains:
            c["m"] = jnp.maximum(jnp.max(c["s"], axis=0, keepdims=True), c["sink"])
        for c in chains:
            c["p"] = jnp.exp2(c["s"] - c["m"]).astype(BF16)
        for c in chains:
            o = jnp.dot(c["vt"], c["p"], preferred_element_type=F32)
            denom = o[HEAD_DIM:HEAD_DIM + 1] + jnp.exp2(c["sink"] - c["m"])
            c["o"] = o[:HEAD_DIM] / denom
        for c in chains:
            for pair in range(SW_GROUP // 2):
                even = c["o"][:, (2 * pair) * SW_BLOCK:(2 * pair + 1) * SW_BLOCK]
                odd = c["o"][:, (2 * pair + 1) * SW_BLOCK:(2 * pair + 2) * SW_BLOCK]
                col = (c["kvh"] * (SW_GROUP // 2) + pair) * LANES
                o_ref[c["qrows"], col:col + LANES] = jnp.concatenate([even, odd], axis=0).T.astype(BF16)
        return carry

    lax.fori_loop(0, nb // SW_BLOCKS_PER_STEP, blocks, 0)


def _sw_attention(q, k2, v, sink):
    b = q.shape[0]
    masks = _sw_masks()
    seq = lambda w: pl.BlockSpec((None, SEQ, w), lambda i: (i, 0, 0))
    return pl.pallas_call(
        _sw_kernel, grid=(b,),
        in_specs=[pl.BlockSpec(memory_space=pltpu.SMEM), seq(SW_WIDTH), seq(2 * SW_KV_WIDTH),
                  seq(SW_KV_WIDTH), pl.BlockSpec(masks.shape, lambda i: (0, 0, 0))],
        out_specs=seq(SW_WIDTH), out_shape=jax.ShapeDtypeStruct((b, SEQ, SW_WIDTH), BF16),
        compiler_params=_params("parallel"), name="sw_attention",
    )(sink.astype(F32), q, k2, v, masks)


def _merge_kernel(x_ref, yna_ref, ysw_ref, gate_ref, wna_ref, wsw_ref, wout_ref, g_ref, b_ref, o_ref):
    a = jnp.dot(yna_ref[...], wna_ref[...], preferred_element_type=F32)
    s = jnp.dot(ysw_ref[...], wsw_ref[...], preferred_element_type=F32)
    m = gate_ref[:, :D_MODEL].astype(F32) * a + gate_ref[:, D_MODEL:].astype(F32) * s
    z = DEEPNORM_ALPHA * x_ref[...] + jnp.dot(m.astype(BF16), wout_ref[...], preferred_element_type=F32)
    o_ref[...] = _layer_norm(z, g_ref[...], b_ref[...])


def _merge(x, y_na, y_sw, gates, w_na, w_sw, w_out, g, b):
    n = x.shape[0]
    row = lambda w: pl.BlockSpec((PROJ_ROW_TILE, w), lambda i: (i, 0))
    full = lambda a: pl.BlockSpec(a.shape, lambda i: (0,) * a.ndim)
    g2, b2 = g.reshape(1, -1), b.reshape(1, -1)
    return pl.pallas_call(
        _merge_kernel, grid=(n // PROJ_ROW_TILE,),
        in_specs=[row(D_MODEL), row(NA_WIDTH), row(SW_WIDTH), row(2 * D_MODEL),
                  full(w_na), full(w_sw), full(w_out), full(g2), full(b2)],
        out_specs=row(D_MODEL), out_shape=jax.ShapeDtypeStruct((n, D_MODEL), F32),
        compiler_params=_params("parallel"), name="merge_ln1",
    )(x, y_na, y_sw, gates, w_na, w_sw, w_out, g2, b2)


def _router_kernel(x_ref, w_ref, cw_ref, pos_ref, post_ref, cnt_ref):
    x, w = x_ref[...], w_ref[...]
    xh, wh = x.astype(BF16), w.astype(BF16)
    xl = (x - xh.astype(F32)).astype(BF16)
    wl = (w - wh.astype(F32)).astype(BF16)
    dot = functools.partial(jnp.dot, preferred_element_type=F32)
    logits = dot(xh, wh) + dot(xl, wh) + dot(xh, wl)
    lane = lax.broadcasted_iota(jnp.int32, logits.shape, 1).astype(F32)
    lg = jnp.where(lane < N_EXPERTS, logits, -jnp.inf)
    m1 = jnp.max(lg, axis=-1, keepdims=True)
    i1 = jnp.min(jnp.where(lg == m1, lane, float(LANES)), axis=-1, keepdims=True)
    lg2 = jnp.where(lane == i1, -jnp.inf, lg)
    m2 = jnp.max(lg2, axis=-1, keepdims=True)
    i2 = jnp.min(jnp.where(lg2 == m2, lane, float(LANES)), axis=-1, keepdims=True)
    e2 = jnp.exp(m2 - m1)
    denom = 1.0 + e2
    cw = jnp.where(lane == i1, 1.0 / denom, 0.0) + jnp.where(lane == i2, e2 / denom, 0.0)
    cw_ref[...] = cw

    t = cw.shape[0]
    sel = jnp.where(cw.T[:N_EXPERTS] > 0.0, 1.0, 0.0)
    tri = jnp.where(lax.broadcasted_iota(jnp.int32, (t, t), 0) <= lax.broadcasted_iota(jnp.int32, (t, t), 1),
                    1.0, 0.0).astype(BF16)
    incl = dot(sel.astype(BF16), tri)
    post = jnp.where(sel > 0.0, incl - 1.0, -1.0)
    post_ref[...] = post
    pos_ref[...] = jnp.concatenate([post, jnp.full((LANES - N_EXPERTS, t), -1.0, F32)], axis=0).T
    cnt_ref[...] = jnp.broadcast_to(incl[:, t - 1:t], (N_EXPERTS, LANES))


def _router(x, w_router):
    n = x.shape[0]
    nblk = n // ROW_TILE
    w = jnp.zeros((D_MODEL, LANES), F32).at[:, :N_EXPERTS].set(w_router)
    tok = pl.BlockSpec((ROW_TILE, LANES), lambda i: (i, 0))
    cw, pos, post, cnt = pl.pallas_call(
        _router_kernel, grid=(nblk,),
        in_specs=[pl.BlockSpec((ROW_TILE, D_MODEL), lambda i: (i, 0)),
                  pl.BlockSpec((D_MODEL, LANES), lambda i: (0, 0))],
        out_specs=[tok, tok, pl.BlockSpec((N_EXPERTS, ROW_TILE), lambda i: (0, i)),
                   pl.BlockSpec((N_EXPERTS, LANES), lambda i: (i, 0))],
        out_shape=[jax.ShapeDtypeStruct((n, LANES), F32), jax.ShapeDtypeStruct((n, LANES), F32),
                   jax.ShapeDtypeStruct((N_EXPERTS, n), F32),
                   jax.ShapeDtypeStruct((nblk * N_EXPERTS, LANES), F32)],
        compiler_params=_params("parallel"), name="router_top2",
    )(x, w)
    return cw, pos, post, cnt[:, 0].astype(jnp.int32).reshape(nblk, N_EXPERTS)


def _moe_layout(cnt, n_tiles):
    seg = (cnt + SEG_ALIGN - 1) // SEG_ALIGN * SEG_ALIGN
    rows_e = jnp.sum(seg, axis=0)
    tiles_e = (rows_e + MOE_CHUNK + MOE_ROW_TILE - 1) // MOE_ROW_TILE
    tile_end = jnp.cumsum(tiles_e)
    tile_off = tile_end - tiles_e
    seg_off = (tile_off * MOE_ROW_TILE)[None, :] + jnp.cumsum(seg, axis=0) - seg
    tile = jnp.arange(n_tiles, dtype=jnp.int32)
    tile_expert = jnp.minimum(jnp.sum(tile[:, None] >= tile_end[None, :], axis=1), N_EXPERTS - 1)
    occupied = jnp.clip(rows_e[tile_expert] - (tile - tile_off[tile_expert]) * MOE_ROW_TILE, 0, MOE_ROW_TILE)
    as_i32 = lambda t: t.astype(jnp.int32)
    return as_i32(seg_off).reshape(-1), as_i32(tile_expert), as_i32(occupied)


def _chunk_copies(off_ref, cnt_ref, block, hbm_ref, vmem_ref, sems, to_hbm):
    slot = block % 2
    out = []
    for e in range(N_EXPERTS):
        c = cnt_ref[block * N_EXPERTS + e]
        off = pl.multiple_of(off_ref[block * N_EXPERTS + e], SEG_ALIGN)
        for k in range(ROW_TILE // MOE_CHUNK):
            idx = k * N_EXPERTS + e
            rows = hbm_ref.at[pl.ds(off + k * MOE_CHUNK, MOE_CHUNK)]
            staged = vmem_ref.at[slot, idx]
            src, dst = (staged, rows) if to_hbm else (rows, staged)
            out.append((c > k * MOE_CHUNK, e, k, idx, pltpu.make_async_copy(src, dst, sems.at[slot, idx])))
    return out


def _dispatch_kernel(off_ref, cnt_ref, x_ref, post_ref, xs_in_ref, xs_ref, stage_ref, sems):
    del xs_in_ref
    b, nblk = pl.program_id(0), pl.num_programs(0)
    copies_of = lambda blk: _chunk_copies(off_ref, cnt_ref, blk, xs_ref, stage_ref, sems, to_hbm=True)

    def wait_all(blk):
        for pred, _, _, _, copy in copies_of(blk):
            pl.when(pred)(copy.wait)

    xb = x_ref[...].astype(BF16)
    slot_row = lax.broadcasted_iota(jnp.int32, (MOE_CHUNK, ROW_TILE), 0).astype(F32)
    copies = copies_of(b)
    for pred, e, k, idx, _ in copies:
        @pl.when(pred)
        def _(e=e, k=k, idx=idx):
            onehot = jnp.where(post_ref[e:e + 1, :] == slot_row + float(k * MOE_CHUNK), 1.0, 0.0).astype(BF16)
            stage_ref[b % 2, idx] = jnp.dot(onehot, xb, preferred_element_type=F32).astype(BF16)
    pl.when(b >= 1)(lambda: wait_all(b - 1))
    for pred, _, _, _, copy in copies:
        pl.when(pred)(copy.start)
    pl.when(b == nblk - 1)(lambda: wait_all(b))


def _dispatch(x, post, seg_off, cnt, n_rows):
    n = x.shape[0]
    n_chunks = N_EXPERTS * (ROW_TILE // MOE_CHUNK)
    grid_spec = pltpu.PrefetchScalarGridSpec(
        num_scalar_prefetch=2, grid=(n // ROW_TILE,),
        in_specs=[pl.BlockSpec((ROW_TILE, D_MODEL), lambda i, *_: (i, 0)),
                  pl.BlockSpec((N_EXPERTS, ROW_TILE), lambda i, *_: (0, i)),
                  pl.BlockSpec(memory_space=pl.ANY)],
        out_specs=pl.BlockSpec(memory_space=pl.ANY),
        scratch_shapes=[pltpu.VMEM((2, n_chunks, MOE_CHUNK, D_MODEL), BF16),
                        pltpu.SemaphoreType.DMA((2, n_chunks))])
    return pl.pallas_call(
        _dispatch_kernel, grid_spec=grid_spec,
        out_shape=jax.ShapeDtypeStruct((n_rows, D_MODEL), BF16),
        input_output_aliases={4: 0},
        compiler_params=_params("arbitrary"), name="moe_dispatch",
    )(seg_off, cnt.reshape(-1), x, post, jnp.zeros((n_rows, D_MODEL), BF16))


def _swiglu_partial(xb, wg, wu, wd):
    hg = jnp.dot(xb, wg, preferred_element_type=F32)
    hu = jnp.dot(xb, wu, preferred_element_type=F32)
    h = hg * _sigmoid(hg) * hu
    return jnp.dot(h.astype(BF16), wd, preferred_element_type=F32)


def _moe_ffn_kernel(expert_ref, occ_ref, x_ref, wg_ref, wu_ref, wd_ref, o_ref, acc_ref):
    del expert_ref
    i, j = pl.program_id(0), pl.program_id(1)
    occupied = occ_ref[i]

    @pl.when(j == 0)
    def _():
        acc_ref[...] = jnp.zeros_like(acc_ref)

    def weights():
        return wg_ref[...].astype(BF16), wu_ref[...].astype(BF16), wd_ref[...].astype(BF16)

    @pl.when(occupied == MOE_ROW_TILE)
    def _():
        acc_ref[...] += _swiglu_partial(x_ref[...], *weights())

    @pl.when((occupied > 0) & (occupied < MOE_ROW_TILE))
    def _():
        wg, wu, wd = weights()
        for s in range(MOE_ROW_TILE // MOE_CHUNK):
            rows = slice(s * MOE_CHUNK, (s + 1) * MOE_CHUNK)

            @pl.when(s * MOE_CHUNK < occupied)
            def _(rows=rows):
                acc_ref[rows, :] += _swiglu_partial(x_ref[rows, :], wg, wu, wd)

    @pl.when(j == pl.num_programs(1) - 1)
    def _():
        o_ref[...] = acc_ref[...].astype(BF16)


def _moe_ffn(xs, tile_expert, occupied, w_gate, w_up, w_down):
    n_rows = xs.shape[0]
    d_ff = w_gate.shape[-1]
    row = pl.BlockSpec((MOE_ROW_TILE, D_MODEL), lambda i, j, *_: (i, 0))
    up = pl.BlockSpec((None, D_MODEL, MOE_FF_TILE), lambda i, j, ex, occ: (ex[i], 0, j))
    down = pl.BlockSpec((None, MOE_FF_TILE, D_MODEL), lambda i, j, ex, occ: (ex[i], j, 0))
    grid_spec = pltpu.PrefetchScalarGridSpec(
        num_scalar_prefetch=2, grid=(n_rows // MOE_ROW_TILE, d_ff // MOE_FF_TILE),
        in_specs=[row, up, up, down], out_specs=row,
        scratch_shapes=[pltpu.VMEM((MOE_ROW_TILE, D_MODEL), F32)])
    return pl.pallas_call(
        _moe_ffn_kernel, grid_spec=grid_spec,
        out_shape=jax.ShapeDtypeStruct((n_rows, D_MODEL), BF16),
        compiler_params=_params("parallel", "arbitrary"), name="moe_swiglu",
    )(tile_expert, occupied, xs, w_gate, w_up, w_down)


def _ffn_kernel(x_ref, wg_ref, wu_ref, wd_ref, g_ref, b_ref, o_ref):
    x = x_ref[...]
    y = _swiglu_partial(x.astype(BF16), wg_ref[...], wu_ref[...], wd_ref[...])
    o_ref[...] = _layer_norm(DEEPNORM_ALPHA * x + y, g_ref[...], b_ref[...])


def _ffn(x, w_gate, w_up, w_down, g, b):
    n = x.shape[0]
    row = pl.BlockSpec((PROJ_ROW_TILE, D_MODEL), lambda i: (i, 0))
    const = lambda a: pl.BlockSpec(a.shape, lambda i: (0,) * a.ndim, pipeline_mode=pl.Buffered(1))
    g2, b2 = g.reshape(1, -1), b.reshape(1, -1)
    return pl.pallas_call(
        _ffn_kernel, grid=(n // PROJ_ROW_TILE,),
        in_specs=[row, const(w_gate), const(w_up), const(w_down), const(g2), const(b2)], out_specs=row,
        out_shape=jax.ShapeDtypeStruct((n, D_MODEL), F32),
        compiler_params=_params("parallel"), name="swiglu_ln2",
    )(x, w_gate, w_up, w_down, g2, b2)


def _combine_kernel(off_ref, cnt_ref, x_ref, cw_ref, pos_ref, g_ref, b_ref, ys_ref, o_ref,
                    buf_ref, acc_ref, sems):
    b, nblk = pl.program_id(0), pl.num_programs(0)
    copies_of = lambda blk: _chunk_copies(off_ref, cnt_ref, blk, ys_ref, buf_ref, sems, to_hbm=False)

    def start_all(blk):
        for pred, _, k, _, copy in copies_of(blk):
            if k == 0:
                copy.start()
            else:
                pl.when(pred)(copy.start)

    pl.when(b == 0)(lambda: start_all(b))
    pl.when(b + 1 < nblk)(lambda: start_all(b + 1))

    lane = lax.broadcasted_iota(jnp.int32, (ROW_TILE, LANES), 1)
    slot_col = lax.broadcasted_iota(jnp.int32, (ROW_TILE, MOE_CHUNK), 1).astype(F32)
    column = lambda ref, e: jnp.sum(jnp.where(lane == e, ref[...], 0.0), axis=-1, keepdims=True)
    weight = [column(cw_ref, e) for e in range(N_EXPERTS)]
    pos = [column(pos_ref, e) for e in range(N_EXPERTS)]

    def scatter_matrix(e, k):
        return jnp.where(pos[e] == slot_col + float(k * MOE_CHUNK), weight[e], 0.0).astype(BF16)

    first = jnp.concatenate([scatter_matrix(e, 0) for e in range(N_EXPERTS)], axis=1)
    copies = copies_of(b)
    for _, _, k, _, copy in copies:
        if k == 0:
            copy.wait()
    firsts = buf_ref[b % 2, 0:N_EXPERTS].reshape(N_EXPERTS * MOE_CHUNK, D_MODEL)
    acc_ref[...] = jnp.dot(first, firsts, preferred_element_type=F32)
    for pred, e, k, idx, copy in copies:
        if k > 0:
            @pl.when(pred)
            def _(e=e, k=k, idx=idx, copy=copy):
                copy.wait()
                acc_ref[...] += jnp.dot(scatter_matrix(e, k), buf_ref[b % 2, idx], preferred_element_type=F32)

    o_ref[...] = _layer_norm(DEEPNORM_ALPHA * x_ref[...] + acc_ref[...], g_ref[...], b_ref[...])


def _combine(x, ys, cw, pos, seg_off, cnt, g, b):
    n = x.shape[0]
    n_chunks = N_EXPERTS * (ROW_TILE // MOE_CHUNK)
    tok = lambda w: pl.BlockSpec((ROW_TILE, w), lambda i, *_: (i, 0))
    vec = pl.BlockSpec((1, D_MODEL), lambda i, *_: (0, 0))
    grid_spec = pltpu.PrefetchScalarGridSpec(
        num_scalar_prefetch=2, grid=(n // ROW_TILE,),
        in_specs=[tok(D_MODEL), tok(LANES), tok(LANES), vec, vec, pl.BlockSpec(memory_space=pl.ANY)],
        out_specs=tok(D_MODEL),
        scratch_shapes=[pltpu.VMEM((2, n_chunks, MOE_CHUNK, D_MODEL), BF16),
                        pltpu.VMEM((ROW_TILE, D_MODEL), F32),
                        pltpu.SemaphoreType.DMA((2, n_chunks))])
    return pl.pallas_call(
        _combine_kernel, grid_spec=grid_spec,
        out_shape=jax.ShapeDtypeStruct((n, D_MODEL), F32),
        compiler_params=_params("arbitrary"), name="moe_combine_ln2",
    )(seg_off, cnt.reshape(-1), x, cw, pos, g.reshape(1, -1), b.reshape(1, -1), ys)


def _moe(x, w_router, w_gate, w_up, w_down, g, b):
    n = x.shape[0]
    nblk = n // ROW_TILE
    max_rows = 2 * n + nblk * N_EXPERTS * (SEG_ALIGN - 1) + N_EXPERTS * (MOE_CHUNK + MOE_ROW_TILE - 1)
    n_tiles = max_rows // MOE_ROW_TILE
    cw, pos, post, cnt = _router(x, w_router)
    seg_off, tile_expert, occupied = _moe_layout(cnt, n_tiles)
    xs = _dispatch(x, post, seg_off, cnt, n_tiles * MOE_ROW_TILE)
    ys = _moe_ffn(xs, tile_expert, occupied, w_gate, w_up, w_down)
    return _combine(x, ys, cw, pos, seg_off, cnt, g, b)


def kernel(x, emb_ln_g, emb_ln_b, w_in, b_gate, na_rpb, sw_sink, w_branch_na, w_branch_sw, w_out,
           ln1_g, ln1_b, ffn_w_gate, ffn_w_up, ffn_w_down, moe_router, moe_w_gate, moe_w_up,
           moe_w_down, ln2_g, ln2_b):
    batch, seq, d = x.shape
    assert (seq, d) == (SEQ, D_MODEL)
    n = batch * seq
    tables = _rotary_tables()
    h = x.reshape(n, d)
    for layer in range(DEPTH):
        h, q_na, k_na, v_na, q_sw, k_sw, v_sw, gates = _proj(
            h, w_in[layer].astype(BF16), b_gate[layer], tables,
            embed_ln=(emb_ln_g, emb_ln_b) if layer == 0 else None)
        seq3 = lambda t: t.reshape(batch, seq, t.shape[-1])
        y_na = _na_attention(seq3(q_na), seq3(k_na), seq3(v_na), _na_bias_tables(na_rpb[layer]))
        y_sw = _sw_attention(seq3(q_sw), seq3(k_sw), seq3(v_sw), sw_sink[layer])
        h = _merge(h, y_na.reshape(n, -1), y_sw.reshape(n, -1), gates,
                   w_branch_na[layer].astype(BF16), w_branch_sw[layer].astype(BF16),
                   w_out[layer].astype(BF16), ln1_g[layer], ln1_b[layer])
        i = layer // 2
        if layer % 2 == 0:
            h = _ffn(h, ffn_w_gate[i].astype(BF16), ffn_w_up[i].astype(BF16),
                     ffn_w_down[i].astype(BF16), ln2_g[layer], ln2_b[layer])
        else:
            h = _moe(h, moe_router[i], moe_w_gate[i], moe_w_up[i], moe_w_down[i],
                     ln2_g[layer], ln2_b[layer])
    return h.reshape(batch, seq, d)
```

```python
import functools

import numpy as np
import jax
import jax.numpy as jnp
from jax import lax
from jax.experimental import pallas as pl
from jax.experimental.pallas import tpu as pltpu

F32 = jnp.float32
BF16 = jnp.bfloat16

D_MODEL = 1024
SEQ = 2048
DEPTH = 2
HEAD_DIM = 64
NA_HEADS = 8
NA_WIDTH = NA_HEADS * HEAD_DIM
GRID_W = 64
GRID_ROWS = SEQ // GRID_W
NA_KH = 8
NA_KW = 16
SW_HEADS = 8
SW_KV_HEADS = 2
SW_GROUP = SW_HEADS // SW_KV_HEADS
SW_WIDTH = SW_HEADS * HEAD_DIM
SW_KV_WIDTH = SW_KV_HEADS * HEAD_DIM
SW_WINDOW = 128
SW_BLOCK = 128
ROT_DIM = HEAD_DIM // 4
ROPE_THETA = 500000.0
OFF_QNA = NA_WIDTH
OFF_KNA = 2 * NA_WIDTH
OFF_VNA = 3 * NA_WIDTH
OFF_QSW = OFF_VNA + SW_WIDTH
OFF_KSW = OFF_QSW + SW_KV_WIDTH
OFF_VSW = OFF_KSW + SW_KV_WIDTH
PROJ_COLS = OFF_VSW + 2 * D_MODEL
N_EXPERTS = 8
DEEPNORM_ALPHA = (2 * DEPTH) ** 0.25
LN_EPS = 1e-5
NEG_INF = -1e30
QK_SCALE = HEAD_DIM ** -0.5
LOG2E = float(np.log2(np.e))

LANES = 128
MXU_DIM = 256
V7X_VMEM_BYTES = 64 * 1024 * 1024
VMEM_LIMIT = V7X_VMEM_BYTES * 7 // 8

PROJ_ROW_TILE = 1024
ROW_TILE = 512
MOE_ROW_TILE = 1536
MOE_FF_TILE = 512
MOE_CHUNK = 256
SEG_ALIGN = 16
NA_ROWS_PER_STEP = 8
SW_BLOCKS_PER_STEP = 8
MERGE_PARTS = 4


def _layer_norm(z, g, b):
    mu = jnp.mean(z, axis=-1, keepdims=True)
    d = z - mu
    var = jnp.mean(d * d, axis=-1, keepdims=True)
    return d * lax.rsqrt(var + LN_EPS) * g + b


def _sigmoid(z):
    return 1.0 / (1.0 + jnp.exp(-z))


def _params(*sem):
    return pltpu.CompilerParams(dimension_semantics=sem, vmem_limit_bytes=VMEM_LIMIT)


def _rotary_tables():
    half = ROT_DIM // 2
    inv_freq = 1.0 / (ROPE_THETA ** (jnp.arange(0, ROT_DIM, 2, dtype=F32) / ROT_DIM))
    ang = jnp.arange(SEQ, dtype=jnp.int32).astype(F32)[:, None] * inv_freq[None, :]
    cos, sin = jnp.cos(ang), jnp.sin(ang)
    ones = jnp.ones((SEQ, HEAD_DIM - ROT_DIM), F32)
    zeros = jnp.zeros((SEQ, HEAD_DIM - ROT_DIM), F32)
    zh = jnp.zeros((SEQ, half), F32)
    cos_h = jnp.concatenate([cos, cos, ones], axis=1)
    sa_h = jnp.concatenate([-sin, zh, zeros], axis=1)
    sb_h = jnp.concatenate([zh, sin, zeros], axis=1)
    two = lambda t: jnp.concatenate([t, t], axis=1)
    return two(cos_h), two(sa_h), two(sb_h)


def _proj_kernel(*refs, embed_ln):
    if embed_ln:
        x_ref, g_ref, b_ref, w_ref, bg_ref, cos_ref, sa_ref, sb_ref, xn_ref, *outs = refs
        x = _layer_norm(x_ref[...], g_ref[...], b_ref[...])
        xn_ref[...] = x
    else:
        x_ref, w_ref, bg_ref, cos_ref, sa_ref, sb_ref, *outs = refs
        x = x_ref[...]
    qna_ref, kna_ref, vna_ref, qsw_ref, ksw_ref, vsw_ref, gate_ref = outs
    xb = x.astype(BF16)

    def mm(lo, hi):
        return jnp.dot(xb, w_ref[:, lo:hi], preferred_element_type=F32)

    qna_ref[...] = (mm(0, OFF_QNA) * (QK_SCALE * LOG2E)).astype(BF16)
    kna_ref[...] = mm(OFF_QNA, OFF_KNA).astype(BF16)
    vna_ref[...] = mm(OFF_KNA, OFF_VNA).astype(BF16)

    cos, sa, sb = cos_ref[...], sa_ref[...], sb_ref[...]
    half = ROT_DIM // 2

    def rot(t):
        return t * cos + pltpu.roll(t, LANES - half, 1) * sa + pltpu.roll(t, half, 1) * sb

    q = mm(OFF_VNA, OFF_QSW)
    for c in range(SW_WIDTH // LANES):
        sl = slice(c * LANES, (c + 1) * LANES)
        qsw_ref[:, sl] = (rot(q[:, sl]) * (QK_SCALE * LOG2E)).astype(BF16)

    kv = mm(OFF_QSW, OFF_VSW)
    k = rot(kv[:, :SW_KV_WIDTH])
    v = kv[:, SW_KV_WIDTH:]
    first = lax.broadcasted_iota(jnp.int32, k.shape, 1) < HEAD_DIM

    def dup(t):
        r = pltpu.roll(t, HEAD_DIM, 1)
        return jnp.concatenate([jnp.where(first, t, r), jnp.where(first, r, t)], axis=1)

    ksw_ref[...] = dup(k).astype(BF16)
    vsw_ref[...] = v.astype(BF16)

    gw = 512
    for c in range(2 * D_MODEL // gw):
        z = mm(OFF_VSW + c * gw, OFF_VSW + (c + 1) * gw) + bg_ref[:, c * gw:(c + 1) * gw]
        gate_ref[:, c * gw:(c + 1) * gw] = _sigmoid(z).astype(BF16)


def _proj(x, w_bf, b_gate, tables, embed_ln=None):
    n = x.shape[0]
    tiles_per_seq = SEQ // PROJ_ROW_TILE
    row = lambda w: pl.BlockSpec((PROJ_ROW_TILE, w), lambda i: (i, 0))
    full = lambda a: pl.BlockSpec(a.shape, lambda i: (0,) * a.ndim)
    tab = pl.BlockSpec((PROJ_ROW_TILE, LANES), lambda i: (i % tiles_per_seq, 0))
    bg = b_gate.reshape(1, -1)
    widths = (NA_WIDTH, NA_WIDTH, NA_WIDTH, SW_WIDTH, 2 * SW_KV_WIDTH, SW_KV_WIDTH, 2 * D_MODEL)
    ln = [t.reshape(1, -1) for t in embed_ln] if embed_ln else []
    out_specs = [row(w) for w in widths]
    out_shape = [jax.ShapeDtypeStruct((n, w), BF16) for w in widths]
    if embed_ln:
        out_specs = [row(D_MODEL)] + out_specs
        out_shape = [jax.ShapeDtypeStruct((n, D_MODEL), F32)] + out_shape
    outs = pl.pallas_call(
        functools.partial(_proj_kernel, embed_ln=bool(embed_ln)), grid=(n // PROJ_ROW_TILE,),
        in_specs=[row(D_MODEL)] + [full(t) for t in ln] + [full(w_bf), full(bg), tab, tab, tab],
        out_specs=out_specs, out_shape=out_shape,
        compiler_params=_params("parallel"), name="in_proj",
    )(x, *ln, w_bf, bg, *tables)
    return outs if embed_ln else [x] + list(outs)


def _na_bias_tables(rpb):
    c = np.arange(GRID_W)
    qcs = np.clip(c - NA_KW // 2, 0, GRID_W - NA_KW)
    valid = (c[None, :] >= qcs[:, None]) & (c[None, :] < qcs[:, None] + NA_KW)
    n_dr, n_dc = 2 * NA_KH - 1, 2 * NA_KW - 1
    span = 2 * GRID_W - 1
    lead = GRID_W - NA_KW
    ext = jnp.pad(rpb.astype(F32), ((0, 0), (0, 0), (lead, span + 1 - lead - n_dc)))
    flat = jnp.broadcast_to(ext[:, :, None, :], (NA_HEADS, n_dr, GRID_W, span + 1))
    flat = flat.reshape(NA_HEADS, n_dr, GRID_W * (span + 1))[:, :, :GRID_W * span]
    t = flat.reshape(NA_HEADS, n_dr, GRID_W, span)[:, :, :, GRID_W - 1:]
    t = jnp.where(valid[None, None], t * LOG2E, NEG_INF)
    pairs = jnp.concatenate([t[:, :-1], t[:, 1:]], axis=-1)
    pairs = pairs.reshape(2, 4, n_dr - 1, GRID_W, 2 * GRID_W).transpose(0, 2, 1, 3, 4)
    return pairs.reshape(2, n_dr - 1, 4 * GRID_W, 2 * GRID_W)


def _na_kernel(q_ref, k_ref, v_ref, bias_ref, o_ref):
    lane = lax.broadcasted_iota(jnp.int32, (GRID_W, MXU_DIM), 1)
    head_mask = [(lane >= h * HEAD_DIM) & (lane < (h + 1) * HEAD_DIM) for h in range(4)]
    nk = NA_KH * GRID_W

    def rows(it, carry):
        chains = []
        for u in range(NA_ROWS_PER_STEP):
            r = it * NA_ROWS_PER_STEP + u
            rs = jnp.clip(r - NA_KH // 2, 0, GRID_ROWS - NA_KH)
            variant = rs - r + (NA_KH - 1)
            qrows = pl.ds(pl.multiple_of(r * GRID_W, GRID_W), GRID_W)
            krows = pl.ds(pl.multiple_of(rs * GRID_W, GRID_W), nk)
            q = q_ref[qrows, :]
            for g in range(2):
                gs = slice(g * MXU_DIM, (g + 1) * MXU_DIM)
                qg = q[:, gs]
                lhs = jnp.concatenate([jnp.where(m, qg, jnp.zeros_like(qg)) for m in head_mask], axis=0)
                bias = jnp.concatenate([bias_ref[g, variant + 2 * a] for a in range(NA_KH // 2)], axis=1)
                chains.append(dict(lhs=lhs, k=k_ref[krows, gs], v=v_ref[krows, gs], bias=bias,
                                   out=(qrows, gs)))
        for c in chains:
            c["s"] = lax.dot_general(c["lhs"], c["k"], (((1,), (1,)), ((), ())),
                                     preferred_element_type=F32) + c["bias"]
        for c in chains:
            c["p"] = jnp.exp2(c["s"] - jnp.max(c["s"], axis=-1, keepdims=True))
        for c in chains:
            c["l"] = jnp.sum(c["p"], axis=-1, keepdims=True)
        for c in chains:
            c["o"] = jnp.dot(c["p"].astype(BF16), c["v"], preferred_element_type=F32) / c["l"]
        for c in chains:
            og = jnp.zeros((GRID_W, MXU_DIM), F32)
            for h in range(4):
                og = og + jnp.where(head_mask[h], c["o"][h * GRID_W:(h + 1) * GRID_W], 0.0)
            o_ref[c["out"]] = og.astype(BF16)
        return carry

    lax.fori_loop(0, GRID_ROWS // NA_ROWS_PER_STEP, rows, 0)


def _na_attention(q, k, v, bias):
    b = q.shape[0]
    seq = pl.BlockSpec((None, SEQ, NA_WIDTH), lambda i: (i, 0, 0))
    return pl.pallas_call(
        _na_kernel, grid=(b,),
        in_specs=[seq, seq, seq, pl.BlockSpec(bias.shape, lambda i: (0, 0, 0, 0))],
        out_specs=seq, out_shape=jax.ShapeDtypeStruct((b, SEQ, NA_WIDTH), BF16),
        compiler_params=_params("parallel"), name="na_attention",
    )(q, k, v, bias)


def _sw_masks():
    nkeys = 3 * SW_BLOCK
    jk = np.arange(nkeys)[:, None]
    iq = np.arange(SW_BLOCK)[None, :]
    band = (jk >= iq) & (jk <= iq + 2 * SW_WINDOW)
    in_seq = [(jk >= SW_BLOCK), np.ones_like(band), (jk < 2 * SW_BLOCK)]
    return jnp.asarray(np.stack([np.where(band & ok, 0.0, NEG_INF) for ok in in_seq]), F32)


def _sw_kernel(sink_ref, q_ref, k_ref, v_ref, mask_ref, o_ref):
    nb = SEQ // SW_BLOCK
    first = lax.broadcasted_iota(jnp.int32, (SW_BLOCK, LANES), 1) < HEAD_DIM
    ones = jnp.ones((8, 3 * SW_BLOCK), BF16)

    def blocks(it, carry):
        chains = []
        for u in range(SW_BLOCKS_PER_STEP):
            n = it * SW_BLOCKS_PER_STEP + u
            mask = mask_ref[jnp.where(n == 0, 0, jnp.where(n == nb - 1, 2, 1))]
            starts = [pl.multiple_of(jnp.clip(n + d, 0, nb - 1) * SW_BLOCK, SW_BLOCK) for d in (-1, 0, 1)]
            qrows = pl.ds(pl.multiple_of(n * SW_BLOCK, SW_BLOCK), SW_BLOCK)
            q = q_ref[qrows, :]
            vv = jnp.concatenate([v_ref[pl.ds(s, SW_BLOCK), :] for s in starts], axis=0)
            vt = vv.astype(F32).T.astype(BF16)
            for kvh in range(SW_KV_HEADS):
                ks = slice(kvh * LANES, (kvh + 1) * LANES)
                kk = jnp.concatenate([k_ref[pl.ds(s, SW_BLOCK), ks] for s in starts], axis=0)
                parts, sinks = [], []
                for j in range(SW_GROUP):
                    h = kvh * SW_GROUP + j
                    qc = q[:, (h // 2) * LANES:(h // 2 + 1) * LANES]
                    keep = first if h % 2 == 0 else jnp.logical_not(first)
                    parts.append(jnp.where(keep, qc, jnp.zeros_like(qc)))
                    sinks.append(jnp.full((1, SW_BLOCK), sink_ref[h] * LOG2E, F32))
                vt_ones = jnp.concatenate([vt[kvh * HEAD_DIM:(kvh + 1) * HEAD_DIM], ones], axis=0)
                chains.append(dict(k=kk, vt=vt_ones, q=jnp.concatenate(parts, axis=0),
                                   sink=jnp.concatenate(sinks, axis=1), mask=mask, qrows=qrows, kvh=kvh))
        for c in chains:
            s = lax.dot_general(c["k"], c["q"], (((1,), (1,)), ((), ())), preferred_element_type=F32)
            prev_mask, next_mask = c["mask"][:SW_BLOCK], c["mask"][2 * SW_BLOCK:]
            cols = []
            for j in range(SW_GROUP):
                sj = s[:, j * SW_BLOCK:(j + 1) * SW_BLOCK]
                cols.append(jnp.concatenate([sj[:SW_BLOCK] + prev_mask, sj[SW_BLOCK:2 * SW_BLOCK],
                                             sj[2 * SW_BLOCK:] + next_mask], axis=0))
            c["s"] = jnp.concatenate(cols, axis=1)
        for c in chains:
            c["m"] = jnp.maximum(jnp.max(c["s"], axis=0, keepdims=True), c["sink"])
        for c in chains:
            c["p"] = jnp.exp2(c["s"] - c["m"]).astype(BF16)
        for c in chains:
            o = jnp.dot(c["vt"], c["p"], preferred_element_type=F32)
            denom = o[HEAD_DIM:HEAD_DIM + 1] + jnp.exp2(c["sink"] - c["m"])
            c["o"] = o[:HEAD_DIM] / denom
        for c in chains:
            for pair in range(SW_GROUP // 2):
                even = c["o"][:, (2 * pair) * SW_BLOCK:(2 * pair + 1) * SW_BLOCK]
                odd = c["o"][:, (2 * pair + 1) * SW_BLOCK:(2 * pair + 2) * SW_BLOCK]
                col = (c["kvh"] * (SW_GROUP // 2) + pair) * LANES
                o_ref[c["qrows"], col:col + LANES] = jnp.concatenate([even, odd], axis=0).T.astype(BF16)
        return carry

    lax.fori_loop(0, nb // SW_BLOCKS_PER_STEP, blocks, 0)


def _sw_attention(q, k2, v, sink):
    b = q.shape[0]
    masks = _sw_masks()
    seq = lambda w: pl.BlockSpec((None, SEQ, w), lambda i: (i, 0, 0))
    return pl.pallas_call(
        _sw_kernel, grid=(b,),
        in_specs=[pl.BlockSpec(memory_space=pltpu.SMEM), seq(SW_WIDTH), seq(2 * SW_KV_WIDTH),
                  seq(SW_KV_WIDTH), pl.BlockSpec(masks.shape, lambda i: (0, 0, 0))],
        out_specs=seq(SW_WIDTH), out_shape=jax.ShapeDtypeStruct((b, SEQ, SW_WIDTH), BF16),
        compiler_params=_params("parallel"), name="sw_attention",
    )(sink.astype(F32), q, k2, v, masks)


def _merge_kernel(x_ref, yna_ref, ysw_ref, gate_ref, wna_ref, wsw_ref, wout_ref, g_ref, b_ref, o_ref):
    part = x_ref.shape[0] // MERGE_PARTS
    parts = [slice(p * part, (p + 1) * part) for p in range(MERGE_PARTS)]
    dot = functools.partial(jnp.dot, preferred_element_type=F32)
    branches = [(dot(yna_ref[r, :], wna_ref[...]), dot(ysw_ref[r, :], wsw_ref[...])) for r in parts]
    mixed = [(gate_ref[r, :D_MODEL].astype(F32) * a + gate_ref[r, D_MODEL:].astype(F32) * s).astype(BF16)
             for r, (a, s) in zip(parts, branches)]
    summed = [DEEPNORM_ALPHA * x_ref[r, :] + dot(m, wout_ref[...]) for r, m in zip(parts, mixed)]
    for r, z in zip(parts, summed):
        o_ref[r, :] = _layer_norm(z, g_ref[...], b_ref[...])


def _merge(x, y_na, y_sw, gates, w_na, w_sw, w_out, g, b):
    n = x.shape[0]
    row = lambda w: pl.BlockSpec((PROJ_ROW_TILE, w), lambda i: (i, 0))
    full = lambda a: pl.BlockSpec(a.shape, lambda i: (0,) * a.ndim)
    g2, b2 = g.reshape(1, -1), b.reshape(1, -1)
    return pl.pallas_call(
        _merge_kernel, grid=(n // PROJ_ROW_TILE,),
        in_specs=[row(D_MODEL), row(NA_WIDTH), row(SW_WIDTH), row(2 * D_MODEL),
                  full(w_na), full(w_sw), full(w_out), full(g2), full(b2)],
        out_specs=row(D_MODEL), out_shape=jax.ShapeDtypeStruct((n, D_MODEL), F32),
        compiler_params=_params("parallel"), name="merge_ln1",
    )(x, y_na, y_sw, gates, w_na, w_sw, w_out, g2, b2)


def _router_kernel(x_ref, w_ref, cw_ref, pos_ref, post_ref, cnt_ref):
    x, w = x_ref[...], w_ref[...]
    xh, wh = x.astype(BF16), w.astype(BF16)
    xl = (x - xh.astype(F32)).astype(BF16)
    wl = (w - wh.astype(F32)).astype(BF16)
    dot = functools.partial(jnp.dot, preferred_element_type=F32)
    logits = dot(xh, wh) + dot(xl, wh) + dot(xh, wl)
    lane = lax.broadcasted_iota(jnp.int32, logits.shape, 1).astype(F32)
    lg = jnp.where(lane < N_EXPERTS, logits, -jnp.inf)
    m1 = jnp.max(lg, axis=-1, keepdims=True)
    i1 = jnp.min(jnp.where(lg == m1, lane, float(LANES)), axis=-1, keepdims=True)
    lg2 = jnp.where(lane == i1, -jnp.inf, lg)
    m2 = jnp.max(lg2, axis=-1, keepdims=True)
    i2 = jnp.min(jnp.where(lg2 == m2, lane, float(LANES)), axis=-1, keepdims=True)
    e2 = jnp.exp(m2 - m1)
    denom = 1.0 + e2
    cw = jnp.where(lane == i1, 1.0 / denom, 0.0) + jnp.where(lane == i2, e2 / denom, 0.0)
    cw_ref[...] = cw

    t = cw.shape[0]
    sel = jnp.where(cw.T[:N_EXPERTS] > 0.0, 1.0, 0.0)
    tri = jnp.where(lax.broadcasted_iota(jnp.int32, (t, t), 0) <= lax.broadcasted_iota(jnp.int32, (t, t), 1),
                    1.0, 0.0).astype(BF16)
    incl = dot(sel.astype(BF16), tri)
    post = jnp.where(sel > 0.0, incl - 1.0, -1.0)
    post_ref[...] = post
    pos_ref[...] = jnp.concatenate([post, jnp.full((LANES - N_EXPERTS, t), -1.0, F32)], axis=0).T
    cnt_ref[...] = jnp.broadcast_to(incl[:, t - 1:t], (N_EXPERTS, LANES))


def _router(x, w_router):
    n = x.shape[0]
    nblk = n // ROW_TILE
    w = jnp.zeros((D_MODEL, LANES), F32).at[:, :N_EXPERTS].set(w_router)
    tok = pl.BlockSpec((ROW_TILE, LANES), lambda i: (i, 0))
    cw, pos, post, cnt = pl.pallas_call(
        _router_kernel, grid=(nblk,),
        in_specs=[pl.BlockSpec((ROW_TILE, D_MODEL), lambda i: (i, 0)),
                  pl.BlockSpec((D_MODEL, LANES), lambda i: (0, 0))],
        out_specs=[tok, tok, pl.BlockSpec((N_EXPERTS, ROW_TILE), lambda i: (0, i)),
                   pl.BlockSpec((N_EXPERTS, LANES), lambda i: (i, 0))],
        out_shape=[jax.ShapeDtypeStruct((n, LANES), F32), jax.ShapeDtypeStruct((n, LANES), F32),
                   jax.ShapeDtypeStruct((N_EXPERTS, n), F32),
                   jax.ShapeDtypeStruct((nblk * N_EXPERTS, LANES), F32)],
        compiler_params=_params("parallel"), name="router_top2",
    )(x, w)
    return cw, pos, post, cnt[:, 0].astype(jnp.int32).reshape(nblk, N_EXPERTS)


def _moe_layout(cnt, n_tiles):
    seg = (cnt + SEG_ALIGN - 1) // SEG_ALIGN * SEG_ALIGN
    rows_e = jnp.sum(seg, axis=0)
    tiles_e = (rows_e + MOE_CHUNK + MOE_ROW_TILE - 1) // MOE_ROW_TILE
    tile_end = jnp.cumsum(tiles_e)
    tile_off = tile_end - tiles_e
    seg_off = (tile_off * MOE_ROW_TILE)[None, :] + jnp.cumsum(seg, axis=0) - seg
    tile = jnp.arange(n_tiles, dtype=jnp.int32)
    tile_expert = jnp.minimum(jnp.sum(tile[:, None] >= tile_end[None, :], axis=1), N_EXPERTS - 1)
    occupied = jnp.clip(rows_e[tile_expert] - (tile - tile_off[tile_expert]) * MOE_ROW_TILE, 0, MOE_ROW_TILE)
    as_i32 = lambda t: t.astype(jnp.int32)
    return as_i32(seg_off).reshape(-1), as_i32(tile_expert), as_i32(occupied)


def _chunk_copies(off_ref, cnt_ref, block, hbm_ref, vmem_ref, sems, to_hbm):
    slot = block % 2
    out = []
    for e in range(N_EXPERTS):
        c = cnt_ref[block * N_EXPERTS + e]
        off = pl.multiple_of(off_ref[block * N_EXPERTS + e], SEG_ALIGN)
        for k in range(ROW_TILE // MOE_CHUNK):
            idx = k * N_EXPERTS + e
            rows = hbm_ref.at[pl.ds(off + k * MOE_CHUNK, MOE_CHUNK)]
            staged = vmem_ref.at[slot, idx]
            src, dst = (staged, rows) if to_hbm else (rows, staged)
            out.append((c > k * MOE_CHUNK, e, k, idx, pltpu.make_async_copy(src, dst, sems.at[slot, idx])))
    return out


def _dispatch_kernel(off_ref, cnt_ref, x_ref, post_ref, xs_in_ref, xs_ref, stage_ref, sems):
    del xs_in_ref
    b, nblk = pl.program_id(0), pl.num_programs(0)
    copies_of = lambda blk: _chunk_copies(off_ref, cnt_ref, blk, xs_ref, stage_ref, sems, to_hbm=True)

    def wait_all(blk):
        for pred, _, _, _, copy in copies_of(blk):
            pl.when(pred)(copy.wait)

    xb = x_ref[...].astype(BF16)
    slot_row = lax.broadcasted_iota(jnp.int32, (MOE_CHUNK, ROW_TILE), 0).astype(F32)

    def gather_matrix(e, k):
        return jnp.where(post_ref[e:e + 1, :] == slot_row + float(k * MOE_CHUNK), 1.0, 0.0).astype(BF16)

    firsts = jnp.concatenate([gather_matrix(e, 0) for e in range(N_EXPERTS)], axis=0)
    stage_ref[b % 2, 0:N_EXPERTS] = jnp.dot(firsts, xb, preferred_element_type=F32).astype(BF16).reshape(
        N_EXPERTS, MOE_CHUNK, D_MODEL)
    copies = copies_of(b)
    for pred, e, k, idx, _ in copies:
        if k > 0:
            @pl.when(pred)
            def _(e=e, k=k, idx=idx):
                stage_ref[b % 2, idx] = jnp.dot(gather_matrix(e, k), xb, preferred_element_type=F32).astype(BF16)
    pl.when(b >= 1)(lambda: wait_all(b - 1))
    for pred, _, _, _, copy in copies:
        pl.when(pred)(copy.start)
    pl.when(b == nblk - 1)(lambda: wait_all(b))


def _dispatch(x, post, seg_off, cnt, n_rows):
    n = x.shape[0]
    n_chunks = N_EXPERTS * (ROW_TILE // MOE_CHUNK)
    grid_spec = pltpu.PrefetchScalarGridSpec(
        num_scalar_prefetch=2, grid=(n // ROW_TILE,),
        in_specs=[pl.BlockSpec((ROW_TILE, D_MODEL), lambda i, *_: (i, 0)),
                  pl.BlockSpec((N_EXPERTS, ROW_TILE), lambda i, *_: (0, i)),
                  pl.BlockSpec(memory_space=pl.ANY)],
        out_specs=pl.BlockSpec(memory_space=pl.ANY),
        scratch_shapes=[pltpu.VMEM((2, n_chunks, MOE_CHUNK, D_MODEL), BF16),
                        pltpu.SemaphoreType.DMA((2, n_chunks))])
    return pl.pallas_call(
        _dispatch_kernel, grid_spec=grid_spec,
        out_shape=jax.ShapeDtypeStruct((n_rows, D_MODEL), BF16),
        input_output_aliases={4: 0},
        compiler_params=_params("arbitrary"), name="moe_dispatch",
    )(seg_off, cnt.reshape(-1), x, post, jnp.zeros((n_rows, D_MODEL), BF16))


def _swiglu_partial(xb, wg, wu, wd):
    hg = jnp.dot(xb, wg, preferred_element_type=F32)
    hu = jnp.dot(xb, wu, preferred_element_type=F32)
    h = hg * _sigmoid(hg) * hu
    return jnp.dot(h.astype(BF16), wd, preferred_element_type=F32)


def _moe_ffn_kernel(expert_ref, occ_ref, x_ref, wg_ref, wu_ref, wd_ref, o_ref, acc_ref):
    del expert_ref
    i, j = pl.program_id(0), pl.program_id(1)
    occupied = occ_ref[i]

    @pl.when(j == 0)
    def _():
        acc_ref[...] = jnp.zeros_like(acc_ref)

    def weights():
        return wg_ref[...].astype(BF16), wu_ref[...].astype(BF16), wd_ref[...].astype(BF16)

    @pl.when(occupied == MOE_ROW_TILE)
    def _():
        acc_ref[...] += _swiglu_partial(x_ref[...], *weights())

    @pl.when((occupied > 0) & (occupied < MOE_ROW_TILE))
    def _():
        wg, wu, wd = weights()
        for s in range(MOE_ROW_TILE // MOE_CHUNK):
            rows = slice(s * MOE_CHUNK, (s + 1) * MOE_CHUNK)

            @pl.when(s * MOE_CHUNK < occupied)
            def _(rows=rows):
                acc_ref[rows, :] += _swiglu_partial(x_ref[rows, :], wg, wu, wd)

    @pl.when(j == pl.num_programs(1) - 1)
    def _():
        o_ref[...] = acc_ref[...].astype(BF16)


def _moe_ffn(xs, tile_expert, occupied, w_gate, w_up, w_down):
    n_rows = xs.shape[0]
    d_ff = w_gate.shape[-1]
    row = pl.BlockSpec((MOE_ROW_TILE, D_MODEL), lambda i, j, *_: (i, 0))
    up = pl.BlockSpec((None, D_MODEL, MOE_FF_TILE), lambda i, j, ex, occ: (ex[i], 0, j))
    down = pl.BlockSpec((None, MOE_FF_TILE, D_MODEL), lambda i, j, ex, occ: (ex[i], j, 0))
    grid_spec = pltpu.PrefetchScalarGridSpec(
        num_scalar_prefetch=2, grid=(n_rows // MOE_ROW_TILE, d_ff // MOE_FF_TILE),
        in_specs=[row, up, up, down], out_specs=row,
        scratch_shapes=[pltpu.VMEM((MOE_ROW_TILE, D_MODEL), F32)])
    return pl.pallas_call(
        _moe_ffn_kernel, grid_spec=grid_spec,
        out_shape=jax.ShapeDtypeStruct((n_rows, D_MODEL), BF16),
        compiler_params=_params("parallel", "arbitrary"), name="moe_swiglu",
    )(tile_expert, occupied, xs, w_gate, w_up, w_down)


def _ffn_kernel(x_ref, wg_ref, wu_ref, wd_ref, g_ref, b_ref, o_ref):
    x = x_ref[...]
    y = _swiglu_partial(x.astype(BF16), wg_ref[...], wu_ref[...], wd_ref[...])
    o_ref[...] = _layer_norm(DEEPNORM_ALPHA * x + y, g_ref[...], b_ref[...])


def _ffn(x, w_gate, w_up, w_down, g, b):
    n = x.shape[0]
    row = pl.BlockSpec((PROJ_ROW_TILE, D_MODEL), lambda i: (i, 0))
    const = lambda a: pl.BlockSpec(a.shape, lambda i: (0,) * a.ndim, pipeline_mode=pl.Buffered(1))
    g2, b2 = g.reshape(1, -1), b.reshape(1, -1)
    return pl.pallas_call(
        _ffn_kernel, grid=(n // PROJ_ROW_TILE,),
        in_specs=[row, const(w_gate), const(w_up), const(w_down), const(g2), const(b2)], out_specs=row,
        out_shape=jax.ShapeDtypeStruct((n, D_MODEL), F32),
        compiler_params=_params("parallel"), name="swiglu_ln2",
    )(x, w_gate, w_up, w_down, g2, b2)


def _combine_kernel(off_ref, cnt_ref, x_ref, cw_ref, pos_ref, g_ref, b_ref, ys_ref, o_ref,
                    buf_ref, acc_ref, sems):
    b, nblk = pl.program_id(0), pl.num_programs(0)
    copies_of = lambda blk: _chunk_copies(off_ref, cnt_ref, blk, ys_ref, buf_ref, sems, to_hbm=False)

    def start_all(blk):
        for pred, _, k, _, copy in copies_of(blk):
            if k == 0:
                copy.start()
            else:
                pl.when(pred)(copy.start)

    pl.when(b == 0)(lambda: start_all(b))
    pl.when(b + 1 < nblk)(lambda: start_all(b + 1))

    lane = lax.broadcasted_iota(jnp.int32, (ROW_TILE, LANES), 1)
    slot_col = lax.broadcasted_iota(jnp.int32, (ROW_TILE, MOE_CHUNK), 1).astype(F32)
    column = lambda ref, e: jnp.sum(jnp.where(lane == e, ref[...], 0.0), axis=-1, keepdims=True)
    weight = [column(cw_ref, e) for e in range(N_EXPERTS)]
    pos = [column(pos_ref, e) for e in range(N_EXPERTS)]

    def scatter_matrix(e, k):
        return jnp.where(pos[e] == slot_col + float(k * MOE_CHUNK), weight[e], 0.0).astype(BF16)

    first = jnp.concatenate([scatter_matrix(e, 0) for e in range(N_EXPERTS)], axis=1)
    copies = copies_of(b)
    for _, _, k, _, copy in copies:
        if k == 0:
            copy.wait()
    firsts = buf_ref[b % 2, 0:N_EXPERTS].reshape(N_EXPERTS * MOE_CHUNK, D_MODEL)
    acc_ref[...] = jnp.dot(first, firsts, preferred_element_type=F32)
    for pred, e, k, idx, copy in copies:
        if k > 0:
            @pl.when(pred)
            def _(e=e, k=k, idx=idx, copy=copy):
                copy.wait()
                acc_ref[...] += jnp.dot(scatter_matrix(e, k), buf_ref[b % 2, idx], preferred_element_type=F32)

    o_ref[...] = _layer_norm(DEEPNORM_ALPHA * x_ref[...] + acc_ref[...], g_ref[...], b_ref[...])


def _combine(x, ys, cw, pos, seg_off, cnt, g, b):
    n = x.shape[0]
    n_chunks = N_EXPERTS * (ROW_TILE // MOE_CHUNK)
    tok = lambda w: pl.BlockSpec((ROW_TILE, w), lambda i, *_: (i, 0))
    vec = pl.BlockSpec((1, D_MODEL), lambda i, *_: (0, 0))
    grid_spec = pltpu.PrefetchScalarGridSpec(
        num_scalar_prefetch=2, grid=(n // ROW_TILE,),
        in_specs=[tok(D_MODEL), tok(LANES), tok(LANES), vec, vec, pl.BlockSpec(memory_space=pl.ANY)],
        out_specs=tok(D_MODEL),
        scratch_shapes=[pltpu.VMEM((2, n_chunks, MOE_CHUNK, D_MODEL), BF16),
                        pltpu.VMEM((ROW_TILE, D_MODEL), F32),
                        pltpu.SemaphoreType.DMA((2, n_chunks))])
    return pl.pallas_call(
        _combine_kernel, grid_spec=grid_spec,
        out_shape=jax.ShapeDtypeStruct((n, D_MODEL), F32),
        compiler_params=_params("arbitrary"), name="moe_combine_ln2",
    )(seg_off, cnt.reshape(-1), x, cw, pos, g.reshape(1, -1), b.reshape(1, -1), ys)


def _moe(x, w_router, w_gate, w_up, w_down, g, b):
    n = x.shape[0]
    nblk = n // ROW_TILE
    max_rows = 2 * n + nblk * N_EXPERTS * (SEG_ALIGN - 1) + N_EXPERTS * (MOE_CHUNK + MOE_ROW_TILE - 1)
    n_tiles = max_rows // MOE_ROW_TILE
    cw, pos, post, cnt = _router(x, w_router)
    seg_off, tile_expert, occupied = _moe_layout(cnt, n_tiles)
    xs = _dispatch(x, post, seg_off, cnt, n_tiles * MOE_ROW_TILE)
    ys = _moe_ffn(xs, tile_expert, occupied, w_gate, w_up, w_down)
    return _combine(x, ys, cw, pos, seg_off, cnt, g, b)


def kernel(x, emb_ln_g, emb_ln_b, w_in, b_gate, na_rpb, sw_sink, w_branch_na, w_branch_sw, w_out,
           ln1_g, ln1_b, ffn_w_gate, ffn_w_up, ffn_w_down, moe_router, moe_w_gate, moe_w_up,
           moe_w_down, ln2_g, ln2_b):
    batch, seq, d = x.shape
    assert (seq, d) == (SEQ, D_MODEL)
    n = batch * seq
    tables = _rotary_tables()
    h = x.reshape(n, d)
    for layer in range(DEPTH):
        h, q_na, k_na, v_na, q_sw, k_sw, v_sw, gates = _proj(
            h, w_in[layer].astype(BF16), b_gate[layer], tables,
            embed_ln=(emb_ln_g, emb_ln_b) if layer == 0 else None)
        seq3 = lambda t: t.reshape(batch, seq, t.shape[-1])
        y_na = _na_attention(seq3(q_na), seq3(k_na), seq3(v_na), _na_bias_tables(na_rpb[layer]))
        y_sw = _sw_attention(seq3(q_sw), seq3(k_sw), seq3(v_sw), sw_sink[layer])
        h = _merge(h, y_na.reshape(n, -1), y_sw.reshape(n, -1), gates,
                   w_branch_na[layer].astype(BF16), w_branch_sw[layer].astype(BF16),
                   w_out[layer].astype(BF16), ln1_g[layer], ln1_b[layer])
        i = layer // 2
        if layer % 2 == 0:
            h = _ffn(h, ffn_w_gate[i].astype(BF16), ffn_w_up[i].astype(BF16),
                     ffn_w_down[i].astype(BF16), ln2_g[layer], ln2_b[layer])
        else:
            h = _moe(h, moe_router[i], moe_w_gate[i], moe_w_up[i], moe_w_down[i],
                     ln2_g[layer], ln2_b[layer])
    return h.reshape(batch, seq, d)
```

```python
import functools

import numpy as np
import jax
import jax.numpy as jnp
from jax import lax
from jax.experimental import pallas as pl
from jax.experimental.pallas import tpu as pltpu

F32 = jnp.float32
BF16 = jnp.bfloat16

D_MODEL = 1024
SEQ = 2048
DEPTH = 2
HEAD_DIM = 64
NA_HEADS = 8
NA_WIDTH = NA_HEADS * HEAD_DIM
GRID_W = 64
GRID_ROWS = SEQ // GRID_W
NA_KH = 8
NA_KW = 16
SW_HEADS = 8
SW_KV_HEADS = 2
SW_GROUP = SW_HEADS // SW_KV_HEADS
SW_WIDTH = SW_HEADS * HEAD_DIM
SW_KV_WIDTH = SW_KV_HEADS * HEAD_DIM
SW_WINDOW = 128
SW_BLOCK = 128
ROT_DIM = HEAD_DIM // 4
ROPE_THETA = 500000.0
OFF_QNA = NA_WIDTH
OFF_KNA = 2 * NA_WIDTH
OFF_VNA = 3 * NA_WIDTH
OFF_QSW = OFF_VNA + SW_WIDTH
OFF_KSW = OFF_QSW + SW_KV_WIDTH
OFF_VSW = OFF_KSW + SW_KV_WIDTH
PROJ_COLS = OFF_VSW + 2 * D_MODEL
N_EXPERTS = 8
DEEPNORM_ALPHA = (2 * DEPTH) ** 0.25
LN_EPS = 1e-5
NEG_INF = -1e30
QK_SCALE = HEAD_DIM ** -0.5
LOG2E = float(np.log2(np.e))

LANES = 128
MXU_DIM = 256
V7X_VMEM_BYTES = 64 * 1024 * 1024
VMEM_LIMIT = V7X_VMEM_BYTES * 7 // 8

PROJ_ROW_TILE = 1024
ROW_TILE = 512
MOE_ROW_TILE = 1536
MOE_FF_TILE = 512
MOE_CHUNK = 256
SEG_ALIGN = 16
NA_ROWS_PER_STEP = 8
SW_BLOCKS_PER_STEP = 8
MERGE_PARTS = 4
ROUTER_BLOCKS = 4


def _layer_norm(z, g, b):
    mu = jnp.mean(z, axis=-1, keepdims=True)
    d = z - mu
    var = jnp.mean(d * d, axis=-1, keepdims=True)
    return d * lax.rsqrt(var + LN_EPS) * g + b


def _sigmoid(z):
    return 1.0 / (1.0 + jnp.exp(-z))


def _params(*sem):
    return pltpu.CompilerParams(dimension_semantics=sem, vmem_limit_bytes=VMEM_LIMIT)


def _rotary_tables():
    half = ROT_DIM // 2
    inv_freq = 1.0 / (ROPE_THETA ** (jnp.arange(0, ROT_DIM, 2, dtype=F32) / ROT_DIM))
    ang = jnp.arange(SEQ, dtype=jnp.int32).astype(F32)[:, None] * inv_freq[None, :]
    cos, sin = jnp.cos(ang), jnp.sin(ang)
    ones = jnp.ones((SEQ, HEAD_DIM - ROT_DIM), F32)
    zeros = jnp.zeros((SEQ, HEAD_DIM - ROT_DIM), F32)
    zh = jnp.zeros((SEQ, half), F32)
    cos_h = jnp.concatenate([cos, cos, ones], axis=1)
    sa_h = jnp.concatenate([-sin, zh, zeros], axis=1)
    sb_h = jnp.concatenate([zh, sin, zeros], axis=1)
    two = lambda t: jnp.concatenate([t, t], axis=1)
    return two(cos_h), two(sa_h), two(sb_h)


def _proj_kernel(*refs, embed_ln):
    if embed_ln:
        x_ref, g_ref, b_ref, w_ref, bg_ref, cos_ref, sa_ref, sb_ref, xn_ref, *outs = refs
        x = _layer_norm(x_ref[...], g_ref[...], b_ref[...])
        xn_ref[...] = x
    else:
        x_ref, w_ref, bg_ref, cos_ref, sa_ref, sb_ref, *outs = refs
        x = x_ref[...]
    qna_ref, kna_ref, vna_ref, qsw_ref, ksw_ref, vsw_ref, gate_ref = outs
    xb = x.astype(BF16)

    def mm(lo, hi):
        return jnp.dot(xb, w_ref[:, lo:hi], preferred_element_type=F32)

    qna_ref[...] = (mm(0, OFF_QNA) * (QK_SCALE * LOG2E)).astype(BF16)
    kna_ref[...] = mm(OFF_QNA, OFF_KNA).astype(BF16)
    vna_ref[...] = mm(OFF_KNA, OFF_VNA).astype(BF16)

    cos, sa, sb = cos_ref[...], sa_ref[...], sb_ref[...]
    half = ROT_DIM // 2

    def rot(t):
        return t * cos + pltpu.roll(t, LANES - half, 1) * sa + pltpu.roll(t, half, 1) * sb

    q = mm(OFF_VNA, OFF_QSW)
    for c in range(SW_WIDTH // LANES):
        sl = slice(c * LANES, (c + 1) * LANES)
        qsw_ref[:, sl] = (rot(q[:, sl]) * (QK_SCALE * LOG2E)).astype(BF16)

    kv = mm(OFF_QSW, OFF_VSW)
    k = rot(kv[:, :SW_KV_WIDTH])
    v = kv[:, SW_KV_WIDTH:]
    first = lax.broadcasted_iota(jnp.int32, k.shape, 1) < HEAD_DIM

    def dup(t):
        r = pltpu.roll(t, HEAD_DIM, 1)
        return jnp.concatenate([jnp.where(first, t, r), jnp.where(first, r, t)], axis=1)

    ksw_ref[...] = dup(k).astype(BF16)
    vsw_ref[...] = v.astype(BF16)

    gw = 512
    for c in range(2 * D_MODEL // gw):
        z = mm(OFF_VSW + c * gw, OFF_VSW + (c + 1) * gw) + bg_ref[:, c * gw:(c + 1) * gw]
        gate_ref[:, c * gw:(c + 1) * gw] = _sigmoid(z).astype(BF16)


def _proj(x, w_bf, b_gate, tables, embed_ln=None):
    n = x.shape[0]
    tiles_per_seq = SEQ // PROJ_ROW_TILE
    row = lambda w: pl.BlockSpec((PROJ_ROW_TILE, w), lambda i: (i, 0))
    full = lambda a: pl.BlockSpec(a.shape, lambda i: (0,) * a.ndim)
    tab = pl.BlockSpec((PROJ_ROW_TILE, LANES), lambda i: (i % tiles_per_seq, 0))
    bg = b_gate.reshape(1, -1)
    widths = (NA_WIDTH, NA_WIDTH, NA_WIDTH, SW_WIDTH, 2 * SW_KV_WIDTH, SW_KV_WIDTH, 2 * D_MODEL)
    ln = [t.reshape(1, -1) for t in embed_ln] if embed_ln else []
    out_specs = [row(w) for w in widths]
    out_shape = [jax.ShapeDtypeStruct((n, w), BF16) for w in widths]
    if embed_ln:
        out_specs = [row(D_MODEL)] + out_specs
        out_shape = [jax.ShapeDtypeStruct((n, D_MODEL), F32)] + out_shape
    outs = pl.pallas_call(
        functools.partial(_proj_kernel, embed_ln=bool(embed_ln)), grid=(n // PROJ_ROW_TILE,),
        in_specs=[row(D_MODEL)] + [full(t) for t in ln] + [full(w_bf), full(bg), tab, tab, tab],
        out_specs=out_specs, out_shape=out_shape,
        compiler_params=_params("parallel"), name="in_proj",
    )(x, *ln, w_bf, bg, *tables)
    return outs if embed_ln else [x] + list(outs)


def _na_bias_tables(rpb):
    c = np.arange(GRID_W)
    qcs = np.clip(c - NA_KW // 2, 0, GRID_W - NA_KW)
    valid = (c[None, :] >= qcs[:, None]) & (c[None, :] < qcs[:, None] + NA_KW)
    n_dr, n_dc = 2 * NA_KH - 1, 2 * NA_KW - 1
    span = 2 * GRID_W - 1
    lead = GRID_W - NA_KW
    ext = jnp.pad(rpb.astype(F32), ((0, 0), (0, 0), (lead, span + 1 - lead - n_dc)))
    flat = jnp.broadcast_to(ext[:, :, None, :], (NA_HEADS, n_dr, GRID_W, span + 1))
    flat = flat.reshape(NA_HEADS, n_dr, GRID_W * (span + 1))[:, :, :GRID_W * span]
    t = flat.reshape(NA_HEADS, n_dr, GRID_W, span)[:, :, :, GRID_W - 1:]
    t = jnp.where(valid[None, None], t * LOG2E, NEG_INF)
    pairs = jnp.concatenate([t[:, :-1], t[:, 1:]], axis=-1)
    pairs = pairs.reshape(2, 4, n_dr - 1, GRID_W, 2 * GRID_W).transpose(0, 2, 1, 3, 4)
    return pairs.reshape(2, n_dr - 1, 4 * GRID_W, 2 * GRID_W)


def _na_kernel(q_ref, k_ref, v_ref, bias_ref, o_ref):
    lane = lax.broadcasted_iota(jnp.int32, (GRID_W, MXU_DIM), 1)
    head_mask = [(lane >= h * HEAD_DIM) & (lane < (h + 1) * HEAD_DIM) for h in range(4)]
    nk = NA_KH * GRID_W

    def rows(it, carry):
        chains = []
        for u in range(NA_ROWS_PER_STEP):
            r = it * NA_ROWS_PER_STEP + u
            rs = jnp.clip(r - NA_KH // 2, 0, GRID_ROWS - NA_KH)
            variant = rs - r + (NA_KH - 1)
            qrows = pl.ds(pl.multiple_of(r * GRID_W, GRID_W), GRID_W)
            krows = pl.ds(pl.multiple_of(rs * GRID_W, GRID_W), nk)
            q = q_ref[qrows, :]
            for g in range(2):
                gs = slice(g * MXU_DIM, (g + 1) * MXU_DIM)
                qg = q[:, gs]
                lhs = jnp.concatenate([jnp.where(m, qg, jnp.zeros_like(qg)) for m in head_mask], axis=0)
                bias = jnp.concatenate([bias_ref[g, variant + 2 * a] for a in range(NA_KH // 2)], axis=1)
                chains.append(dict(lhs=lhs, k=k_ref[krows, gs], v=v_ref[krows, gs], bias=bias,
                                   out=(qrows, gs)))
        for c in chains:
            c["s"] = lax.dot_general(c["lhs"], c["k"], (((1,), (1,)), ((), ())),
                                     preferred_element_type=F32) + c["bias"]
        for c in chains:
            c["p"] = jnp.exp2(c["s"] - jnp.max(c["s"], axis=-1, keepdims=True))
        for c in chains:
            c["l"] = jnp.sum(c["p"], axis=-1, keepdims=True)
        for c in chains:
            c["o"] = jnp.dot(c["p"].astype(BF16), c["v"], preferred_element_type=F32) / c["l"]
        for c in chains:
            og = jnp.zeros((GRID_W, MXU_DIM), F32)
            for h in range(4):
                og = og + jnp.where(head_mask[h], c["o"][h * GRID_W:(h + 1) * GRID_W], 0.0)
            o_ref[c["out"]] = og.astype(BF16)
        return carry

    lax.fori_loop(0, GRID_ROWS // NA_ROWS_PER_STEP, rows, 0)


def _na_attention(q, k, v, bias):
    b = q.shape[0]
    seq = pl.BlockSpec((None, SEQ, NA_WIDTH), lambda i: (i, 0, 0))
    return pl.pallas_call(
        _na_kernel, grid=(b,),
        in_specs=[seq, seq, seq, pl.BlockSpec(bias.shape, lambda i: (0, 0, 0, 0))],
        out_specs=seq, out_shape=jax.ShapeDtypeStruct((b, SEQ, NA_WIDTH), BF16),
        compiler_params=_params("parallel"), name="na_attention",
    )(q, k, v, bias)


def _sw_masks():
    nkeys = 3 * SW_BLOCK
    jk = np.arange(nkeys)[:, None]
    iq = np.arange(SW_BLOCK)[None, :]
    band = (jk >= iq) & (jk <= iq + 2 * SW_WINDOW)
    in_seq = [(jk >= SW_BLOCK), np.ones_like(band), (jk < 2 * SW_BLOCK)]
    return jnp.asarray(np.stack([np.where(band & ok, 0.0, NEG_INF) for ok in in_seq]), F32)


def _sw_kernel(sink_ref, q_ref, k_ref, v_ref, mask_ref, o_ref):
    nb = SEQ // SW_BLOCK
    first = lax.broadcasted_iota(jnp.int32, (SW_BLOCK, LANES), 1) < HEAD_DIM
    ones = jnp.ones((8, 3 * SW_BLOCK), BF16)

    def blocks(it, carry):
        chains = []
        for u in range(SW_BLOCKS_PER_STEP):
            n = it * SW_BLOCKS_PER_STEP + u
            mask = mask_ref[jnp.where(n == 0, 0, jnp.where(n == nb - 1, 2, 1))]
            starts = [pl.multiple_of(jnp.clip(n + d, 0, nb - 1) * SW_BLOCK, SW_BLOCK) for d in (-1, 0, 1)]
            qrows = pl.ds(pl.multiple_of(n * SW_BLOCK, SW_BLOCK), SW_BLOCK)
            q = q_ref[qrows, :]
            vv = jnp.concatenate([v_ref[pl.ds(s, SW_BLOCK), :] for s in starts], axis=0)
            vt = vv.astype(F32).T.astype(BF16)
            for kvh in range(SW_KV_HEADS):
                ks = slice(kvh * LANES, (kvh + 1) * LANES)
                kk = jnp.concatenate([k_ref[pl.ds(s, SW_BLOCK), ks] for s in starts], axis=0)
                parts, sinks = [], []
                for j in range(SW_GROUP):
                    h = kvh * SW_GROUP + j
                    qc = q[:, (h // 2) * LANES:(h // 2 + 1) * LANES]
                    keep = first if h % 2 == 0 else jnp.logical_not(first)
                    parts.append(jnp.where(keep, qc, jnp.zeros_like(qc)))
                    sinks.append(jnp.full((1, SW_BLOCK), sink_ref[h] * LOG2E, F32))
                vt_ones = jnp.concatenate([vt[kvh * HEAD_DIM:(kvh + 1) * HEAD_DIM], ones], axis=0)
                chains.append(dict(k=kk, vt=vt_ones, q=jnp.concatenate(parts, axis=0),
                                   sink=jnp.concatenate(sinks, axis=1), mask=mask, qrows=qrows, kvh=kvh))
        for c in chains:
            s = lax.dot_general(c["k"], c["q"], (((1,), (1,)), ((), ())), preferred_element_type=F32)
            prev_mask, next_mask = c["mask"][:SW_BLOCK], c["mask"][2 * SW_BLOCK:]
            cols = []
            for j in range(SW_GROUP):
                sj = s[:, j * SW_BLOCK:(j + 1) * SW_BLOCK]
                cols.append(jnp.concatenate([sj[:SW_BLOCK] + prev_mask, sj[SW_BLOCK:2 * SW_BLOCK],
                                             sj[2 * SW_BLOCK:] + next_mask], axis=0))
            c["s"] = jnp.concatenate(cols, axis=1)
        for c in chains:
            c["m"] = jnp.maximum(jnp.max(c["s"], axis=0, keepdims=True), c["sink"])
        for c in chains:
            c["p"] = jnp.exp2(c["s"] - c["m"]).astype(BF16)
        for c in chains:
            o = jnp.dot(c["vt"], c["p"], preferred_element_type=F32)
            denom = o[HEAD_DIM:HEAD_DIM + 1] + jnp.exp2(c["sink"] - c["m"])
            c["o"] = o[:HEAD_DIM] / denom
        for c in chains:
            for pair in range(SW_GROUP // 2):
                even = c["o"][:, (2 * pair) * SW_BLOCK:(2 * pair + 1) * SW_BLOCK]
                odd = c["o"][:, (2 * pair + 1) * SW_BLOCK:(2 * pair + 2) * SW_BLOCK]
                col = (c["kvh"] * (SW_GROUP // 2) + pair) * LANES
                o_ref[c["qrows"], col:col + LANES] = jnp.concatenate([even, odd], axis=0).T.astype(BF16)
        return carry

    lax.fori_loop(0, nb // SW_BLOCKS_PER_STEP, blocks, 0)


def _sw_attention(q, k2, v, sink):
    b = q.shape[0]
    masks = _sw_masks()
    seq = lambda w: pl.BlockSpec((None, SEQ, w), lambda i: (i, 0, 0))
    return pl.pallas_call(
        _sw_kernel, grid=(b,),
        in_specs=[pl.BlockSpec(memory_space=pltpu.SMEM), seq(SW_WIDTH), seq(2 * SW_KV_WIDTH),
                  seq(SW_KV_WIDTH), pl.BlockSpec(masks.shape, lambda i: (0, 0, 0))],
        out_specs=seq(SW_WIDTH), out_shape=jax.ShapeDtypeStruct((b, SEQ, SW_WIDTH), BF16),
        compiler_params=_params("parallel"), name="sw_attention",
    )(sink.astype(F32), q, k2, v, masks)


def _merge_kernel(x_ref, yna_ref, ysw_ref, gate_ref, wna_ref, wsw_ref, wout_ref, g_ref, b_ref, o_ref):
    part = x_ref.shape[0] // MERGE_PARTS
    parts = [slice(p * part, (p + 1) * part) for p in range(MERGE_PARTS)]
    dot = functools.partial(jnp.dot, preferred_element_type=F32)
    branches = [(dot(yna_ref[r, :], wna_ref[...]), dot(ysw_ref[r, :], wsw_ref[...])) for r in parts]
    mixed = [(gate_ref[r, :D_MODEL].astype(F32) * a + gate_ref[r, D_MODEL:].astype(F32) * s).astype(BF16)
             for r, (a, s) in zip(parts, branches)]
    summed = [DEEPNORM_ALPHA * x_ref[r, :] + dot(m, wout_ref[...]) for r, m in zip(parts, mixed)]
    for r, z in zip(parts, summed):
        o_ref[r, :] = _layer_norm(z, g_ref[...], b_ref[...])


def _merge(x, y_na, y_sw, gates, w_na, w_sw, w_out, g, b):
    n = x.shape[0]
    row = lambda w: pl.BlockSpec((PROJ_ROW_TILE, w), lambda i: (i, 0))
    full = lambda a: pl.BlockSpec(a.shape, lambda i: (0,) * a.ndim)
    g2, b2 = g.reshape(1, -1), b.reshape(1, -1)
    return pl.pallas_call(
        _merge_kernel, grid=(n // PROJ_ROW_TILE,),
        in_specs=[row(D_MODEL), row(NA_WIDTH), row(SW_WIDTH), row(2 * D_MODEL),
                  full(w_na), full(w_sw), full(w_out), full(g2), full(b2)],
        out_specs=row(D_MODEL), out_shape=jax.ShapeDtypeStruct((n, D_MODEL), F32),
        compiler_params=_params("parallel"), name="merge_ln1",
    )(x, y_na, y_sw, gates, w_na, w_sw, w_out, g2, b2)


def _router_kernel(x_ref, w_ref, cw_ref, pos_ref, post_ref, cnt_ref, zero_ref):
    zero_ref[...] = jnp.zeros_like(zero_ref)

    blocks = [slice(p * ROW_TILE, (p + 1) * ROW_TILE) for p in range(ROUTER_BLOCKS)]
    dot = functools.partial(jnp.dot, preferred_element_type=F32)
    w = w_ref[...]
    wh = w.astype(BF16)
    wl = (w - wh.astype(F32)).astype(BF16)
    xs = [x_ref[r, :] for r in blocks]
    xh = [x.astype(BF16) for x in xs]
    xl = [(x - h.astype(F32)).astype(BF16) for x, h in zip(xs, xh)]
    logits = [dot(h, wh) + dot(l, wh) + dot(h, wl) for h, l in zip(xh, xl)]
    lane = lax.broadcasted_iota(jnp.int32, (ROW_TILE, LANES), 1).astype(F32)
    lg = [jnp.where(lane < N_EXPERTS, t, -jnp.inf) for t in logits]
    m1 = [jnp.max(t, axis=-1, keepdims=True) for t in lg]
    i1 = [jnp.min(jnp.where(t == m, lane, float(LANES)), axis=-1, keepdims=True) for t, m in zip(lg, m1)]
    lg2 = [jnp.where(lane == i, -jnp.inf, t) for t, i in zip(lg, i1)]
    m2 = [jnp.max(t, axis=-1, keepdims=True) for t in lg2]
    i2 = [jnp.min(jnp.where(t == m, lane, float(LANES)), axis=-1, keepdims=True) for t, m in zip(lg2, m2)]
    e2 = [jnp.exp(b - a) for a, b in zip(m1, m2)]
    cw = [jnp.where(lane == a, 1.0 / (1.0 + e), 0.0) + jnp.where(lane == b, e / (1.0 + e), 0.0)
          for a, b, e in zip(i1, i2, e2)]

    t = ROW_TILE
    tri = jnp.where(lax.broadcasted_iota(jnp.int32, (t, t), 0) <= lax.broadcasted_iota(jnp.int32, (t, t), 1),
                    1.0, 0.0).astype(BF16)
    sel = [jnp.where(c.T[:N_EXPERTS] > 0.0, 1.0, 0.0) for c in cw]
    incl = [dot(s.astype(BF16), tri) for s in sel]
    post = [jnp.where(s > 0.0, n - 1.0, -1.0) for s, n in zip(sel, incl)]
    unrouted = jnp.full((LANES - N_EXPERTS, t), -1.0, F32)
    for p, r in enumerate(blocks):
        cw_ref[r, :] = cw[p]
        post_ref[:, r] = post[p]
        pos_ref[r, :] = jnp.concatenate([post[p], unrouted], axis=0).T
        cnt_ref[p * N_EXPERTS:(p + 1) * N_EXPERTS, :] = jnp.broadcast_to(incl[p][:, t - 1:t], (N_EXPERTS, LANES))


def _router(x, w_router, n_rows):
    n = x.shape[0]
    nblk = n // ROW_TILE
    step = ROUTER_BLOCKS * ROW_TILE
    steps = n // step
    assert n_rows % (steps * SEG_ALIGN) == 0
    w = jnp.zeros((D_MODEL, LANES), F32).at[:, :N_EXPERTS].set(w_router)
    tok = pl.BlockSpec((step, LANES), lambda i: (i, 0))
    cw, pos, post, cnt, zeros = pl.pallas_call(
        _router_kernel, grid=(steps,),
        in_specs=[pl.BlockSpec((step, D_MODEL), lambda i: (i, 0)),
                  pl.BlockSpec((D_MODEL, LANES), lambda i: (0, 0))],
        out_specs=[tok, tok, pl.BlockSpec((N_EXPERTS, step), lambda i: (0, i)),
                   pl.BlockSpec((ROUTER_BLOCKS * N_EXPERTS, LANES), lambda i: (i, 0)),
                   pl.BlockSpec((n_rows // steps, D_MODEL), lambda i: (i, 0))],
        out_shape=[jax.ShapeDtypeStruct((n, LANES), F32), jax.ShapeDtypeStruct((n, LANES), F32),
                   jax.ShapeDtypeStruct((N_EXPERTS, n), F32),
                   jax.ShapeDtypeStruct((nblk * N_EXPERTS, LANES), F32),
                   jax.ShapeDtypeStruct((n_rows, D_MODEL), BF16)],
        compiler_params=_params("parallel"), name="router_top2",
    )(x, w)
    return cw, pos, post, cnt[:, 0].astype(jnp.int32).reshape(nblk, N_EXPERTS), zeros


def _moe_layout(cnt, n_tiles):
    seg = (cnt + SEG_ALIGN - 1) // SEG_ALIGN * SEG_ALIGN
    rows_e = jnp.sum(seg, axis=0)
    tiles_e = (rows_e + MOE_CHUNK + MOE_ROW_TILE - 1) // MOE_ROW_TILE
    tile_end = jnp.cumsum(tiles_e)
    tile_off = tile_end - tiles_e
    seg_off = (tile_off * MOE_ROW_TILE)[None, :] + jnp.cumsum(seg, axis=0) - seg
    tile = jnp.arange(n_tiles, dtype=jnp.int32)
    tile_expert = jnp.minimum(jnp.sum(tile[:, None] >= tile_end[None, :], axis=1), N_EXPERTS - 1)
    occupied = jnp.clip(rows_e[tile_expert] - (tile - tile_off[tile_expert]) * MOE_ROW_TILE, 0, MOE_ROW_TILE)
    as_i32 = lambda t: t.astype(jnp.int32)
    return as_i32(seg_off).reshape(-1), as_i32(tile_expert), as_i32(occupied)


def _chunk_copies(off_ref, cnt_ref, block, hbm_ref, vmem_ref, sems, to_hbm):
    slot = block % 2
    out = []
    for e in range(N_EXPERTS):
        c = cnt_ref[block * N_EXPERTS + e]
        off = pl.multiple_of(off_ref[block * N_EXPERTS + e], SEG_ALIGN)
        for k in range(ROW_TILE // MOE_CHUNK):
            idx = k * N_EXPERTS + e
            rows = hbm_ref.at[pl.ds(off + k * MOE_CHUNK, MOE_CHUNK)]
            staged = vmem_ref.at[slot, idx]
            src, dst = (staged, rows) if to_hbm else (rows, staged)
            out.append((c > k * MOE_CHUNK, e, k, idx, pltpu.make_async_copy(src, dst, sems.at[slot, idx])))
    return out


def _dispatch_kernel(off_ref, cnt_ref, x_ref, post_ref, xs_in_ref, xs_ref, stage_ref, sems):
    del xs_in_ref
    b, nblk = pl.program_id(0), pl.num_programs(0)
    copies_of = lambda blk: _chunk_copies(off_ref, cnt_ref, blk, xs_ref, stage_ref, sems, to_hbm=True)

    def wait_all(blk):
        for pred, _, _, _, copy in copies_of(blk):
            pl.when(pred)(copy.wait)

    xb = x_ref[...].astype(BF16)
    slot_row = lax.broadcasted_iota(jnp.int32, (MOE_CHUNK, ROW_TILE), 0).astype(F32)

    def gather_matrix(e, k):
        return jnp.where(post_ref[e:e + 1, :] == slot_row + float(k * MOE_CHUNK), 1.0, 0.0).astype(BF16)

    firsts = jnp.concatenate([gather_matrix(e, 0) for e in range(N_EXPERTS)], axis=0)
    stage_ref[b % 2, 0:N_EXPERTS] = jnp.dot(firsts, xb, preferred_element_type=F32).astype(BF16).reshape(
        N_EXPERTS, MOE_CHUNK, D_MODEL)
    copies = copies_of(b)
    for pred, e, k, idx, _ in copies:
        if k > 0:
            @pl.when(pred)
            def _(e=e, k=k, idx=idx):
                stage_ref[b % 2, idx] = jnp.dot(gather_matrix(e, k), xb, preferred_element_type=F32).astype(BF16)
    pl.when(b >= 1)(lambda: wait_all(b - 1))
    for pred, _, _, _, copy in copies:
        pl.when(pred)(copy.start)
    pl.when(b == nblk - 1)(lambda: wait_all(b))


def _dispatch(x, post, seg_off, cnt, zeros):
    n = x.shape[0]
    n_chunks = N_EXPERTS * (ROW_TILE // MOE_CHUNK)
    grid_spec = pltpu.PrefetchScalarGridSpec(
        num_scalar_prefetch=2, grid=(n // ROW_TILE,),
        in_specs=[pl.BlockSpec((ROW_TILE, D_MODEL), lambda i, *_: (i, 0)),
                  pl.BlockSpec((N_EXPERTS, ROW_TILE), lambda i, *_: (0, i)),
                  pl.BlockSpec(memory_space=pl.ANY)],
        out_specs=pl.BlockSpec(memory_space=pl.ANY),
        scratch_shapes=[pltpu.VMEM((2, n_chunks, MOE_CHUNK, D_MODEL), BF16),
                        pltpu.SemaphoreType.DMA((2, n_chunks))])
    return pl.pallas_call(
        _dispatch_kernel, grid_spec=grid_spec,
        out_shape=jax.ShapeDtypeStruct(zeros.shape, BF16),
        input_output_aliases={4: 0},
        compiler_params=_params("arbitrary"), name="moe_dispatch",
    )(seg_off, cnt.reshape(-1), x, post, zeros)


def _swiglu_partial(xb, wg, wu, wd):
    hg = jnp.dot(xb, wg, preferred_element_type=F32)
    hu = jnp.dot(xb, wu, preferred_element_type=F32)
    h = hg * _sigmoid(hg) * hu
    return jnp.dot(h.astype(BF16), wd, preferred_element_type=F32)


def _moe_ffn_kernel(expert_ref, occ_ref, x_ref, wg_ref, wu_ref, wd_ref, o_ref, acc_ref):
    del expert_ref
    i, j = pl.program_id(0), pl.program_id(1)
    occupied = occ_ref[i]

    @pl.when(j == 0)
    def _():
        acc_ref[...] = jnp.zeros_like(acc_ref)

    def weights():
        return wg_ref[...].astype(BF16), wu_ref[...].astype(BF16), wd_ref[...].astype(BF16)

    @pl.when(occupied == MOE_ROW_TILE)
    def _():
        acc_ref[...] += _swiglu_partial(x_ref[...], *weights())

    @pl.when((occupied > 0) & (occupied < MOE_ROW_TILE))
    def _():
        wg, wu, wd = weights()
        for s in range(MOE_ROW_TILE // MOE_CHUNK):
            rows = slice(s * MOE_CHUNK, (s + 1) * MOE_CHUNK)

            @pl.when(s * MOE_CHUNK < occupied)
            def _(rows=rows):
                acc_ref[rows, :] += _swiglu_partial(x_ref[rows, :], wg, wu, wd)

    @pl.when(j == pl.num_programs(1) - 1)
    def _():
        o_ref[...] = acc_ref[...].astype(BF16)


def _moe_ffn(xs, tile_expert, occupied, w_gate, w_up, w_down):
    n_rows = xs.shape[0]
    d_ff = w_gate.shape[-1]
    row = pl.BlockSpec((MOE_ROW_TILE, D_MODEL), lambda i, j, *_: (i, 0))
    up = pl.BlockSpec((None, D_MODEL, MOE_FF_TILE), lambda i, j, ex, occ: (ex[i], 0, j))
    down = pl.BlockSpec((None, MOE_FF_TILE, D_MODEL), lambda i, j, ex, occ: (ex[i], j, 0))
    grid_spec = pltpu.PrefetchScalarGridSpec(
        num_scalar_prefetch=2, grid=(n_rows // MOE_ROW_TILE, d_ff // MOE_FF_TILE),
        in_specs=[row, up, up, down], out_specs=row,
        scratch_shapes=[pltpu.VMEM((MOE_ROW_TILE, D_MODEL), F32)])
    return pl.pallas_call(
        _moe_ffn_kernel, grid_spec=grid_spec,
        out_shape=jax.ShapeDtypeStruct((n_rows, D_MODEL), BF16),
        compiler_params=_params("parallel", "arbitrary"), name="moe_swiglu",
    )(tile_expert, occupied, xs, w_gate, w_up, w_down)


def _ffn_kernel(x_ref, wg_ref, wu_ref, wd_ref, g_ref, b_ref, o_ref):
    x = x_ref[...]
    y = _swiglu_partial(x.astype(BF16), wg_ref[...], wu_ref[...], wd_ref[...])
    o_ref[...] = _layer_norm(DEEPNORM_ALPHA * x + y, g_ref[...], b_ref[...])


def _ffn(x, w_gate, w_up, w_down, g, b):
    n = x.shape[0]
    row = pl.BlockSpec((PROJ_ROW_TILE, D_MODEL), lambda i: (i, 0))
    const = lambda a: pl.BlockSpec(a.shape, lambda i: (0,) * a.ndim, pipeline_mode=pl.Buffered(1))
    g2, b2 = g.reshape(1, -1), b.reshape(1, -1)
    return pl.pallas_call(
        _ffn_kernel, grid=(n // PROJ_ROW_TILE,),
        in_specs=[row, const(w_gate), const(w_up), const(w_down), const(g2), const(b2)], out_specs=row,
        out_shape=jax.ShapeDtypeStruct((n, D_MODEL), F32),
        compiler_params=_params("parallel"), name="swiglu_ln2",
    )(x, w_gate, w_up, w_down, g2, b2)


def _combine_kernel(off_ref, cnt_ref, x_ref, cw_ref, pos_ref, g_ref, b_ref, ys_ref, o_ref,
                    buf_ref, acc_ref, sems):
    b, nblk = pl.program_id(0), pl.num_programs(0)
    copies_of = lambda blk: _chunk_copies(off_ref, cnt_ref, blk, ys_ref, buf_ref, sems, to_hbm=False)

    def start_all(blk):
        for pred, _, k, _, copy in copies_of(blk):
            if k == 0:
                copy.start()
            else:
                pl.when(pred)(copy.start)

    pl.when(b == 0)(lambda: start_all(b))
    pl.when(b + 1 < nblk)(lambda: start_all(b + 1))

    lane = lax.broadcasted_iota(jnp.int32, (ROW_TILE, LANES), 1)
    slot_col = lax.broadcasted_iota(jnp.int32, (ROW_TILE, MOE_CHUNK), 1).astype(F32)
    column = lambda ref, e: jnp.sum(jnp.where(lane == e, ref[...], 0.0), axis=-1, keepdims=True)
    weight = [column(cw_ref, e) for e in range(N_EXPERTS)]
    pos = [column(pos_ref, e) for e in range(N_EXPERTS)]

    def scatter_matrix(e, k):
        return jnp.where(pos[e] == slot_col + float(k * MOE_CHUNK), weight[e], 0.0).astype(BF16)

    first = jnp.concatenate([scatter_matrix(e, 0) for e in range(N_EXPERTS)], axis=1)
    copies = copies_of(b)
    for _, _, k, _, copy in copies:
        if k == 0:
            copy.wait()
    firsts = buf_ref[b % 2, 0:N_EXPERTS].reshape(N_EXPERTS * MOE_CHUNK, D_MODEL)
    acc_ref[...] = jnp.dot(first, firsts, preferred_element_type=F32)
    for pred, e, k, idx, copy in copies:
        if k > 0:
            @pl.when(pred)
            def _(e=e, k=k, idx=idx, copy=copy):
                copy.wait()
                acc_ref[...] += jnp.dot(scatter_matrix(e, k), buf_ref[b % 2, idx], preferred_element_type=F32)

    o_ref[...] = _layer_norm(DEEPNORM_ALPHA * x_ref[...] + acc_ref[...], g_ref[...], b_ref[...])


def _combine(x, ys, cw, pos, seg_off, cnt, g, b):
    n = x.shape[0]
    n_chunks = N_EXPERTS * (ROW_TILE // MOE_CHUNK)
    tok = lambda w: pl.BlockSpec((ROW_TILE, w), lambda i, *_: (i, 0))
    vec = pl.BlockSpec((1, D_MODEL), lambda i, *_: (0, 0))
    grid_spec = pltpu.PrefetchScalarGridSpec(
        num_scalar_prefetch=2, grid=(n // ROW_TILE,),
        in_specs=[tok(D_MODEL), tok(LANES), tok(LANES), vec, vec, pl.BlockSpec(memory_space=pl.ANY)],
        out_specs=tok(D_MODEL),
        scratch_shapes=[pltpu.VMEM((2, n_chunks, MOE_CHUNK, D_MODEL), BF16),
                        pltpu.VMEM((ROW_TILE, D_MODEL), F32),
                        pltpu.SemaphoreType.DMA((2, n_chunks))])
    return pl.pallas_call(
        _combine_kernel, grid_spec=grid_spec,
        out_shape=jax.ShapeDtypeStruct((n, D_MODEL), F32),
        compiler_params=_params("arbitrary"), name="moe_combine_ln2",
    )(seg_off, cnt.reshape(-1), x, cw, pos, g.reshape(1, -1), b.reshape(1, -1), ys)


def _moe(x, w_router, w_gate, w_up, w_down, g, b):
    n = x.shape[0]
    nblk = n // ROW_TILE
    max_rows = 2 * n + nblk * N_EXPERTS * (SEG_ALIGN - 1) + N_EXPERTS * (MOE_CHUNK + MOE_ROW_TILE - 1)
    n_tiles = max_rows // MOE_ROW_TILE
    cw, pos, post, cnt, zeros = _router(x, w_router, n_tiles * MOE_ROW_TILE)
    seg_off, tile_expert, occupied = _moe_layout(cnt, n_tiles)
    xs = _dispatch(x, post, seg_off, cnt, zeros)
    ys = _moe_ffn(xs, tile_expert, occupied, w_gate, w_up, w_down)
    return _combine(x, ys, cw, pos, seg_off, cnt, g, b)


def kernel(x, emb_ln_g, emb_ln_b, w_in, b_gate, na_rpb, sw_sink, w_branch_na, w_branch_sw, w_out,
           ln1_g, ln1_b, ffn_w_gate, ffn_w_up, ffn_w_down, moe_router, moe_w_gate, moe_w_up,
           moe_w_down, ln2_g, ln2_b):
    batch, seq, d = x.shape
    assert (seq, d) == (SEQ, D_MODEL)
    n = batch * seq
    tables = _rotary_tables()
    h = x.reshape(n, d)
    for layer in range(DEPTH):
        h, q_na, k_na, v_na, q_sw, k_sw, v_sw, gates = _proj(
            h, w_in[layer].astype(BF16), b_gate[layer], tables,
            embed_ln=(emb_ln_g, emb_ln_b) if layer == 0 else None)
        seq3 = lambda t: t.reshape(batch, seq, t.shape[-1])
        y_na = _na_attention(seq3(q_na), seq3(k_na), seq3(v_na), _na_bias_tables(na_rpb[layer]))
        y_sw = _sw_attention(seq3(q_sw), seq3(k_sw), seq3(v_sw), sw_sink[layer])
        h = _merge(h, y_na.reshape(n, -1), y_sw.reshape(n, -1), gates,
                   w_branch_na[layer].astype(BF16), w_branch_sw[layer].astype(BF16),
                   w_out[layer].astype(BF16), ln1_g[layer], ln1_b[layer])
        i = layer // 2
        if layer % 2 == 0:
            h = _ffn(h, ffn_w_gate[i].astype(BF16), ffn_w_up[i].astype(BF16),
                     ffn_w_down[i].astype(BF16), ln2_g[layer], ln2_b[layer])
        else:
            h = _moe(h, moe_router[i], moe_w_gate[i], moe_w_up[i], moe_w_down[i],
                     ln2_g[layer], ln2_b[layer])
    return h.reshape(batch, seq, d)
```

```python
import functools

import numpy as np
import jax
import jax.numpy as jnp
from jax import lax
from jax.experimental import pallas as pl
from jax.experimental.pallas import tpu as pltpu

F32 = jnp.float32
BF16 = jnp.bfloat16

D_MODEL = 1024
SEQ = 2048
DEPTH = 2
HEAD_DIM = 64
NA_HEADS = 8
NA_WIDTH = NA_HEADS * HEAD_DIM
GRID_W = 64
GRID_ROWS = SEQ // GRID_W
NA_KH = 8
NA_KW = 16
SW_HEADS = 8
SW_KV_HEADS = 2
SW_GROUP = SW_HEADS // SW_KV_HEADS
SW_WIDTH = SW_HEADS * HEAD_DIM
SW_KV_WIDTH = SW_KV_HEADS * HEAD_DIM
SW_WINDOW = 128
SW_BLOCK = 128
ROT_DIM = HEAD_DIM // 4
ROPE_THETA = 500000.0
OFF_QNA = NA_WIDTH
OFF_KNA = 2 * NA_WIDTH
OFF_VNA = 3 * NA_WIDTH
OFF_QSW = OFF_VNA + SW_WIDTH
OFF_KSW = OFF_QSW + SW_KV_WIDTH
OFF_VSW = OFF_KSW + SW_KV_WIDTH
PROJ_COLS = OFF_VSW + 2 * D_MODEL
N_EXPERTS = 8
DEEPNORM_ALPHA = (2 * DEPTH) ** 0.25
LN_EPS = 1e-5
NEG_INF = -1e30
QK_SCALE = HEAD_DIM ** -0.5
LOG2E = float(np.log2(np.e))

LANES = 128
MXU_DIM = 256
V7X_VMEM_BYTES = 64 * 1024 * 1024
VMEM_LIMIT = V7X_VMEM_BYTES * 7 // 8

PROJ_ROW_TILE = 1024
ROW_TILE = 512
MOE_ROW_TILE = 1536
MOE_FF_TILE = 512
MOE_CHUNK = 256
SEG_ALIGN = 16
NA_ROWS_PER_STEP = 8
SW_BLOCKS_PER_STEP = 8
MERGE_PARTS = 4
ROUTER_BLOCKS = 4


def _layer_norm(z, g, b):
    mu = jnp.mean(z, axis=-1, keepdims=True)
    d = z - mu
    var = jnp.mean(d * d, axis=-1, keepdims=True)
    return d * lax.rsqrt(var + LN_EPS) * g + b


def _sigmoid(z):
    return 1.0 / (1.0 + jnp.exp(-z))


def _params(*sem):
    return pltpu.CompilerParams(dimension_semantics=sem, vmem_limit_bytes=VMEM_LIMIT)


def _rotary_tables():
    half = ROT_DIM // 2
    inv_freq = 1.0 / (ROPE_THETA ** (jnp.arange(0, ROT_DIM, 2, dtype=F32) / ROT_DIM))
    ang = jnp.arange(SEQ, dtype=jnp.int32).astype(F32)[:, None] * inv_freq[None, :]
    cos, sin = jnp.cos(ang), jnp.sin(ang)
    ones = jnp.ones((SEQ, HEAD_DIM - ROT_DIM), F32)
    zeros = jnp.zeros((SEQ, HEAD_DIM - ROT_DIM), F32)
    zh = jnp.zeros((SEQ, half), F32)
    cos_h = jnp.concatenate([cos, cos, ones], axis=1)
    sa_h = jnp.concatenate([-sin, zh, zeros], axis=1)
    sb_h = jnp.concatenate([zh, sin, zeros], axis=1)
    two = lambda t: jnp.concatenate([t, t], axis=1)
    return two(cos_h), two(sa_h), two(sb_h)


def _proj_kernel(*refs, embed_ln):
    if embed_ln:
        x_ref, g_ref, b_ref, w_ref, bg_ref, cos_ref, sa_ref, sb_ref, xn_ref, *outs = refs
        x = _layer_norm(x_ref[...], g_ref[...], b_ref[...])
        xn_ref[...] = x
    else:
        x_ref, w_ref, bg_ref, cos_ref, sa_ref, sb_ref, *outs = refs
        x = x_ref[...]
    qna_ref, kna_ref, vna_ref, qsw_ref, ksw_ref, vsw_ref, gate_ref = outs
    xb = x.astype(BF16)

    def mm(lo, hi):
        return jnp.dot(xb, w_ref[:, lo:hi], preferred_element_type=F32)

    qna_ref[...] = (mm(0, OFF_QNA) * (QK_SCALE * LOG2E)).astype(BF16)
    kna_ref[...] = mm(OFF_QNA, OFF_KNA).astype(BF16)
    vna_ref[...] = mm(OFF_KNA, OFF_VNA).astype(BF16)

    cos, sa, sb = cos_ref[...], sa_ref[...], sb_ref[...]
    half = ROT_DIM // 2

    def rot(t):
        return t * cos + pltpu.roll(t, LANES - half, 1) * sa + pltpu.roll(t, half, 1) * sb

    q = mm(OFF_VNA, OFF_QSW)
    for c in range(SW_WIDTH // LANES):
        sl = slice(c * LANES, (c + 1) * LANES)
        qsw_ref[:, sl] = (rot(q[:, sl]) * (QK_SCALE * LOG2E)).astype(BF16)

    kv = mm(OFF_QSW, OFF_VSW)
    k = rot(kv[:, :SW_KV_WIDTH])
    v = kv[:, SW_KV_WIDTH:]
    first = lax.broadcasted_iota(jnp.int32, k.shape, 1) < HEAD_DIM

    def dup(t):
        r = pltpu.roll(t, HEAD_DIM, 1)
        return jnp.concatenate([jnp.where(first, t, r), jnp.where(first, r, t)], axis=1)

    ksw_ref[...] = dup(k).astype(BF16)
    vsw_ref[...] = v.astype(BF16)

    gw = 512
    for c in range(2 * D_MODEL // gw):
        z = mm(OFF_VSW + c * gw, OFF_VSW + (c + 1) * gw) + bg_ref[:, c * gw:(c + 1) * gw]
        gate_ref[:, c * gw:(c + 1) * gw] = _sigmoid(z).astype(BF16)


def _proj(x, w_bf, b_gate, tables, embed_ln=None):
    n = x.shape[0]
    tiles_per_seq = SEQ // PROJ_ROW_TILE
    row = lambda w: pl.BlockSpec((PROJ_ROW_TILE, w), lambda i: (i, 0))
    full = lambda a: pl.BlockSpec(a.shape, lambda i: (0,) * a.ndim)
    tab = pl.BlockSpec((PROJ_ROW_TILE, LANES), lambda i: (i % tiles_per_seq, 0))
    bg = b_gate.reshape(1, -1)
    widths = (NA_WIDTH, NA_WIDTH, NA_WIDTH, SW_WIDTH, 2 * SW_KV_WIDTH, SW_KV_WIDTH, 2 * D_MODEL)
    ln = [t.reshape(1, -1) for t in embed_ln] if embed_ln else []
    out_specs = [row(w) for w in widths]
    out_shape = [jax.ShapeDtypeStruct((n, w), BF16) for w in widths]
    if embed_ln:
        out_specs = [row(D_MODEL)] + out_specs
        out_shape = [jax.ShapeDtypeStruct((n, D_MODEL), F32)] + out_shape
    outs = pl.pallas_call(
        functools.partial(_proj_kernel, embed_ln=bool(embed_ln)), grid=(n // PROJ_ROW_TILE,),
        in_specs=[row(D_MODEL)] + [full(t) for t in ln] + [full(w_bf), full(bg), tab, tab, tab],
        out_specs=out_specs, out_shape=out_shape,
        compiler_params=_params("parallel"), name="in_proj",
    )(x, *ln, w_bf, bg, *tables)
    return outs if embed_ln else [x] + list(outs)


def _na_bias_tables(rpb):
    c = np.arange(GRID_W)
    qcs = np.clip(c - NA_KW // 2, 0, GRID_W - NA_KW)
    valid = (c[None, :] >= qcs[:, None]) & (c[None, :] < qcs[:, None] + NA_KW)
    n_dr, n_dc = 2 * NA_KH - 1, 2 * NA_KW - 1
    span = 2 * GRID_W - 1
    lead = GRID_W - NA_KW
    ext = jnp.pad(rpb.astype(F32), ((0, 0), (0, 0), (lead, span + 1 - lead - n_dc)))
    flat = jnp.broadcast_to(ext[:, :, None, :], (NA_HEADS, n_dr, GRID_W, span + 1))
    flat = flat.reshape(NA_HEADS, n_dr, GRID_W * (span + 1))[:, :, :GRID_W * span]
    t = flat.reshape(NA_HEADS, n_dr, GRID_W, span)[:, :, :, GRID_W - 1:]
    t = jnp.where(valid[None, None], t * LOG2E, NEG_INF)
    pairs = jnp.concatenate([t[:, :-1], t[:, 1:]], axis=-1)
    pairs = pairs.reshape(2, 4, n_dr - 1, GRID_W, 2 * GRID_W).transpose(0, 2, 1, 3, 4)
    return pairs.reshape(2, n_dr - 1, 4 * GRID_W, 2 * GRID_W)


def _na_kernel(q_ref, k_ref, v_ref, bias_ref, o_ref):
    lane = lax.broadcasted_iota(jnp.int32, (GRID_W, MXU_DIM), 1)
    head_mask = [(lane >= h * HEAD_DIM) & (lane < (h + 1) * HEAD_DIM) for h in range(4)]
    nk = NA_KH * GRID_W

    def rows(it, carry):
        chains = []
        for u in range(NA_ROWS_PER_STEP):
            r = it * NA_ROWS_PER_STEP + u
            rs = jnp.clip(r - NA_KH // 2, 0, GRID_ROWS - NA_KH)
            variant = rs - r + (NA_KH - 1)
            qrows = pl.ds(pl.multiple_of(r * GRID_W, GRID_W), GRID_W)
            krows = pl.ds(pl.multiple_of(rs * GRID_W, GRID_W), nk)
            q = q_ref[qrows, :]
            for g in range(2):
                gs = slice(g * MXU_DIM, (g + 1) * MXU_DIM)
                qg = q[:, gs]
                lhs = jnp.concatenate([jnp.where(m, qg, jnp.zeros_like(qg)) for m in head_mask], axis=0)
                bias = jnp.concatenate([bias_ref[g, variant + 2 * a] for a in range(NA_KH // 2)], axis=1)
                chains.append(dict(lhs=lhs, k=k_ref[krows, gs], v=v_ref[krows, gs], bias=bias,
                                   out=(qrows, gs)))
        for c in chains:
            c["s"] = lax.dot_general(c["lhs"], c["k"], (((1,), (1,)), ((), ())),
                                     preferred_element_type=F32) + c["bias"]
        for c in chains:
            c["p"] = jnp.exp2(c["s"] - jnp.max(c["s"], axis=-1, keepdims=True))
        for c in chains:
            c["l"] = jnp.sum(c["p"], axis=-1, keepdims=True)
        for c in chains:
            c["o"] = jnp.dot(c["p"].astype(BF16), c["v"], preferred_element_type=F32) / c["l"]
        for c in chains:
            og = jnp.zeros((GRID_W, MXU_DIM), F32)
            for h in range(4):
                og = og + jnp.where(head_mask[h], c["o"][h * GRID_W:(h + 1) * GRID_W], 0.0)
            o_ref[c["out"]] = og.astype(BF16)
        return carry

    lax.fori_loop(0, GRID_ROWS // NA_ROWS_PER_STEP, rows, 0)


def _na_attention(q, k, v, bias):
    b = q.shape[0]
    seq = pl.BlockSpec((None, SEQ, NA_WIDTH), lambda i: (i, 0, 0))
    return pl.pallas_call(
        _na_kernel, grid=(b,),
        in_specs=[seq, seq, seq, pl.BlockSpec(bias.shape, lambda i: (0, 0, 0, 0))],
        out_specs=seq, out_shape=jax.ShapeDtypeStruct((b, SEQ, NA_WIDTH), BF16),
        compiler_params=_params("parallel"), name="na_attention",
    )(q, k, v, bias)


def _sw_masks():
    nkeys = 3 * SW_BLOCK
    jk = np.arange(nkeys)[:, None]
    iq = np.arange(SW_BLOCK)[None, :]
    band = (jk >= iq) & (jk <= iq + 2 * SW_WINDOW)
    in_seq = [(jk >= SW_BLOCK), np.ones_like(band), (jk < 2 * SW_BLOCK)]
    return jnp.asarray(np.stack([np.where(band & ok, 0.0, NEG_INF) for ok in in_seq]), F32)


def _sw_kernel(sink_ref, q_ref, k_ref, v_ref, mask_ref, o_ref):
    nb = SEQ // SW_BLOCK
    first = lax.broadcasted_iota(jnp.int32, (SW_BLOCK, LANES), 1) < HEAD_DIM
    ones = jnp.ones((8, 3 * SW_BLOCK), BF16)

    def blocks(it, carry):
        chains = []
        for u in range(SW_BLOCKS_PER_STEP):
            n = it * SW_BLOCKS_PER_STEP + u
            mask = mask_ref[jnp.where(n == 0, 0, jnp.where(n == nb - 1, 2, 1))]
            starts = [pl.multiple_of(jnp.clip(n + d, 0, nb - 1) * SW_BLOCK, SW_BLOCK) for d in (-1, 0, 1)]
            qrows = pl.ds(pl.multiple_of(n * SW_BLOCK, SW_BLOCK), SW_BLOCK)
            q = q_ref[qrows, :]
            vv = jnp.concatenate([v_ref[pl.ds(s, SW_BLOCK), :] for s in starts], axis=0)
            vt = vv.astype(F32).T.astype(BF16)
            for kvh in range(SW_KV_HEADS):
                ks = slice(kvh * LANES, (kvh + 1) * LANES)
                kk = jnp.concatenate([k_ref[pl.ds(s, SW_BLOCK), ks] for s in starts], axis=0)
                parts, sinks = [], []
                for j in range(SW_GROUP):
                    h = kvh * SW_GROUP + j
                    qc = q[:, (h // 2) * LANES:(h // 2 + 1) * LANES]
                    keep = first if h % 2 == 0 else jnp.logical_not(first)
                    parts.append(jnp.where(keep, qc, jnp.zeros_like(qc)))
                    sinks.append(jnp.full((1, SW_BLOCK), sink_ref[h] * LOG2E, F32))
                vt_ones = jnp.concatenate([vt[kvh * HEAD_DIM:(kvh + 1) * HEAD_DIM], ones], axis=0)
                chains.append(dict(k=kk, vt=vt_ones, q=jnp.concatenate(parts, axis=0),
                                   sink=jnp.concatenate(sinks, axis=1), mask=mask, qrows=qrows, kvh=kvh))
        for c in chains:
            s = lax.dot_general(c["k"], c["q"], (((1,), (1,)), ((), ())), preferred_element_type=F32)
            prev_mask, next_mask = c["mask"][:SW_BLOCK], c["mask"][2 * SW_BLOCK:]
            cols = []
            for j in range(SW_GROUP):
                sj = s[:, j * SW_BLOCK:(j + 1) * SW_BLOCK]
                cols.append(jnp.concatenate([sj[:SW_BLOCK] + prev_mask, sj[SW_BLOCK:2 * SW_BLOCK],
                                             sj[2 * SW_BLOCK:] + next_mask], axis=0))
            c["s"] = jnp.concatenate(cols, axis=1)
        for c in chains:
            c["m"] = jnp.maximum(jnp.max(c["s"], axis=0, keepdims=True), c["sink"])
        for c in chains:
            c["p"] = jnp.exp2(c["s"] - c["m"]).astype(BF16)
        for c in chains:
            o = jnp.dot(c["vt"], c["p"], preferred_element_type=F32)
            denom = o[HEAD_DIM:HEAD_DIM + 1] + jnp.exp2(c["sink"] - c["m"])
            c["o"] = o[:HEAD_DIM] / denom
        for c in chains:
            for pair in range(SW_GROUP // 2):
                even = c["o"][:, (2 * pair) * SW_BLOCK:(2 * pair + 1) * SW_BLOCK]
                odd = c["o"][:, (2 * pair + 1) * SW_BLOCK:(2 * pair + 2) * SW_BLOCK]
                col = (c["kvh"] * (SW_GROUP // 2) + pair) * LANES
                o_ref[c["qrows"], col:col + LANES] = jnp.concatenate([even, odd], axis=0).T.astype(BF16)
        return carry

    lax.fori_loop(0, nb // SW_BLOCKS_PER_STEP, blocks, 0)


def _sw_attention(q, k2, v, sink):
    b = q.shape[0]
    masks = _sw_masks()
    seq = lambda w: pl.BlockSpec((None, SEQ, w), lambda i: (i, 0, 0))
    return pl.pallas_call(
        _sw_kernel, grid=(b,),
        in_specs=[pl.BlockSpec(memory_space=pltpu.SMEM), seq(SW_WIDTH), seq(2 * SW_KV_WIDTH),
                  seq(SW_KV_WIDTH), pl.BlockSpec(masks.shape, lambda i: (0, 0, 0))],
        out_specs=seq(SW_WIDTH), out_shape=jax.ShapeDtypeStruct((b, SEQ, SW_WIDTH), BF16),
        compiler_params=_params("parallel"), name="sw_attention",
    )(sink.astype(F32), q, k2, v, masks)


def _merge_kernel(x_ref, yna_ref, ysw_ref, gate_ref, wna_ref, wsw_ref, wout_ref, g_ref, b_ref, o_ref):
    part = x_ref.shape[0] // MERGE_PARTS
    parts = [slice(p * part, (p + 1) * part) for p in range(MERGE_PARTS)]
    dot = functools.partial(jnp.dot, preferred_element_type=F32)
    branches = [(dot(yna_ref[r, :], wna_ref[...]), dot(ysw_ref[r, :], wsw_ref[...])) for r in parts]
    mixed = [(gate_ref[r, :D_MODEL].astype(F32) * a + gate_ref[r, D_MODEL:].astype(F32) * s).astype(BF16)
             for r, (a, s) in zip(parts, branches)]
    summed = [DEEPNORM_ALPHA * x_ref[r, :] + dot(m, wout_ref[...]) for r, m in zip(parts, mixed)]
    for r, z in zip(parts, summed):
        o_ref[r, :] = _layer_norm(z, g_ref[...], b_ref[...])


def _merge(x, y_na, y_sw, gates, w_na, w_sw, w_out, g, b):
    n = x.shape[0]
    row = lambda w: pl.BlockSpec((PROJ_ROW_TILE, w), lambda i: (i, 0))
    full = lambda a: pl.BlockSpec(a.shape, lambda i: (0,) * a.ndim)
    g2, b2 = g.reshape(1, -1), b.reshape(1, -1)
    return pl.pallas_call(
        _merge_kernel, grid=(n // PROJ_ROW_TILE,),
        in_specs=[row(D_MODEL), row(NA_WIDTH), row(SW_WIDTH), row(2 * D_MODEL),
                  full(w_na), full(w_sw), full(w_out), full(g2), full(b2)],
        out_specs=row(D_MODEL), out_shape=jax.ShapeDtypeStruct((n, D_MODEL), F32),
        compiler_params=_params("parallel"), name="merge_ln1",
    )(x, y_na, y_sw, gates, w_na, w_sw, w_out, g2, b2)


def _router_kernel(x_ref, w_ref, cw_ref, pos_ref, post_ref, cnt_ref, zero_ref):
    zero_ref[...] = jnp.zeros_like(zero_ref)

    blocks = [slice(p * ROW_TILE, (p + 1) * ROW_TILE) for p in range(ROUTER_BLOCKS)]
    dot = functools.partial(jnp.dot, preferred_element_type=F32)
    w = w_ref[...]
    wh = w.astype(BF16)
    wl = (w - wh.astype(F32)).astype(BF16)
    xs = [x_ref[r, :] for r in blocks]
    xh = [x.astype(BF16) for x in xs]
    xl = [(x - h.astype(F32)).astype(BF16) for x, h in zip(xs, xh)]
    logits = [dot(h, wh) + dot(l, wh) + dot(h, wl) for h, l in zip(xh, xl)]
    lane = lax.broadcasted_iota(jnp.int32, (ROW_TILE, LANES), 1).astype(F32)
    lg = [jnp.where(lane < N_EXPERTS, t, -jnp.inf) for t in logits]
    m1 = [jnp.max(t, axis=-1, keepdims=True) for t in lg]
    i1 = [jnp.min(jnp.where(t == m, lane, float(LANES)), axis=-1, keepdims=True) for t, m in zip(lg, m1)]
    lg2 = [jnp.where(lane == i, -jnp.inf, t) for t, i in zip(lg, i1)]
    m2 = [jnp.max(t, axis=-1, keepdims=True) for t in lg2]
    i2 = [jnp.min(jnp.where(t == m, lane, float(LANES)), axis=-1, keepdims=True) for t, m in zip(lg2, m2)]
    e2 = [jnp.exp(b - a) for a, b in zip(m1, m2)]
    cw = [jnp.where(lane == a, 1.0 / (1.0 + e), 0.0) + jnp.where(lane == b, e / (1.0 + e), 0.0)
          for a, b, e in zip(i1, i2, e2)]

    t = ROW_TILE
    tri = jnp.where(lax.broadcasted_iota(jnp.int32, (t, t), 0) <= lax.broadcasted_iota(jnp.int32, (t, t), 1),
                    1.0, 0.0).astype(BF16)
    sel = [jnp.where(c.T[:N_EXPERTS] > 0.0, 1.0, 0.0) for c in cw]
    incl = [dot(s.astype(BF16), tri) for s in sel]
    post = [jnp.where(s > 0.0, n - 1.0, -1.0) for s, n in zip(sel, incl)]
    unrouted = jnp.full((LANES - N_EXPERTS, t), -1.0, F32)
    for p, r in enumerate(blocks):
        cw_ref[r, :] = cw[p]
        post_ref[:, r] = post[p]
        pos_ref[r, :] = jnp.concatenate([post[p], unrouted], axis=0).T
        cnt_ref[p * N_EXPERTS:(p + 1) * N_EXPERTS, :] = jnp.broadcast_to(incl[p][:, t - 1:t], (N_EXPERTS, LANES))


def _router(x, w_router, n_rows):
    n = x.shape[0]
    nblk = n // ROW_TILE
    step = ROUTER_BLOCKS * ROW_TILE
    steps = n // step
    assert n_rows % (steps * SEG_ALIGN) == 0
    w = jnp.zeros((D_MODEL, LANES), F32).at[:, :N_EXPERTS].set(w_router)
    tok = pl.BlockSpec((step, LANES), lambda i: (i, 0))
    cw, pos, post, cnt, zeros = pl.pallas_call(
        _router_kernel, grid=(steps,),
        in_specs=[pl.BlockSpec((step, D_MODEL), lambda i: (i, 0)),
                  pl.BlockSpec((D_MODEL, LANES), lambda i: (0, 0))],
        out_specs=[tok, tok, pl.BlockSpec((N_EXPERTS, step), lambda i: (0, i)),
                   pl.BlockSpec((ROUTER_BLOCKS * N_EXPERTS, LANES), lambda i: (i, 0)),
                   pl.BlockSpec((n_rows // steps, D_MODEL), lambda i: (i, 0))],
        out_shape=[jax.ShapeDtypeStruct((n, LANES), F32), jax.ShapeDtypeStruct((n, LANES), F32),
                   jax.ShapeDtypeStruct((N_EXPERTS, n), F32),
                   jax.ShapeDtypeStruct((nblk * N_EXPERTS, LANES), F32),
                   jax.ShapeDtypeStruct((n_rows, D_MODEL), BF16)],
        compiler_params=_params("parallel"), name="router_top2",
    )(x, w)
    return cw, pos, post, cnt[:, 0].astype(jnp.int32).reshape(nblk, N_EXPERTS), zeros


def _moe_layout(cnt, n_tiles):
    seg = (cnt + SEG_ALIGN - 1) // SEG_ALIGN * SEG_ALIGN
    rows_e = jnp.sum(seg, axis=0)
    tiles_e = (rows_e + MOE_CHUNK + MOE_ROW_TILE - 1) // MOE_ROW_TILE
    tile_end = jnp.cumsum(tiles_e)
    tile_off = tile_end - tiles_e
    seg_off = (tile_off * MOE_ROW_TILE)[None, :] + jnp.cumsum(seg, axis=0) - seg
    tile = jnp.arange(n_tiles, dtype=jnp.int32)
    tile_expert = jnp.minimum(jnp.sum(tile[:, None] >= tile_end[None, :], axis=1), N_EXPERTS - 1)
    occupied = jnp.clip(rows_e[tile_expert] - (tile - tile_off[tile_expert]) * MOE_ROW_TILE, 0, MOE_ROW_TILE)
    as_i32 = lambda t: t.astype(jnp.int32)
    return as_i32(seg_off).reshape(-1), as_i32(tile_expert), as_i32(occupied)


def _chunk_copies(off_ref, cnt_ref, block, hbm_ref, vmem_ref, sems, to_hbm):
    slot = block % 2
    out = []
    for e in range(N_EXPERTS):
        c = cnt_ref[block * N_EXPERTS + e]
        off = pl.multiple_of(off_ref[block * N_EXPERTS + e], SEG_ALIGN)
        for k in range(ROW_TILE // MOE_CHUNK):
            idx = k * N_EXPERTS + e
            rows = hbm_ref.at[pl.ds(off + k * MOE_CHUNK, MOE_CHUNK)]
            staged = vmem_ref.at[slot, idx]
            src, dst = (staged, rows) if to_hbm else (rows, staged)
            out.append((c > k * MOE_CHUNK, e, k, idx, pltpu.make_async_copy(src, dst, sems.at[slot, idx])))
    return out


def _dispatch_kernel(off_ref, cnt_ref, x_ref, post_ref, xs_in_ref, xs_ref, stage_ref, sems):
    del xs_in_ref
    b, nblk = pl.program_id(0), pl.num_programs(0)
    copies_of = lambda blk: _chunk_copies(off_ref, cnt_ref, blk, xs_ref, stage_ref, sems, to_hbm=True)

    def wait_all(blk):
        for pred, _, _, _, copy in copies_of(blk):
            pl.when(pred)(copy.wait)

    xb = x_ref[...].astype(BF16)
    slot_row = lax.broadcasted_iota(jnp.int32, (MOE_CHUNK, ROW_TILE), 0).astype(F32)

    def gather_matrix(e, k):
        return jnp.where(post_ref[e:e + 1, :] == slot_row + float(k * MOE_CHUNK), 1.0, 0.0).astype(BF16)

    firsts = jnp.concatenate([gather_matrix(e, 0) for e in range(N_EXPERTS)], axis=0)
    stage_ref[b % 2, 0:N_EXPERTS] = jnp.dot(firsts, xb, preferred_element_type=F32).astype(BF16).reshape(
        N_EXPERTS, MOE_CHUNK, D_MODEL)
    copies = copies_of(b)
    for pred, e, k, idx, _ in copies:
        if k > 0:
            @pl.when(pred)
            def _(e=e, k=k, idx=idx):
                stage_ref[b % 2, idx] = jnp.dot(gather_matrix(e, k), xb, preferred_element_type=F32).astype(BF16)
    pl.when(b >= 1)(lambda: wait_all(b - 1))
    for pred, _, _, _, copy in copies:
        pl.when(pred)(copy.start)
    pl.when(b == nblk - 1)(lambda: wait_all(b))


def _dispatch(x, post, seg_off, cnt, zeros):
    n = x.shape[0]
    n_chunks = N_EXPERTS * (ROW_TILE // MOE_CHUNK)
    grid_spec = pltpu.PrefetchScalarGridSpec(
        num_scalar_prefetch=2, grid=(n // ROW_TILE,),
        in_specs=[pl.BlockSpec((ROW_TILE, D_MODEL), lambda i, *_: (i, 0)),
                  pl.BlockSpec((N_EXPERTS, ROW_TILE), lambda i, *_: (0, i)),
                  pl.BlockSpec(memory_space=pl.ANY)],
        out_specs=pl.BlockSpec(memory_space=pl.ANY),
        scratch_shapes=[pltpu.VMEM((2, n_chunks, MOE_CHUNK, D_MODEL), BF16),
                        pltpu.SemaphoreType.DMA((2, n_chunks))])
    return pl.pallas_call(
        _dispatch_kernel, grid_spec=grid_spec,
        out_shape=jax.ShapeDtypeStruct(zeros.shape, BF16),
        input_output_aliases={4: 0},
        compiler_params=_params("arbitrary"), name="moe_dispatch",
    )(seg_off, cnt.reshape(-1), x, post, zeros)


def _swiglu_partial(xb, wg, wu, wd):
    hg = jnp.dot(xb, wg, preferred_element_type=F32)
    hu = jnp.dot(xb, wu, preferred_element_type=F32)
    h = hg * _sigmoid(hg) * hu
    return jnp.dot(h.astype(BF16), wd, preferred_element_type=F32)


def _moe_ffn_kernel(expert_ref, occ_ref, x_ref, wg_ref, wu_ref, wd_ref, o_ref, acc_ref):
    del expert_ref
    i, j = pl.program_id(0), pl.program_id(1)
    occupied = occ_ref[i]

    @pl.when(j == 0)
    def _():
        acc_ref[...] = jnp.zeros_like(acc_ref)

    def weights():
        return wg_ref[...].astype(BF16), wu_ref[...].astype(BF16), wd_ref[...].astype(BF16)

    chunks = (occupied + MOE_CHUNK - 1) // MOE_CHUNK
    for n in range(1, MOE_ROW_TILE // MOE_CHUNK + 1):
        rows = slice(0, n * MOE_CHUNK)

        @pl.when(chunks == n)
        def _(rows=rows):
            acc_ref[rows, :] += _swiglu_partial(x_ref[rows, :], *weights())

    @pl.when(j == pl.num_programs(1) - 1)
    def _():
        o_ref[...] = acc_ref[...].astype(BF16)


def _moe_ffn(xs, tile_expert, occupied, w_gate, w_up, w_down):
    n_rows = xs.shape[0]
    d_ff = w_gate.shape[-1]
    row = pl.BlockSpec((MOE_ROW_TILE, D_MODEL), lambda i, j, *_: (i, 0))
    up = pl.BlockSpec((None, D_MODEL, MOE_FF_TILE), lambda i, j, ex, occ: (ex[i], 0, j))
    down = pl.BlockSpec((None, MOE_FF_TILE, D_MODEL), lambda i, j, ex, occ: (ex[i], j, 0))
    grid_spec = pltpu.PrefetchScalarGridSpec(
        num_scalar_prefetch=2, grid=(n_rows // MOE_ROW_TILE, d_ff // MOE_FF_TILE),
        in_specs=[row, up, up, down], out_specs=row,
        scratch_shapes=[pltpu.VMEM((MOE_ROW_TILE, D_MODEL), F32)])
    return pl.pallas_call(
        _moe_ffn_kernel, grid_spec=grid_spec,
        out_shape=jax.ShapeDtypeStruct((n_rows, D_MODEL), BF16),
        compiler_params=_params("parallel", "arbitrary"), name="moe_swiglu",
    )(tile_expert, occupied, xs, w_gate, w_up, w_down)


def _ffn_kernel(x_ref, wg_ref, wu_ref, wd_ref, g_ref, b_ref, o_ref):
    x = x_ref[...]
    y = _swiglu_partial(x.astype(BF16), wg_ref[...], wu_ref[...], wd_ref[...])
    o_ref[...] = _layer_norm(DEEPNORM_ALPHA * x + y, g_ref[...], b_ref[...])


def _ffn(x, w_gate, w_up, w_down, g, b):
    n = x.shape[0]
    row = pl.BlockSpec((PROJ_ROW_TILE, D_MODEL), lambda i: (i, 0))
    const = lambda a: pl.BlockSpec(a.shape, lambda i: (0,) * a.ndim, pipeline_mode=pl.Buffered(1))
    g2, b2 = g.reshape(1, -1), b.reshape(1, -1)
    return pl.pallas_call(
        _ffn_kernel, grid=(n // PROJ_ROW_TILE,),
        in_specs=[row, const(w_gate), const(w_up), const(w_down), const(g2), const(b2)], out_specs=row,
        out_shape=jax.ShapeDtypeStruct((n, D_MODEL), F32),
        compiler_params=_params("parallel"), name="swiglu_ln2",
    )(x, w_gate, w_up, w_down, g2, b2)


def _combine_kernel(off_ref, cnt_ref, x_ref, cw_ref, pos_ref, g_ref, b_ref, ys_ref, o_ref,
                    buf_ref, acc_ref, sems):
    b, nblk = pl.program_id(0), pl.num_programs(0)
    copies_of = lambda blk: _chunk_copies(off_ref, cnt_ref, blk, ys_ref, buf_ref, sems, to_hbm=False)

    def start_all(blk):
        for pred, _, k, _, copy in copies_of(blk):
            if k == 0:
                copy.start()
            else:
                pl.when(pred)(copy.start)

    pl.when(b == 0)(lambda: start_all(b))
    pl.when(b + 1 < nblk)(lambda: start_all(b + 1))

    lane = lax.broadcasted_iota(jnp.int32, (ROW_TILE, LANES), 1)
    slot_col = lax.broadcasted_iota(jnp.int32, (ROW_TILE, MOE_CHUNK), 1).astype(F32)
    column = lambda ref, e: jnp.sum(jnp.where(lane == e, ref[...], 0.0), axis=-1, keepdims=True)
    weight = [column(cw_ref, e) for e in range(N_EXPERTS)]
    pos = [column(pos_ref, e) for e in range(N_EXPERTS)]

    def scatter_matrix(e, k):
        return jnp.where(pos[e] == slot_col + float(k * MOE_CHUNK), weight[e], 0.0).astype(BF16)

    first = jnp.concatenate([scatter_matrix(e, 0) for e in range(N_EXPERTS)], axis=1)
    copies = copies_of(b)
    for _, _, k, _, copy in copies:
        if k == 0:
            copy.wait()
    firsts = buf_ref[b % 2, 0:N_EXPERTS].reshape(N_EXPERTS * MOE_CHUNK, D_MODEL)
    acc_ref[...] = jnp.dot(first, firsts, preferred_element_type=F32)
    for pred, e, k, idx, copy in copies:
        if k > 0:
            @pl.when(pred)
            def _(e=e, k=k, idx=idx, copy=copy):
                copy.wait()
                acc_ref[...] += jnp.dot(scatter_matrix(e, k), buf_ref[b % 2, idx], preferred_element_type=F32)

    o_ref[...] = _layer_norm(DEEPNORM_ALPHA * x_ref[...] + acc_ref[...], g_ref[...], b_ref[...])


def _combine(x, ys, cw, pos, seg_off, cnt, g, b):
    n = x.shape[0]
    n_chunks = N_EXPERTS * (ROW_TILE // MOE_CHUNK)
    tok = lambda w: pl.BlockSpec((ROW_TILE, w), lambda i, *_: (i, 0))
    vec = pl.BlockSpec((1, D_MODEL), lambda i, *_: (0, 0))
    grid_spec = pltpu.PrefetchScalarGridSpec(
        num_scalar_prefetch=2, grid=(n // ROW_TILE,),
        in_specs=[tok(D_MODEL), tok(LANES), tok(LANES), vec, vec, pl.BlockSpec(memory_space=pl.ANY)],
        out_specs=tok(D_MODEL),
        scratch_shapes=[pltpu.VMEM((2, n_chunks, MOE_CHUNK, D_MODEL), BF16),
                        pltpu.VMEM((ROW_TILE, D_MODEL), F32),
                        pltpu.SemaphoreType.DMA((2, n_chunks))])
    return pl.pallas_call(
        _combine_kernel, grid_spec=grid_spec,
        out_shape=jax.ShapeDtypeStruct((n, D_MODEL), F32),
        compiler_params=_params("arbitrary"), name="moe_combine_ln2",
    )(seg_off, cnt.reshape(-1), x, cw, pos, g.reshape(1, -1), b.reshape(1, -1), ys)


def _moe(x, w_router, w_gate, w_up, w_down, g, b):
    n = x.shape[0]
    nblk = n // ROW_TILE
    max_rows = 2 * n + nblk * N_EXPERTS * (SEG_ALIGN - 1) + N_EXPERTS * (MOE_CHUNK + MOE_ROW_TILE - 1)
    n_tiles = max_rows // MOE_ROW_TILE
    cw, pos, post, cnt, zeros = _router(x, w_router, n_tiles * MOE_ROW_TILE)
    seg_off, tile_expert, occupied = _moe_layout(cnt, n_tiles)
    xs = _dispatch(x, post, seg_off, cnt, zeros)
    ys = _moe_ffn(xs, tile_expert, occupied, w_gate, w_up, w_down)
    return _combine(x, ys, cw, pos, seg_off, cnt, g, b)


def kernel(x, emb_ln_g, emb_ln_b, w_in, b_gate, na_rpb, sw_sink, w_branch_na, w_branch_sw, w_out,
           ln1_g, ln1_b, ffn_w_gate, ffn_w_up, ffn_w_down, moe_router, moe_w_gate, moe_w_up,
           moe_w_down, ln2_g, ln2_b):
    batch, seq, d = x.shape
    assert (seq, d) == (SEQ, D_MODEL)
    n = batch * seq
    tables = _rotary_tables()
    h = x.reshape(n, d)
    for layer in range(DEPTH):
        h, q_na, k_na, v_na, q_sw, k_sw, v_sw, gates = _proj(
            h, w_in[layer].astype(BF16), b_gate[layer], tables,
            embed_ln=(emb_ln_g, emb_ln_b) if layer == 0 else None)
        seq3 = lambda t: t.reshape(batch, seq, t.shape[-1])
        y_na = _na_attention(seq3(q_na), seq3(k_na), seq3(v_na), _na_bias_tables(na_rpb[layer]))
        y_sw = _sw_attention(seq3(q_sw), seq3(k_sw), seq3(v_sw), sw_sink[layer])
        h = _merge(h, y_na.reshape(n, -1), y_sw.reshape(n, -1), gates,
                   w_branch_na[layer].astype(BF16), w_branch_sw[layer].astype(BF16),
                   w_out[layer].astype(BF16), ln1_g[layer], ln1_b[layer])
        i = layer // 2
        if layer % 2 == 0:
            h = _ffn(h, ffn_w_gate[i].astype(BF16), ffn_w_up[i].astype(BF16),
                     ffn_w_down[i].astype(BF16), ln2_g[layer], ln2_b[layer])
        else:
            h = _moe(h, moe_router[i], moe_w_gate[i], moe_w_up[i], moe_w_down[i],
                     ln2_g[layer], ln2_b[layer])
    return h.reshape(batch, seq, d)
```

```python
import functools

import numpy as np
import jax
import jax.numpy as jnp
from jax import lax
from jax.experimental import pallas as pl
from jax.experimental.pallas import tpu as pltpu

F32 = jnp.float32
BF16 = jnp.bfloat16

D_MODEL = 1024
SEQ = 2048
DEPTH = 2
HEAD_DIM = 64
NA_HEADS = 8
NA_WIDTH = NA_HEADS * HEAD_DIM
GRID_W = 64
GRID_ROWS = SEQ // GRID_W
NA_KH = 8
NA_KW = 16
SW_HEADS = 8
SW_KV_HEADS = 2
SW_GROUP = SW_HEADS // SW_KV_HEADS
SW_WIDTH = SW_HEADS * HEAD_DIM
SW_KV_WIDTH = SW_KV_HEADS * HEAD_DIM
SW_WINDOW = 128
SW_BLOCK = 128
ROT_DIM = HEAD_DIM // 4
ROPE_THETA = 500000.0
OFF_QNA = NA_WIDTH
OFF_KNA = 2 * NA_WIDTH
OFF_VNA = 3 * NA_WIDTH
OFF_QSW = OFF_VNA + SW_WIDTH
OFF_KSW = OFF_QSW + SW_KV_WIDTH
OFF_VSW = OFF_KSW + SW_KV_WIDTH
PROJ_COLS = OFF_VSW + 2 * D_MODEL
N_EXPERTS = 8
DEEPNORM_ALPHA = (2 * DEPTH) ** 0.25
LN_EPS = 1e-5
NEG_INF = -1e30
QK_SCALE = HEAD_DIM ** -0.5
LOG2E = float(np.log2(np.e))

LANES = 128
MXU_DIM = 256
V7X_VMEM_BYTES = 64 * 1024 * 1024
VMEM_LIMIT = V7X_VMEM_BYTES * 7 // 8

PROJ_ROW_TILE = 1024
ROW_TILE = 512
MOE_ROW_TILE = 1536
MOE_FF_TILE = 512
MOE_CHUNK = 256
SEG_ALIGN = 16
NA_ROWS_PER_STEP = 8
SW_BLOCKS_PER_STEP = 8
MERGE_PARTS = 4
ROUTER_BLOCKS = 4


def _layer_norm(z, g, b):
    mu = jnp.mean(z, axis=-1, keepdims=True)
    d = z - mu
    var = jnp.mean(d * d, axis=-1, keepdims=True)
    return d * lax.rsqrt(var + LN_EPS) * g + b


def _sigmoid(z):
    return 1.0 / (1.0 + jnp.exp(-z))


def _params(*sem):
    return pltpu.CompilerParams(dimension_semantics=sem, vmem_limit_bytes=VMEM_LIMIT)


def _rotary_tables():
    half = ROT_DIM // 2
    inv_freq = 1.0 / (ROPE_THETA ** (jnp.arange(0, ROT_DIM, 2, dtype=F32) / ROT_DIM))
    ang = jnp.arange(SEQ, dtype=jnp.int32).astype(F32)[:, None] * inv_freq[None, :]
    cos, sin = jnp.cos(ang), jnp.sin(ang)
    ones = jnp.ones((SEQ, HEAD_DIM - ROT_DIM), F32)
    zeros = jnp.zeros((SEQ, HEAD_DIM - ROT_DIM), F32)
    zh = jnp.zeros((SEQ, half), F32)
    cos_h = jnp.concatenate([cos, cos, ones], axis=1)
    sa_h = jnp.concatenate([-sin, zh, zeros], axis=1)
    sb_h = jnp.concatenate([zh, sin, zeros], axis=1)
    two = lambda t: jnp.concatenate([t, t], axis=1)
    return two(cos_h), two(sa_h), two(sb_h)


def _proj_kernel(*refs, embed_ln):
    if embed_ln:
        x_ref, g_ref, b_ref, w_ref, bg_ref, cos_ref, sa_ref, sb_ref, xn_ref, *outs = refs
        x = _layer_norm(x_ref[...], g_ref[...], b_ref[...])
        xn_ref[...] = x
    else:
        x_ref, w_ref, bg_ref, cos_ref, sa_ref, sb_ref, *outs = refs
        x = x_ref[...]
    qna_ref, kna_ref, vna_ref, qsw_ref, ksw_ref, vsw_ref, gate_ref = outs
    xb = x.astype(BF16)

    def mm(lo, hi):
        return jnp.dot(xb, w_ref[:, lo:hi], preferred_element_type=F32)

    qna_ref[...] = (mm(0, OFF_QNA) * (QK_SCALE * LOG2E)).astype(BF16)
    kna_ref[...] = mm(OFF_QNA, OFF_KNA).astype(BF16)
    vna_ref[...] = mm(OFF_KNA, OFF_VNA).astype(BF16)

    cos, sa, sb = cos_ref[...], sa_ref[...], sb_ref[...]
    half = ROT_DIM // 2

    def rot(t):
        return t * cos + pltpu.roll(t, LANES - half, 1) * sa + pltpu.roll(t, half, 1) * sb

    q = mm(OFF_VNA, OFF_QSW)
    for c in range(SW_WIDTH // LANES):
        sl = slice(c * LANES, (c + 1) * LANES)
        qsw_ref[:, sl] = (rot(q[:, sl]) * (QK_SCALE * LOG2E)).astype(BF16)

    kv = mm(OFF_QSW, OFF_VSW)
    k = rot(kv[:, :SW_KV_WIDTH])
    v = kv[:, SW_KV_WIDTH:]
    first = lax.broadcasted_iota(jnp.int32, k.shape, 1) < HEAD_DIM

    def dup(t):
        r = pltpu.roll(t, HEAD_DIM, 1)
        return jnp.concatenate([jnp.where(first, t, r), jnp.where(first, r, t)], axis=1)

    ksw_ref[...] = dup(k).astype(BF16)
    vsw_ref[...] = v.astype(BF16)

    gw = 512
    for c in range(2 * D_MODEL // gw):
        z = mm(OFF_VSW + c * gw, OFF_VSW + (c + 1) * gw) + bg_ref[:, c * gw:(c + 1) * gw]
        gate_ref[:, c * gw:(c + 1) * gw] = _sigmoid(z).astype(BF16)


def _proj(x, w_bf, b_gate, tables, embed_ln=None):
    n = x.shape[0]
    tiles_per_seq = SEQ // PROJ_ROW_TILE
    row = lambda w: pl.BlockSpec((PROJ_ROW_TILE, w), lambda i: (i, 0))
    full = lambda a: pl.BlockSpec(a.shape, lambda i: (0,) * a.ndim)
    tab = pl.BlockSpec((PROJ_ROW_TILE, LANES), lambda i: (i % tiles_per_seq, 0))
    bg = b_gate.reshape(1, -1)
    widths = (NA_WIDTH, NA_WIDTH, NA_WIDTH, SW_WIDTH, 2 * SW_KV_WIDTH, SW_KV_WIDTH, 2 * D_MODEL)
    ln = [t.reshape(1, -1) for t in embed_ln] if embed_ln else []
    out_specs = [row(w) for w in widths]
    out_shape = [jax.ShapeDtypeStruct((n, w), BF16) for w in widths]
    if embed_ln:
        out_specs = [row(D_MODEL)] + out_specs
        out_shape = [jax.ShapeDtypeStruct((n, D_MODEL), F32)] + out_shape
    outs = pl.pallas_call(
        functools.partial(_proj_kernel, embed_ln=bool(embed_ln)), grid=(n // PROJ_ROW_TILE,),
        in_specs=[row(D_MODEL)] + [full(t) for t in ln] + [full(w_bf), full(bg), tab, tab, tab],
        out_specs=out_specs, out_shape=out_shape,
        compiler_params=_params("parallel"), name="in_proj",
    )(x, *ln, w_bf, bg, *tables)
    return outs if embed_ln else [x] + list(outs)


def _na_bias_tables(rpb):
    c = np.arange(GRID_W)
    qcs = np.clip(c - NA_KW // 2, 0, GRID_W - NA_KW)
    valid = (c[None, :] >= qcs[:, None]) & (c[None, :] < qcs[:, None] + NA_KW)
    n_dr, n_dc = 2 * NA_KH - 1, 2 * NA_KW - 1
    span = 2 * GRID_W - 1
    lead = GRID_W - NA_KW
    ext = jnp.pad(rpb.astype(F32), ((0, 0), (0, 0), (lead, span + 1 - lead - n_dc)))
    flat = jnp.broadcast_to(ext[:, :, None, :], (NA_HEADS, n_dr, GRID_W, span + 1))
    flat = flat.reshape(NA_HEADS, n_dr, GRID_W * (span + 1))[:, :, :GRID_W * span]
    t = flat.reshape(NA_HEADS, n_dr, GRID_W, span)[:, :, :, GRID_W - 1:]
    t = jnp.where(valid[None, None], t * LOG2E, NEG_INF)
    pairs = jnp.concatenate([t[:, :-1], t[:, 1:]], axis=-1)
    pairs = pairs.reshape(2, 4, n_dr - 1, GRID_W, 2 * GRID_W).transpose(0, 2, 1, 3, 4)
    return pairs.reshape(2, n_dr - 1, 4 * GRID_W, 2 * GRID_W)


def _na_kernel(q_ref, k_ref, v_ref, bias_ref, o_ref):
    lane = lax.broadcasted_iota(jnp.int32, (GRID_W, MXU_DIM), 1)
    head_mask = [(lane >= h * HEAD_DIM) & (lane < (h + 1) * HEAD_DIM) for h in range(4)]
    nk = NA_KH * GRID_W

    def rows(it, carry):
        chains = []
        for u in range(NA_ROWS_PER_STEP):
            r = it * NA_ROWS_PER_STEP + u
            rs = jnp.clip(r - NA_KH // 2, 0, GRID_ROWS - NA_KH)
            variant = rs - r + (NA_KH - 1)
            qrows = pl.ds(pl.multiple_of(r * GRID_W, GRID_W), GRID_W)
            krows = pl.ds(pl.multiple_of(rs * GRID_W, GRID_W), nk)
            q = q_ref[qrows, :]
            for g in range(2):
                gs = slice(g * MXU_DIM, (g + 1) * MXU_DIM)
                qg = q[:, gs]
                lhs = jnp.concatenate([jnp.where(m, qg, jnp.zeros_like(qg)) for m in head_mask], axis=0)
                bias = jnp.concatenate([bias_ref[g, variant + 2 * a] for a in range(NA_KH // 2)], axis=1)
                chains.append(dict(lhs=lhs, k=k_ref[krows, gs], v=v_ref[krows, gs], bias=bias,
                                   out=(qrows, gs)))
        for c in chains:
            c["s"] = lax.dot_general(c["lhs"], c["k"], (((1,), (1,)), ((), ())),
                                     preferred_element_type=F32) + c["bias"]
        for c in chains:
            c["p"] = jnp.exp2(c["s"] - jnp.max(c["s"], axis=-1, keepdims=True))
        for c in chains:
            c["l"] = jnp.sum(c["p"], axis=-1, keepdims=True)
        for c in chains:
            c["o"] = jnp.dot(c["p"].astype(BF16), c["v"], preferred_element_type=F32) / c["l"]
        for c in chains:
            og = jnp.zeros((GRID_W, MXU_DIM), F32)
            for h in range(4):
                og = og + jnp.where(head_mask[h], c["o"][h * GRID_W:(h + 1) * GRID_W], 0.0)
            o_ref[c["out"]] = og.astype(BF16)
        return carry

    lax.fori_loop(0, GRID_ROWS // NA_ROWS_PER_STEP, rows, 0)


def _na_attention(q, k, v, bias):
    b = q.shape[0]
    seq = pl.BlockSpec((None, SEQ, NA_WIDTH), lambda i: (i, 0, 0))
    return pl.pallas_call(
        _na_kernel, grid=(b,),
        in_specs=[seq, seq, seq, pl.BlockSpec(bias.shape, lambda i: (0, 0, 0, 0))],
        out_specs=seq, out_shape=jax.ShapeDtypeStruct((b, SEQ, NA_WIDTH), BF16),
        compiler_params=_params("parallel"), name="na_attention",
    )(q, k, v, bias)


def _sw_masks():
    nkeys = 3 * SW_BLOCK
    jk = np.arange(nkeys)[:, None]
    iq = np.arange(SW_BLOCK)[None, :]
    band = (jk >= iq) & (jk <= iq + 2 * SW_WINDOW)
    in_seq = [(jk >= SW_BLOCK), np.ones_like(band), (jk < 2 * SW_BLOCK)]
    return jnp.asarray(np.stack([np.where(band & ok, 0.0, NEG_INF) for ok in in_seq]), F32)


def _sw_kernel(sink_ref, q_ref, k_ref, v_ref, mask_ref, o_ref):
    nb = SEQ // SW_BLOCK
    first = lax.broadcasted_iota(jnp.int32, (SW_BLOCK, LANES), 1) < HEAD_DIM
    ones = jnp.ones((8, 3 * SW_BLOCK), BF16)

    def blocks(it, carry):
        chains = []
        for u in range(SW_BLOCKS_PER_STEP):
            n = it * SW_BLOCKS_PER_STEP + u
            mask = mask_ref[jnp.where(n == 0, 0, jnp.where(n == nb - 1, 2, 1))]
            starts = [pl.multiple_of(jnp.clip(n + d, 0, nb - 1) * SW_BLOCK, SW_BLOCK) for d in (-1, 0, 1)]
            qrows = pl.ds(pl.multiple_of(n * SW_BLOCK, SW_BLOCK), SW_BLOCK)
            q = q_ref[qrows, :]
            vv = jnp.concatenate([v_ref[pl.ds(s, SW_BLOCK), :] for s in starts], axis=0)
            vt = vv.astype(F32).T.astype(BF16)
            for kvh in range(SW_KV_HEADS):
                ks = slice(kvh * LANES, (kvh + 1) * LANES)
                kk = jnp.concatenate([k_ref[pl.ds(s, SW_BLOCK), ks] for s in starts], axis=0)
                parts, sinks = [], []
                for j in range(SW_GROUP):
                    h = kvh * SW_GROUP + j
                    qc = q[:, (h // 2) * LANES:(h // 2 + 1) * LANES]
                    keep = first if h % 2 == 0 else jnp.logical_not(first)
                    parts.append(jnp.where(keep, qc, jnp.zeros_like(qc)))
                    sinks.append(jnp.full((1, SW_BLOCK), sink_ref[h] * LOG2E, F32))
                vt_ones = jnp.concatenate([vt[kvh * HEAD_DIM:(kvh + 1) * HEAD_DIM], ones], axis=0)
                chains.append(dict(k=kk, vt=vt_ones, q=jnp.concatenate(parts, axis=0),
                                   sink=jnp.concatenate(sinks, axis=1), mask=mask, qrows=qrows, kvh=kvh))
        for c in chains:
            s = lax.dot_general(c["k"], c["q"], (((1,), (1,)), ((), ())), preferred_element_type=F32)
            prev_mask, next_mask = c["mask"][:SW_BLOCK], c["mask"][2 * SW_BLOCK:]
            cols = []
            for j in range(SW_GROUP):
                sj = s[:, j * SW_BLOCK:(j + 1) * SW_BLOCK]
                cols.append(jnp.concatenate([sj[:SW_BLOCK] + prev_mask, sj[SW_BLOCK:2 * SW_BLOCK],
                                             sj[2 * SW_BLOCK:] + next_mask], axis=0))
            c["s"] = jnp.concatenate(cols, axis=1)
        for c in chains:
            c["m"] = jnp.maximum(jnp.max(c["s"], axis=0, keepdims=True), c["sink"])
        for c in chains:
            c["p"] = jnp.exp2(c["s"] - c["m"]).astype(BF16)
        for c in chains:
            o = jnp.dot(c["vt"], c["p"], preferred_element_type=F32)
            denom = o[HEAD_DIM:HEAD_DIM + 1] + jnp.exp2(c["sink"] - c["m"])
            c["o"] = o[:HEAD_DIM] / denom
        for c in chains:
            for pair in range(SW_GROUP // 2):
                even = c["o"][:, (2 * pair) * SW_BLOCK:(2 * pair + 1) * SW_BLOCK]
                odd = c["o"][:, (2 * pair + 1) * SW_BLOCK:(2 * pair + 2) * SW_BLOCK]
                col = (c["kvh"] * (SW_GROUP // 2) + pair) * LANES
                o_ref[c["qrows"], col:col + LANES] = jnp.concatenate([even, odd], axis=0).T.astype(BF16)
        return carry

    lax.fori_loop(0, nb // SW_BLOCKS_PER_STEP, blocks, 0)


def _sw_attention(q, k2, v, sink):
    b = q.shape[0]
    masks = _sw_masks()
    seq = lambda w: pl.BlockSpec((None, SEQ, w), lambda i: (i, 0, 0))
    return pl.pallas_call(
        _sw_kernel, grid=(b,),
        in_specs=[pl.BlockSpec(memory_space=pltpu.SMEM), seq(SW_WIDTH), seq(2 * SW_KV_WIDTH),
                  seq(SW_KV_WIDTH), pl.BlockSpec(masks.shape, lambda i: (0, 0, 0))],
        out_specs=seq(SW_WIDTH), out_shape=jax.ShapeDtypeStruct((b, SEQ, SW_WIDTH), BF16),
        compiler_params=_params("parallel"), name="sw_attention",
    )(sink.astype(F32), q, k2, v, masks)


def _merge_kernel(x_ref, yna_ref, ysw_ref, gate_ref, wna_ref, wsw_ref, wout_ref, g_ref, b_ref, o_ref):
    part = x_ref.shape[0] // MERGE_PARTS
    parts = [slice(p * part, (p + 1) * part) for p in range(MERGE_PARTS)]
    dot = functools.partial(jnp.dot, preferred_element_type=F32)
    branches = [(dot(yna_ref[r, :], wna_ref[...]), dot(ysw_ref[r, :], wsw_ref[...])) for r in parts]
    mixed = [(gate_ref[r, :D_MODEL].astype(F32) * a + gate_ref[r, D_MODEL:].astype(F32) * s).astype(BF16)
             for r, (a, s) in zip(parts, branches)]
    summed = [DEEPNORM_ALPHA * x_ref[r, :] + dot(m, wout_ref[...]) for r, m in zip(parts, mixed)]
    for r, z in zip(parts, summed):
        o_ref[r, :] = _layer_norm(z, g_ref[...], b_ref[...])


def _merge(x, y_na, y_sw, gates, w_na, w_sw, w_out, g, b):
    n = x.shape[0]
    row = lambda w: pl.BlockSpec((PROJ_ROW_TILE, w), lambda i: (i, 0))
    full = lambda a: pl.BlockSpec(a.shape, lambda i: (0,) * a.ndim)
    g2, b2 = g.reshape(1, -1), b.reshape(1, -1)
    return pl.pallas_call(
        _merge_kernel, grid=(n // PROJ_ROW_TILE,),
        in_specs=[row(D_MODEL), row(NA_WIDTH), row(SW_WIDTH), row(2 * D_MODEL),
                  full(w_na), full(w_sw), full(w_out), full(g2), full(b2)],
        out_specs=row(D_MODEL), out_shape=jax.ShapeDtypeStruct((n, D_MODEL), F32),
        compiler_params=_params("parallel"), name="merge_ln1",
    )(x, y_na, y_sw, gates, w_na, w_sw, w_out, g2, b2)


def _router_kernel(x_ref, w_ref, cw_ref, pos_ref, post_ref, cnt_ref, zero_ref):
    zero_ref[...] = jnp.zeros_like(zero_ref)

    blocks = [slice(p * ROW_TILE, (p + 1) * ROW_TILE) for p in range(ROUTER_BLOCKS)]
    dot = functools.partial(jnp.dot, preferred_element_type=F32)
    w = w_ref[...]
    wh = w.astype(BF16)
    wl = (w - wh.astype(F32)).astype(BF16)
    xs = [x_ref[r, :] for r in blocks]
    xh = [x.astype(BF16) for x in xs]
    xl = [(x - h.astype(F32)).astype(BF16) for x, h in zip(xs, xh)]
    logits = [dot(h, wh) + dot(l, wh) + dot(h, wl) for h, l in zip(xh, xl)]
    lane = lax.broadcasted_iota(jnp.int32, (ROW_TILE, LANES), 1).astype(F32)
    lg = [jnp.where(lane < N_EXPERTS, t, -jnp.inf) for t in logits]
    m1 = [jnp.max(t, axis=-1, keepdims=True) for t in lg]
    i1 = [jnp.min(jnp.where(t == m, lane, float(LANES)), axis=-1, keepdims=True) for t, m in zip(lg, m1)]
    lg2 = [jnp.where(lane == i, -jnp.inf, t) for t, i in zip(lg, i1)]
    m2 = [jnp.max(t, axis=-1, keepdims=True) for t in lg2]
    i2 = [jnp.min(jnp.where(t == m, lane, float(LANES)), axis=-1, keepdims=True) for t, m in zip(lg2, m2)]
    e2 = [jnp.exp(b - a) for a, b in zip(m1, m2)]
    cw = [jnp.where(lane == a, 1.0 / (1.0 + e), 0.0) + jnp.where(lane == b, e / (1.0 + e), 0.0)
          for a, b, e in zip(i1, i2, e2)]

    t = ROW_TILE
    tri = jnp.where(lax.broadcasted_iota(jnp.int32, (t, t), 0) <= lax.broadcasted_iota(jnp.int32, (t, t), 1),
                    1.0, 0.0).astype(BF16)
    sel = [jnp.where(c.T[:N_EXPERTS] > 0.0, 1.0, 0.0) for c in cw]
    incl = [dot(s.astype(BF16), tri) for s in sel]
    post = [jnp.where(s > 0.0, n - 1.0, -1.0) for s, n in zip(sel, incl)]
    unrouted = jnp.full((LANES - N_EXPERTS, t), -1.0, F32)
    for p, r in enumerate(blocks):
        cw_ref[r, :] = cw[p]
        post_ref[:, r] = post[p]
        pos_ref[r, :] = jnp.concatenate([post[p], unrouted], axis=0).T
        cnt_ref[p * N_EXPERTS:(p + 1) * N_EXPERTS, :] = jnp.broadcast_to(incl[p][:, t - 1:t], (N_EXPERTS, LANES))


def _router(x, w_router, n_rows):
    n = x.shape[0]
    nblk = n // ROW_TILE
    step = ROUTER_BLOCKS * ROW_TILE
    steps = n // step
    assert n_rows % (steps * SEG_ALIGN) == 0
    w = jnp.zeros((D_MODEL, LANES), F32).at[:, :N_EXPERTS].set(w_router)
    tok = pl.BlockSpec((step, LANES), lambda i: (i, 0))
    cw, pos, post, cnt, zeros = pl.pallas_call(
        _router_kernel, grid=(steps,),
        in_specs=[pl.BlockSpec((step, D_MODEL), lambda i: (i, 0)),
                  pl.BlockSpec((D_MODEL, LANES), lambda i: (0, 0))],
        out_specs=[tok, tok, pl.BlockSpec((N_EXPERTS, step), lambda i: (0, i)),
                   pl.BlockSpec((ROUTER_BLOCKS * N_EXPERTS, LANES), lambda i: (i, 0)),
                   pl.BlockSpec((n_rows // steps, D_MODEL), lambda i: (i, 0))],
        out_shape=[jax.ShapeDtypeStruct((n, LANES), F32), jax.ShapeDtypeStruct((n, LANES), F32),
                   jax.ShapeDtypeStruct((N_EXPERTS, n), F32),
                   jax.ShapeDtypeStruct((nblk * N_EXPERTS, LANES), F32),
                   jax.ShapeDtypeStruct((n_rows, D_MODEL), BF16)],
        compiler_params=_params("parallel"), name="router_top2",
    )(x, w)
    return cw, pos, post, cnt[:, 0].astype(jnp.int32).reshape(nblk, N_EXPERTS), zeros


def _moe_layout(cnt, n_tiles):
    seg = (cnt + SEG_ALIGN - 1) // SEG_ALIGN * SEG_ALIGN
    rows_e = jnp.sum(seg, axis=0)
    tiles_e = (rows_e + MOE_CHUNK + MOE_ROW_TILE - 1) // MOE_ROW_TILE
    tile_end = jnp.cumsum(tiles_e)
    tile_off = tile_end - tiles_e
    seg_off = (tile_off * MOE_ROW_TILE)[None, :] + jnp.cumsum(seg, axis=0) - seg
    tile = jnp.arange(n_tiles, dtype=jnp.int32)
    tile_expert = jnp.minimum(jnp.sum(tile[:, None] >= tile_end[None, :], axis=1), N_EXPERTS - 1)
    occupied = jnp.clip(rows_e[tile_expert] - (tile - tile_off[tile_expert]) * MOE_ROW_TILE, 0, MOE_ROW_TILE)
    as_i32 = lambda t: t.astype(jnp.int32)
    return as_i32(seg_off).reshape(-1), as_i32(tile_expert), as_i32(occupied)


def _chunk_copies(off_ref, cnt_ref, block, hbm_ref, vmem_ref, sems, to_hbm):
    slot = block % 2
    out = []
    for e in range(N_EXPERTS):
        c = cnt_ref[block * N_EXPERTS + e]
        off = pl.multiple_of(off_ref[block * N_EXPERTS + e], SEG_ALIGN)
        for k in range(ROW_TILE // MOE_CHUNK):
            idx = k * N_EXPERTS + e
            rows = hbm_ref.at[pl.ds(off + k * MOE_CHUNK, MOE_CHUNK)]
            staged = vmem_ref.at[slot, idx]
            src, dst = (staged, rows) if to_hbm else (rows, staged)
            out.append((c > k * MOE_CHUNK, e, k, idx, pltpu.make_async_copy(src, dst, sems.at[slot, idx])))
    return out


def _dispatch_kernel(off_ref, cnt_ref, x_ref, post_ref, xs_in_ref, xs_ref, stage_ref, sems):
    del xs_in_ref
    b, nblk = pl.program_id(0), pl.num_programs(0)
    copies_of = lambda blk: _chunk_copies(off_ref, cnt_ref, blk, xs_ref, stage_ref, sems, to_hbm=True)

    def wait_all(blk):
        for pred, _, _, _, copy in copies_of(blk):
            pl.when(pred)(copy.wait)

    xb = x_ref[...].astype(BF16)
    slot_row = lax.broadcasted_iota(jnp.int32, (MOE_CHUNK, ROW_TILE), 0).astype(F32)

    def gather_matrix(e, k):
        return jnp.where(post_ref[e:e + 1, :] == slot_row + float(k * MOE_CHUNK), 1.0, 0.0).astype(BF16)

    firsts = jnp.concatenate([gather_matrix(e, 0) for e in range(N_EXPERTS)], axis=0)
    stage_ref[b % 2, 0:N_EXPERTS] = jnp.dot(firsts, xb, preferred_element_type=F32).astype(BF16).reshape(
        N_EXPERTS, MOE_CHUNK, D_MODEL)
    copies = copies_of(b)
    for pred, e, k, idx, _ in copies:
        if k > 0:
            @pl.when(pred)
            def _(e=e, k=k, idx=idx):
                stage_ref[b % 2, idx] = jnp.dot(gather_matrix(e, k), xb, preferred_element_type=F32).astype(BF16)
    pl.when(b >= 1)(lambda: wait_all(b - 1))
    for pred, _, _, _, copy in copies:
        pl.when(pred)(copy.start)
    pl.when(b == nblk - 1)(lambda: wait_all(b))


def _dispatch(x, post, seg_off, cnt, zeros):
    n = x.shape[0]
    n_chunks = N_EXPERTS * (ROW_TILE // MOE_CHUNK)
    grid_spec = pltpu.PrefetchScalarGridSpec(
        num_scalar_prefetch=2, grid=(n // ROW_TILE,),
        in_specs=[pl.BlockSpec((ROW_TILE, D_MODEL), lambda i, *_: (i, 0)),
                  pl.BlockSpec((N_EXPERTS, ROW_TILE), lambda i, *_: (0, i)),
                  pl.BlockSpec(memory_space=pl.ANY)],
        out_specs=pl.BlockSpec(memory_space=pl.ANY),
        scratch_shapes=[pltpu.VMEM((2, n_chunks, MOE_CHUNK, D_MODEL), BF16),
                        pltpu.SemaphoreType.DMA((2, n_chunks))])
    return pl.pallas_call(
        _dispatch_kernel, grid_spec=grid_spec,
        out_shape=jax.ShapeDtypeStruct(zeros.shape, BF16),
        input_output_aliases={4: 0},
        compiler_params=_params("arbitrary"), name="moe_dispatch",
    )(seg_off, cnt.reshape(-1), x, post, zeros)


def _swiglu_partial(xb, wg, wu, wd):
    hg = jnp.dot(xb, wg, preferred_element_type=F32)
    hu = jnp.dot(xb, wu, preferred_element_type=F32)
    h = hg * _sigmoid(hg) * hu
    return jnp.dot(h.astype(BF16), wd, preferred_element_type=F32)


def _moe_ffn_kernel(expert_ref, occ_ref, x_ref, wg_ref, wu_ref, wd_ref, o_ref, acc_ref):
    del expert_ref
    i, j = pl.program_id(0), pl.program_id(1)
    occupied = occ_ref[i]

    def for_each_chunk(body):
        def step(t, carry):
            body(pl.ds(pl.multiple_of(t * MOE_CHUNK, MOE_CHUNK), MOE_CHUNK))
            return carry
        lax.fori_loop(0, MOE_ROW_TILE // MOE_CHUNK, step, 0)

    @pl.when(j == 0)
    def _():
        def zero(rows):
            acc_ref[rows, :] = jnp.zeros((MOE_CHUNK, D_MODEL), F32)
        for_each_chunk(zero)

    def weights():
        return wg_ref[...].astype(BF16), wu_ref[...].astype(BF16), wd_ref[...].astype(BF16)

    chunks = (occupied + MOE_CHUNK - 1) // MOE_CHUNK
    for n in range(1, MOE_ROW_TILE // MOE_CHUNK + 1):
        rows = slice(0, n * MOE_CHUNK)

        @pl.when(chunks == n)
        def _(rows=rows):
            acc_ref[rows, :] += _swiglu_partial(x_ref[rows, :], *weights())

    @pl.when(j == pl.num_programs(1) - 1)
    def _():
        def emit(rows):
            o_ref[rows, :] = acc_ref[rows, :].astype(BF16)
        for_each_chunk(emit)


def _moe_ffn(xs, tile_expert, occupied, w_gate, w_up, w_down):
    n_rows = xs.shape[0]
    d_ff = w_gate.shape[-1]
    row = pl.BlockSpec((MOE_ROW_TILE, D_MODEL), lambda i, j, *_: (i, 0))
    up = pl.BlockSpec((None, D_MODEL, MOE_FF_TILE), lambda i, j, ex, occ: (ex[i], 0, j))
    down = pl.BlockSpec((None, MOE_FF_TILE, D_MODEL), lambda i, j, ex, occ: (ex[i], j, 0))
    grid_spec = pltpu.PrefetchScalarGridSpec(
        num_scalar_prefetch=2, grid=(n_rows // MOE_ROW_TILE, d_ff // MOE_FF_TILE),
        in_specs=[row, up, up, down], out_specs=row,
        scratch_shapes=[pltpu.VMEM((MOE_ROW_TILE, D_MODEL), F32)])
    return pl.pallas_call(
        _moe_ffn_kernel, grid_spec=grid_spec,
        out_shape=jax.ShapeDtypeStruct((n_rows, D_MODEL), BF16),
        compiler_params=_params("parallel", "arbitrary"), name="moe_swiglu",
    )(tile_expert, occupied, xs, w_gate, w_up, w_down)


def _ffn_kernel(x_ref, wg_ref, wu_ref, wd_ref, g_ref, b_ref, o_ref):
    x = x_ref[...]
    y = _swiglu_partial(x.astype(BF16), wg_ref[...], wu_ref[...], wd_ref[...])
    o_ref[...] = _layer_norm(DEEPNORM_ALPHA * x + y, g_ref[...], b_ref[...])


def _ffn(x, w_gate, w_up, w_down, g, b):
    n = x.shape[0]
    row = pl.BlockSpec((PROJ_ROW_TILE, D_MODEL), lambda i: (i, 0))
    const = lambda a: pl.BlockSpec(a.shape, lambda i: (0,) * a.ndim, pipeline_mode=pl.Buffered(1))
    g2, b2 = g.reshape(1, -1), b.reshape(1, -1)
    return pl.pallas_call(
        _ffn_kernel, grid=(n // PROJ_ROW_TILE,),
        in_specs=[row, const(w_gate), const(w_up), const(w_down), const(g2), const(b2)], out_specs=row,
        out_shape=jax.ShapeDtypeStruct((n, D_MODEL), F32),
        compiler_params=_params("parallel"), name="swiglu_ln2",
    )(x, w_gate, w_up, w_down, g2, b2)


def _combine_kernel(off_ref, cnt_ref, x_ref, cw_ref, pos_ref, g_ref, b_ref, ys_ref, o_ref,
                    buf_ref, acc_ref, sems):
    b, nblk = pl.program_id(0), pl.num_programs(0)
    copies_of = lambda blk: _chunk_copies(off_ref, cnt_ref, blk, ys_ref, buf_ref, sems, to_hbm=False)

    def start_all(blk):
        for pred, _, k, _, copy in copies_of(blk):
            if k == 0:
                copy.start()
            else:
                pl.when(pred)(copy.start)

    pl.when(b == 0)(lambda: start_all(b))
    pl.when(b + 1 < nblk)(lambda: start_all(b + 1))

    lane = lax.broadcasted_iota(jnp.int32, (ROW_TILE, LANES), 1)
    slot_col = lax.broadcasted_iota(jnp.int32, (ROW_TILE, MOE_CHUNK), 1).astype(F32)
    column = lambda ref, e: jnp.sum(jnp.where(lane == e, ref[...], 0.0), axis=-1, keepdims=True)
    weight = [column(cw_ref, e) for e in range(N_EXPERTS)]
    pos = [column(pos_ref, e) for e in range(N_EXPERTS)]

    def scatter_matrix(e, k):
        return jnp.where(pos[e] == slot_col + float(k * MOE_CHUNK), weight[e], 0.0).astype(BF16)

    first = jnp.concatenate([scatter_matrix(e, 0) for e in range(N_EXPERTS)], axis=1)
    copies = copies_of(b)
    for _, _, k, _, copy in copies:
        if k == 0:
            copy.wait()
    firsts = buf_ref[b % 2, 0:N_EXPERTS].reshape(N_EXPERTS * MOE_CHUNK, D_MODEL)
    acc_ref[...] = jnp.dot(first, firsts, preferred_element_type=F32)
    for pred, e, k, idx, copy in copies:
        if k > 0:
            @pl.when(pred)
            def _(e=e, k=k, idx=idx, copy=copy):
                copy.wait()
                acc_ref[...] += jnp.dot(scatter_matrix(e, k), buf_ref[b % 2, idx], preferred_element_type=F32)

    o_ref[...] = _layer_norm(DEEPNORM_ALPHA * x_ref[...] + acc_ref[...], g_ref[...], b_ref[...])


def _combine(x, ys, cw, pos, seg_off, cnt, g, b):
    n = x.shape[0]
    n_chunks = N_EXPERTS * (ROW_TILE // MOE_CHUNK)
    tok = lambda w: pl.BlockSpec((ROW_TILE, w), lambda i, *_: (i, 0))
    vec = pl.BlockSpec((1, D_MODEL), lambda i, *_: (0, 0))
    grid_spec = pltpu.PrefetchScalarGridSpec(
        num_scalar_prefetch=2, grid=(n // ROW_TILE,),
        in_specs=[tok(D_MODEL), tok(LANES), tok(LANES), vec, vec, pl.BlockSpec(memory_space=pl.ANY)],
        out_specs=tok(D_MODEL),
        scratch_shapes=[pltpu.VMEM((2, n_chunks, MOE_CHUNK, D_MODEL), BF16),
                        pltpu.VMEM((ROW_TILE, D_MODEL), F32),
                        pltpu.SemaphoreType.DMA((2, n_chunks))])
    return pl.pallas_call(
        _combine_kernel, grid_spec=grid_spec,
        out_shape=jax.ShapeDtypeStruct((n, D_MODEL), F32),
        compiler_params=_params("arbitrary"), name="moe_combine_ln2",
    )(seg_off, cnt.reshape(-1), x, cw, pos, g.reshape(1, -1), b.reshape(1, -1), ys)


def _moe(x, w_router, w_gate, w_up, w_down, g, b):
    n = x.shape[0]
    nblk = n // ROW_TILE
    max_rows = 2 * n + nblk * N_EXPERTS * (SEG_ALIGN - 1) + N_EXPERTS * (MOE_CHUNK + MOE_ROW_TILE - 1)
    n_tiles = max_rows // MOE_ROW_TILE
    cw, pos, post, cnt, zeros = _router(x, w_router, n_tiles * MOE_ROW_TILE)
    seg_off, tile_expert, occupied = _moe_layout(cnt, n_tiles)
    xs = _dispatch(x, post, seg_off, cnt, zeros)
    ys = _moe_ffn(xs, tile_expert, occupied, w_gate, w_up, w_down)
    return _combine(x, ys, cw, pos, seg_off, cnt, g, b)


def kernel(x, emb_ln_g, emb_ln_b, w_in, b_gate, na_rpb, sw_sink, w_branch_na, w_branch_sw, w_out,
           ln1_g, ln1_b, ffn_w_gate, ffn_w_up, ffn_w_down, moe_router, moe_w_gate, moe_w_up,
           moe_w_down, ln2_g, ln2_b):
    batch, seq, d = x.shape
    assert (seq, d) == (SEQ, D_MODEL)
    n = batch * seq
    tables = _rotary_tables()
    h = x.reshape(n, d)
    for layer in range(DEPTH):
        h, q_na, k_na, v_na, q_sw, k_sw, v_sw, gates = _proj(
            h, w_in[layer].astype(BF16), b_gate[layer], tables,
            embed_ln=(emb_ln_g, emb_ln_b) if layer == 0 else None)
        seq3 = lambda t: t.reshape(batch, seq, t.shape[-1])
        y_na = _na_attention(seq3(q_na), seq3(k_na), seq3(v_na), _na_bias_tables(na_rpb[layer]))
        y_sw = _sw_attention(seq3(q_sw), seq3(k_sw), seq3(v_sw), sw_sink[layer])
        h = _merge(h, y_na.reshape(n, -1), y_sw.reshape(n, -1), gates,
                   w_branch_na[layer].astype(BF16), w_branch_sw[layer].astype(BF16),
                   w_out[layer].astype(BF16), ln1_g[layer], ln1_b[layer])
        i = layer // 2
        if layer % 2 == 0:
            h = _ffn(h, ffn_w_gate[i].astype(BF16), ffn_w_up[i].astype(BF16),
                     ffn_w_down[i].astype(BF16), ln2_g[layer], ln2_b[layer])
        else:
            h = _moe(h, moe_router[i], moe_w_gate[i], moe_w_up[i], moe_w_down[i],
                     ln2_g[layer], ln2_b[layer])
    return h.reshape(batch, seq, d)
```

```python
import functools

import numpy as np
import jax
import jax.numpy as jnp
from jax import lax
from jax.experimental import pallas as pl
from jax.experimental.pallas import tpu as pltpu

F32 = jnp.float32
BF16 = jnp.bfloat16

D_MODEL = 1024
SEQ = 2048
DEPTH = 2
HEAD_DIM = 64
NA_HEADS = 8
NA_WIDTH = NA_HEADS * HEAD_DIM
GRID_W = 64
GRID_ROWS = SEQ // GRID_W
NA_KH = 8
NA_KW = 16
SW_HEADS = 8
SW_KV_HEADS = 2
SW_GROUP = SW_HEADS // SW_KV_HEADS
SW_WIDTH = SW_HEADS * HEAD_DIM
SW_KV_WIDTH = SW_KV_HEADS * HEAD_DIM
SW_WINDOW = 128
SW_BLOCK = 128
ROT_DIM = HEAD_DIM // 4
ROPE_THETA = 500000.0
OFF_QNA = NA_WIDTH
OFF_KNA = 2 * NA_WIDTH
OFF_VNA = 3 * NA_WIDTH
OFF_QSW = OFF_VNA + SW_WIDTH
OFF_KSW = OFF_QSW + SW_KV_WIDTH
OFF_VSW = OFF_KSW + SW_KV_WIDTH
PROJ_COLS = OFF_VSW + 2 * D_MODEL
N_EXPERTS = 8
DEEPNORM_ALPHA = (2 * DEPTH) ** 0.25
LN_EPS = 1e-5
NEG_INF = -1e30
QK_SCALE = HEAD_DIM ** -0.5
LOG2E = float(np.log2(np.e))

LANES = 128
MXU_DIM = 256
V7X_VMEM_BYTES = 64 * 1024 * 1024
VMEM_LIMIT = V7X_VMEM_BYTES * 7 // 8

PROJ_ROW_TILE = 1024
ROW_TILE = 512
MOE_ROW_TILE = 1536
MOE_FF_TILE = 512
MOE_CHUNK = 256
SEG_ALIGN = 16
NA_ROWS_PER_STEP = 8
SW_BLOCKS_PER_STEP = 8
MERGE_PARTS = 4
ROUTER_BLOCKS = 4


def _layer_norm(z, g, b):
    mu = jnp.mean(z, axis=-1, keepdims=True)
    d = z - mu
    var = jnp.mean(d * d, axis=-1, keepdims=True)
    return d * lax.rsqrt(var + LN_EPS) * g + b


def _sigmoid(z):
    return 1.0 / (1.0 + jnp.exp(-z))


def _params(*sem):
    return pltpu.CompilerParams(dimension_semantics=sem, vmem_limit_bytes=VMEM_LIMIT)


def _rotary_tables():
    half = ROT_DIM // 2
    inv_freq = 1.0 / (ROPE_THETA ** (jnp.arange(0, ROT_DIM, 2, dtype=F32) / ROT_DIM))
    ang = jnp.arange(SEQ, dtype=jnp.int32).astype(F32)[:, None] * inv_freq[None, :]
    cos, sin = jnp.cos(ang), jnp.sin(ang)
    ones = jnp.ones((SEQ, HEAD_DIM - ROT_DIM), F32)
    zeros = jnp.zeros((SEQ, HEAD_DIM - ROT_DIM), F32)
    zh = jnp.zeros((SEQ, half), F32)
    cos_h = jnp.concatenate([cos, cos, ones], axis=1)
    sa_h = jnp.concatenate([-sin, zh, zeros], axis=1)
    sb_h = jnp.concatenate([zh, sin, zeros], axis=1)
    two = lambda t: jnp.concatenate([t, t], axis=1)
    return two(cos_h), two(sa_h), two(sb_h)


def _proj_kernel(*refs, embed_ln):
    if embed_ln:
        x_ref, g_ref, b_ref, w_ref, bg_ref, cos_ref, sa_ref, sb_ref, xn_ref, *outs = refs
        x = _layer_norm(x_ref[...], g_ref[...], b_ref[...])
        xn_ref[...] = x
    else:
        x_ref, w_ref, bg_ref, cos_ref, sa_ref, sb_ref, *outs = refs
        x = x_ref[...]
    qna_ref, kna_ref, vna_ref, qsw_ref, ksw_ref, vsw_ref, gate_ref = outs
    xb = x.astype(BF16)

    def mm(lo, hi):
        return jnp.dot(xb, w_ref[:, lo:hi], preferred_element_type=F32)

    qna_ref[...] = (mm(0, OFF_QNA) * (QK_SCALE * LOG2E)).astype(BF16)
    kna_ref[...] = mm(OFF_QNA, OFF_KNA).astype(BF16)
    vna_ref[...] = mm(OFF_KNA, OFF_VNA).astype(BF16)

    cos, sa, sb = cos_ref[...], sa_ref[...], sb_ref[...]
    half = ROT_DIM // 2

    def rot(t):
        return t * cos + pltpu.roll(t, LANES - half, 1) * sa + pltpu.roll(t, half, 1) * sb

    q = mm(OFF_VNA, OFF_QSW)
    for c in range(SW_WIDTH // LANES):
        sl = slice(c * LANES, (c + 1) * LANES)
        qsw_ref[:, sl] = (rot(q[:, sl]) * (QK_SCALE * LOG2E)).astype(BF16)

    kv = mm(OFF_QSW, OFF_VSW)
    k = rot(kv[:, :SW_KV_WIDTH])
    v = kv[:, SW_KV_WIDTH:]
    first = lax.broadcasted_iota(jnp.int32, k.shape, 1) < HEAD_DIM

    def dup(t):
        r = pltpu.roll(t, HEAD_DIM, 1)
        return jnp.concatenate([jnp.where(first, t, r), jnp.where(first, r, t)], axis=1)

    ksw_ref[...] = dup(k).astype(BF16)
    vsw_ref[...] = v.astype(BF16)

    gw = 512
    for c in range(2 * D_MODEL // gw):
        z = mm(OFF_VSW + c * gw, OFF_VSW + (c + 1) * gw) + bg_ref[:, c * gw:(c + 1) * gw]
        gate_ref[:, c * gw:(c + 1) * gw] = _sigmoid(z).astype(BF16)


def _proj(x, w_bf, b_gate, tables, embed_ln=None):
    n = x.shape[0]
    tiles_per_seq = SEQ // PROJ_ROW_TILE
    row = lambda w: pl.BlockSpec((PROJ_ROW_TILE, w), lambda i: (i, 0))
    full = lambda a: pl.BlockSpec(a.shape, lambda i: (0,) * a.ndim)
    tab = pl.BlockSpec((PROJ_ROW_TILE, LANES), lambda i: (i % tiles_per_seq, 0))
    bg = b_gate.reshape(1, -1)
    widths = (NA_WIDTH, NA_WIDTH, NA_WIDTH, SW_WIDTH, 2 * SW_KV_WIDTH, SW_KV_WIDTH, 2 * D_MODEL)
    ln = [t.reshape(1, -1) for t in embed_ln] if embed_ln else []
    out_specs = [row(w) for w in widths]
    out_shape = [jax.ShapeDtypeStruct((n, w), BF16) for w in widths]
    if embed_ln:
        out_specs = [row(D_MODEL)] + out_specs
        out_shape = [jax.ShapeDtypeStruct((n, D_MODEL), F32)] + out_shape
    outs = pl.pallas_call(
        functools.partial(_proj_kernel, embed_ln=bool(embed_ln)), grid=(n // PROJ_ROW_TILE,),
        in_specs=[row(D_MODEL)] + [full(t) for t in ln] + [full(w_bf), full(bg), tab, tab, tab],
        out_specs=out_specs, out_shape=out_shape,
        compiler_params=_params("parallel"), name="in_proj",
    )(x, *ln, w_bf, bg, *tables)
    return outs if embed_ln else [x] + list(outs)


def _na_bias_kernel(ext_ref, o_ref):
    qc = lax.broadcasted_iota(jnp.int32, (GRID_W, LANES), 0)
    lane = lax.broadcasted_iota(jnp.int32, (GRID_W, LANES), 1)
    kc = lane % GRID_W
    qcs = jnp.clip(qc - NA_KW // 2, 0, GRID_W - NA_KW)
    valid = (kc >= qcs) & (kc < qcs + NA_KW)
    left = lane < GRID_W
    for h in range(NA_HEADS):
        rows = slice((h % 4) * GRID_W, (h % 4 + 1) * GRID_W)

        def toeplitz(d, base):
            tile = jnp.broadcast_to(ext_ref[h, d:d + 1, :], (GRID_W, LANES))
            return pltpu.roll(tile, base, 1, stride=1, stride_axis=0)

        for d in range(2 * NA_KH - 2):
            pair = jnp.where(left, toeplitz(d, GRID_W + 1), toeplitz(d + 1, 1)) * LOG2E
            o_ref[h // 4, d, rows, :] = jnp.where(valid, pair, NEG_INF)


def _na_bias_tables(rpb):
    layers = rpb.shape[0]
    n_dr, n_dc = 2 * NA_KH - 1, 2 * NA_KW - 1
    lead = GRID_W - NA_KW
    ext = jnp.pad(rpb.astype(F32), ((0, 0), (0, 0), (0, 0), (lead, LANES - lead - n_dc)))
    table = (2, n_dr - 1, 4 * GRID_W, LANES)
    return pl.pallas_call(
        _na_bias_kernel, grid=(layers,),
        in_specs=[pl.BlockSpec((None, NA_HEADS, n_dr, LANES), lambda l: (l, 0, 0, 0))],
        out_specs=pl.BlockSpec((None,) + table, lambda l: (l, 0, 0, 0, 0)),
        out_shape=jax.ShapeDtypeStruct((layers,) + table, F32),
        compiler_params=_params("parallel"), name="na_bias_tables",
    )(ext)


def _na_kernel(q_ref, k_ref, v_ref, bias_ref, o_ref):
    lane = lax.broadcasted_iota(jnp.int32, (GRID_W, MXU_DIM), 1)
    head_mask = [(lane >= h * HEAD_DIM) & (lane < (h + 1) * HEAD_DIM) for h in range(4)]
    nk = NA_KH * GRID_W

    def rows(it, carry):
        chains = []
        for u in range(NA_ROWS_PER_STEP):
            r = it * NA_ROWS_PER_STEP + u
            rs = jnp.clip(r - NA_KH // 2, 0, GRID_ROWS - NA_KH)
            variant = rs - r + (NA_KH - 1)
            qrows = pl.ds(pl.multiple_of(r * GRID_W, GRID_W), GRID_W)
            krows = pl.ds(pl.multiple_of(rs * GRID_W, GRID_W), nk)
            q = q_ref[qrows, :]
            for g in range(2):
                gs = slice(g * MXU_DIM, (g + 1) * MXU_DIM)
                qg = q[:, gs]
                lhs = jnp.concatenate([jnp.where(m, qg, jnp.zeros_like(qg)) for m in head_mask], axis=0)
                bias = jnp.concatenate([bias_ref[g, variant + 2 * a] for a in range(NA_KH // 2)], axis=1)
                chains.append(dict(lhs=lhs, k=k_ref[krows, gs], v=v_ref[krows, gs], bias=bias,
                                   out=(qrows, gs)))
        for c in chains:
            c["s"] = lax.dot_general(c["lhs"], c["k"], (((1,), (1,)), ((), ())),
                                     preferred_element_type=F32) + c["bias"]
        for c in chains:
            c["p"] = jnp.exp2(c["s"] - jnp.max(c["s"], axis=-1, keepdims=True))
        for c in chains:
            c["l"] = jnp.sum(c["p"], axis=-1, keepdims=True)
        for c in chains:
            c["o"] = jnp.dot(c["p"].astype(BF16), c["v"], preferred_element_type=F32) / c["l"]
        for c in chains:
            og = jnp.zeros((GRID_W, MXU_DIM), F32)
            for h in range(4):
                og = og + jnp.where(head_mask[h], c["o"][h * GRID_W:(h + 1) * GRID_W], 0.0)
            o_ref[c["out"]] = og.astype(BF16)
        return carry

    lax.fori_loop(0, GRID_ROWS // NA_ROWS_PER_STEP, rows, 0)


def _na_attention(q, k, v, bias, layer):
    b = q.shape[0]
    seq = pl.BlockSpec((None, SEQ, NA_WIDTH), lambda i: (i, 0, 0))
    return pl.pallas_call(
        _na_kernel, grid=(b,),
        in_specs=[seq, seq, seq, pl.BlockSpec((None,) + bias.shape[1:], lambda i: (layer, 0, 0, 0, 0))],
        out_specs=seq, out_shape=jax.ShapeDtypeStruct((b, SEQ, NA_WIDTH), BF16),
        compiler_params=_params("parallel"), name="na_attention",
    )(q, k, v, bias)


def _sw_masks():
    nkeys = 3 * SW_BLOCK
    jk = np.arange(nkeys)[:, None]
    iq = np.arange(SW_BLOCK)[None, :]
    band = (jk >= iq) & (jk <= iq + 2 * SW_WINDOW)
    in_seq = [(jk >= SW_BLOCK), np.ones_like(band), (jk < 2 * SW_BLOCK)]
    return jnp.asarray(np.stack([np.where(band & ok, 0.0, NEG_INF) for ok in in_seq]), F32)


def _sw_kernel(sink_ref, q_ref, k_ref, v_ref, mask_ref, o_ref):
    nb = SEQ // SW_BLOCK
    first = lax.broadcasted_iota(jnp.int32, (SW_BLOCK, LANES), 1) < HEAD_DIM
    ones = jnp.ones((8, 3 * SW_BLOCK), BF16)

    def blocks(it, carry):
        chains = []
        for u in range(SW_BLOCKS_PER_STEP):
            n = it * SW_BLOCKS_PER_STEP + u
            mask = mask_ref[jnp.where(n == 0, 0, jnp.where(n == nb - 1, 2, 1))]
            starts = [pl.multiple_of(jnp.clip(n + d, 0, nb - 1) * SW_BLOCK, SW_BLOCK) for d in (-1, 0, 1)]
            qrows = pl.ds(pl.multiple_of(n * SW_BLOCK, SW_BLOCK), SW_BLOCK)
            q = q_ref[qrows, :]
            vv = jnp.concatenate([v_ref[pl.ds(s, SW_BLOCK), :] for s in starts], axis=0)
            vt = vv.astype(F32).T.astype(BF16)
            for kvh in range(SW_KV_HEADS):
                ks = slice(kvh * LANES, (kvh + 1) * LANES)
                kk = jnp.concatenate([k_ref[pl.ds(s, SW_BLOCK), ks] for s in starts], axis=0)
                parts, sinks = [], []
                for j in range(SW_GROUP):
                    h = kvh * SW_GROUP + j
                    qc = q[:, (h // 2) * LANES:(h // 2 + 1) * LANES]
                    keep = first if h % 2 == 0 else jnp.logical_not(first)
                    parts.append(jnp.where(keep, qc, jnp.zeros_like(qc)))
                    sinks.append(jnp.full((1, SW_BLOCK), sink_ref[h] * LOG2E, F32))
                vt_ones = jnp.concatenate([vt[kvh * HEAD_DIM:(kvh + 1) * HEAD_DIM], ones], axis=0)
                chains.append(dict(k=kk, vt=vt_ones, q=jnp.concatenate(parts, axis=0),
                                   sink=jnp.concatenate(sinks, axis=1), mask=mask, qrows=qrows, kvh=kvh))
        for c in chains:
            s = lax.dot_general(c["k"], c["q"], (((1,), (1,)), ((), ())), preferred_element_type=F32)
            prev_mask, next_mask = c["mask"][:SW_BLOCK], c["mask"][2 * SW_BLOCK:]
            cols = []
            for j in range(SW_GROUP):
                sj = s[:, j * SW_BLOCK:(j + 1) * SW_BLOCK]
                cols.append(jnp.concatenate([sj[:SW_BLOCK] + prev_mask, sj[SW_BLOCK:2 * SW_BLOCK],
                                             sj[2 * SW_BLOCK:] + next_mask], axis=0))
            c["s"] = jnp.concatenate(cols, axis=1)
        for c in chains:
            c["m"] = jnp.maximum(jnp.max(c["s"], axis=0, keepdims=True), c["sink"])
        for c in chains:
            c["p"] = jnp.exp2(c["s"] - c["m"]).astype(BF16)
        for c in chains:
            o = jnp.dot(c["vt"], c["p"], preferred_element_type=F32)
            denom = o[HEAD_DIM:HEAD_DIM + 1] + jnp.exp2(c["sink"] - c["m"])
            c["o"] = o[:HEAD_DIM] / denom
        for c in chains:
            for pair in range(SW_GROUP // 2):
                even = c["o"][:, (2 * pair) * SW_BLOCK:(2 * pair + 1) * SW_BLOCK]
                odd = c["o"][:, (2 * pair + 1) * SW_BLOCK:(2 * pair + 2) * SW_BLOCK]
                col = (c["kvh"] * (SW_GROUP // 2) + pair) * LANES
                o_ref[c["qrows"], col:col + LANES] = jnp.concatenate([even, odd], axis=0).T.astype(BF16)
        return carry

    lax.fori_loop(0, nb // SW_BLOCKS_PER_STEP, blocks, 0)


def _sw_attention(q, k2, v, sink):
    b = q.shape[0]
    masks = _sw_masks()
    seq = lambda w: pl.BlockSpec((None, SEQ, w), lambda i: (i, 0, 0))
    return pl.pallas_call(
        _sw_kernel, grid=(b,),
        in_specs=[pl.BlockSpec(memory_space=pltpu.SMEM), seq(SW_WIDTH), seq(2 * SW_KV_WIDTH),
                  seq(SW_KV_WIDTH), pl.BlockSpec(masks.shape, lambda i: (0, 0, 0))],
        out_specs=seq(SW_WIDTH), out_shape=jax.ShapeDtypeStruct((b, SEQ, SW_WIDTH), BF16),
        compiler_params=_params("parallel"), name="sw_attention",
    )(sink.astype(F32), q, k2, v, masks)


def _merge_kernel(x_ref, yna_ref, ysw_ref, gate_ref, wna_ref, wsw_ref, wout_ref, g_ref, b_ref, o_ref):
    part = x_ref.shape[0] // MERGE_PARTS
    parts = [slice(p * part, (p + 1) * part) for p in range(MERGE_PARTS)]
    dot = functools.partial(jnp.dot, preferred_element_type=F32)
    branches = [(dot(yna_ref[r, :], wna_ref[...]), dot(ysw_ref[r, :], wsw_ref[...])) for r in parts]
    mixed = [(gate_ref[r, :D_MODEL].astype(F32) * a + gate_ref[r, D_MODEL:].astype(F32) * s).astype(BF16)
             for r, (a, s) in zip(parts, branches)]
    summed = [DEEPNORM_ALPHA * x_ref[r, :] + dot(m, wout_ref[...]) for r, m in zip(parts, mixed)]
    for r, z in zip(parts, summed):
        o_ref[r, :] = _layer_norm(z, g_ref[...], b_ref[...])


def _merge(x, y_na, y_sw, gates, w_na, w_sw, w_out, g, b):
    n = x.shape[0]
    row = lambda w: pl.BlockSpec((PROJ_ROW_TILE, w), lambda i: (i, 0))
    full = lambda a: pl.BlockSpec(a.shape, lambda i: (0,) * a.ndim)
    g2, b2 = g.reshape(1, -1), b.reshape(1, -1)
    return pl.pallas_call(
        _merge_kernel, grid=(n // PROJ_ROW_TILE,),
        in_specs=[row(D_MODEL), row(NA_WIDTH), row(SW_WIDTH), row(2 * D_MODEL),
                  full(w_na), full(w_sw), full(w_out), full(g2), full(b2)],
        out_specs=row(D_MODEL), out_shape=jax.ShapeDtypeStruct((n, D_MODEL), F32),
        compiler_params=_params("parallel"), name="merge_ln1",
    )(x, y_na, y_sw, gates, w_na, w_sw, w_out, g2, b2)


def _router_kernel(x_ref, w_ref, cw_ref, pos_ref, post_ref, cnt_ref, zero_ref):
    zero_ref[...] = jnp.zeros_like(zero_ref)

    blocks = [slice(p * ROW_TILE, (p + 1) * ROW_TILE) for p in range(ROUTER_BLOCKS)]
    dot = functools.partial(jnp.dot, preferred_element_type=F32)
    w = w_ref[...]
    wh = w.astype(BF16)
    wl = (w - wh.astype(F32)).astype(BF16)
    xs = [x_ref[r, :] for r in blocks]
    xh = [x.astype(BF16) for x in xs]
    xl = [(x - h.astype(F32)).astype(BF16) for x, h in zip(xs, xh)]
    logits = [dot(h, wh) + dot(l, wh) + dot(h, wl) for h, l in zip(xh, xl)]
    lane = lax.broadcasted_iota(jnp.int32, (ROW_TILE, LANES), 1).astype(F32)
    lg = [jnp.where(lane < N_EXPERTS, t, -jnp.inf) for t in logits]
    m1 = [jnp.max(t, axis=-1, keepdims=True) for t in lg]
    i1 = [jnp.min(jnp.where(t == m, lane, float(LANES)), axis=-1, keepdims=True) for t, m in zip(lg, m1)]
    lg2 = [jnp.where(lane == i, -jnp.inf, t) for t, i in zip(lg, i1)]
    m2 = [jnp.max(t, axis=-1, keepdims=True) for t in lg2]
    i2 = [jnp.min(jnp.where(t == m, lane, float(LANES)), axis=-1, keepdims=True) for t, m in zip(lg2, m2)]
    e2 = [jnp.exp(b - a) for a, b in zip(m1, m2)]
    cw = [jnp.where(lane == a, 1.0 / (1.0 + e), 0.0) + jnp.where(lane == b, e / (1.0 + e), 0.0)
          for a, b, e in zip(i1, i2, e2)]

    t = ROW_TILE
    tri = jnp.where(lax.broadcasted_iota(jnp.int32, (t, t), 0) <= lax.broadcasted_iota(jnp.int32, (t, t), 1),
                    1.0, 0.0).astype(BF16)
    sel = [jnp.where(c.T[:N_EXPERTS] > 0.0, 1.0, 0.0) for c in cw]
    incl = [dot(s.astype(BF16), tri) for s in sel]
    post = [jnp.where(s > 0.0, n - 1.0, -1.0) for s, n in zip(sel, incl)]
    unrouted = jnp.full((LANES - N_EXPERTS, t), -1.0, F32)
    for p, r in enumerate(blocks):
        cw_ref[r, :] = cw[p]
        post_ref[:, r] = post[p]
        pos_ref[r, :] = jnp.concatenate([post[p], unrouted], axis=0).T
        cnt_ref[p * N_EXPERTS:(p + 1) * N_EXPERTS, :] = jnp.broadcast_to(incl[p][:, t - 1:t], (N_EXPERTS, LANES))


def _router(x, w_router, n_rows):
    n = x.shape[0]
    nblk = n // ROW_TILE
    step = ROUTER_BLOCKS * ROW_TILE
    steps = n // step
    assert n_rows % (steps * SEG_ALIGN) == 0
    w = jnp.zeros((D_MODEL, LANES), F32).at[:, :N_EXPERTS].set(w_router)
    tok = pl.BlockSpec((step, LANES), lambda i: (i, 0))
    cw, pos, post, cnt, zeros = pl.pallas_call(
        _router_kernel, grid=(steps,),
        in_specs=[pl.BlockSpec((step, D_MODEL), lambda i: (i, 0)),
                  pl.BlockSpec((D_MODEL, LANES), lambda i: (0, 0))],
        out_specs=[tok, tok, pl.BlockSpec((N_EXPERTS, step), lambda i: (0, i)),
                   pl.BlockSpec((ROUTER_BLOCKS * N_EXPERTS, LANES), lambda i: (i, 0)),
                   pl.BlockSpec((n_rows // steps, D_MODEL), lambda i: (i, 0))],
        out_shape=[jax.ShapeDtypeStruct((n, LANES), F32), jax.ShapeDtypeStruct((n, LANES), F32),
                   jax.ShapeDtypeStruct((N_EXPERTS, n), F32),
                   jax.ShapeDtypeStruct((nblk * N_EXPERTS, LANES), F32),
                   jax.ShapeDtypeStruct((n_rows, D_MODEL), BF16)],
        compiler_params=_params("parallel"), name="router_top2",
    )(x, w)
    return cw, pos, post, cnt[:, 0].astype(jnp.int32).reshape(nblk, N_EXPERTS), zeros


def _moe_layout(cnt, n_tiles):
    seg = (cnt + SEG_ALIGN - 1) // SEG_ALIGN * SEG_ALIGN
    rows_e = jnp.sum(seg, axis=0)
    tiles_e = (rows_e + MOE_CHUNK + MOE_ROW_TILE - 1) // MOE_ROW_TILE
    tile_end = jnp.cumsum(tiles_e)
    tile_off = tile_end - tiles_e
    seg_off = (tile_off * MOE_ROW_TILE)[None, :] + jnp.cumsum(seg, axis=0) - seg
    tile = jnp.arange(n_tiles, dtype=jnp.int32)
    tile_expert = jnp.minimum(jnp.sum(tile[:, None] >= tile_end[None, :], axis=1), N_EXPERTS - 1)
    occupied = jnp.clip(rows_e[tile_expert] - (tile - tile_off[tile_expert]) * MOE_ROW_TILE, 0, MOE_ROW_TILE)
    as_i32 = lambda t: t.astype(jnp.int32)
    return as_i32(seg_off).reshape(-1), as_i32(tile_expert), as_i32(occupied)


def _chunk_copies(off_ref, cnt_ref, block, hbm_ref, vmem_ref, sems, to_hbm):
    slot = block % 2
    out = []
    for e in range(N_EXPERTS):
        c = cnt_ref[block * N_EXPERTS + e]
        off = pl.multiple_of(off_ref[block * N_EXPERTS + e], SEG_ALIGN)
        for k in range(ROW_TILE // MOE_CHUNK):
            idx = k * N_EXPERTS + e
            rows = hbm_ref.at[pl.ds(off + k * MOE_CHUNK, MOE_CHUNK)]
            staged = vmem_ref.at[slot, idx]
            src, dst = (staged, rows) if to_hbm else (rows, staged)
            out.append((c > k * MOE_CHUNK, e, k, idx, pltpu.make_async_copy(src, dst, sems.at[slot, idx])))
    return out


def _dispatch_kernel(off_ref, cnt_ref, x_ref, post_ref, xs_in_ref, xs_ref, stage_ref, sems):
    del xs_in_ref
    b, nblk = pl.program_id(0), pl.num_programs(0)
    copies_of = lambda blk: _chunk_copies(off_ref, cnt_ref, blk, xs_ref, stage_ref, sems, to_hbm=True)

    def wait_all(blk):
        for pred, _, _, _, copy in copies_of(blk):
            pl.when(pred)(copy.wait)

    xb = x_ref[...].astype(BF16)
    slot_row = lax.broadcasted_iota(jnp.int32, (MOE_CHUNK, ROW_TILE), 0).astype(F32)

    def gather_matrix(e, k):
        return jnp.where(post_ref[e:e + 1, :] == slot_row + float(k * MOE_CHUNK), 1.0, 0.0).astype(BF16)

    firsts = jnp.concatenate([gather_matrix(e, 0) for e in range(N_EXPERTS)], axis=0)
    stage_ref[b % 2, 0:N_EXPERTS] = jnp.dot(firsts, xb, preferred_element_type=F32).astype(BF16).reshape(
        N_EXPERTS, MOE_CHUNK, D_MODEL)
    copies = copies_of(b)
    for pred, e, k, idx, _ in copies:
        if k > 0:
            @pl.when(pred)
            def _(e=e, k=k, idx=idx):
                stage_ref[b % 2, idx] = jnp.dot(gather_matrix(e, k), xb, preferred_element_type=F32).astype(BF16)
    pl.when(b >= 1)(lambda: wait_all(b - 1))
    for pred, _, _, _, copy in copies:
        pl.when(pred)(copy.start)
    pl.when(b == nblk - 1)(lambda: wait_all(b))


def _dispatch(x, post, seg_off, cnt, zeros):
    n = x.shape[0]
    n_chunks = N_EXPERTS * (ROW_TILE // MOE_CHUNK)
    grid_spec = pltpu.PrefetchScalarGridSpec(
        num_scalar_prefetch=2, grid=(n // ROW_TILE,),
        in_specs=[pl.BlockSpec((ROW_TILE, D_MODEL), lambda i, *_: (i, 0)),
                  pl.BlockSpec((N_EXPERTS, ROW_TILE), lambda i, *_: (0, i)),
                  pl.BlockSpec(memory_space=pl.ANY)],
        out_specs=pl.BlockSpec(memory_space=pl.ANY),
        scratch_shapes=[pltpu.VMEM((2, n_chunks, MOE_CHUNK, D_MODEL), BF16),
                        pltpu.SemaphoreType.DMA((2, n_chunks))])
    return pl.pallas_call(
        _dispatch_kernel, grid_spec=grid_spec,
        out_shape=jax.ShapeDtypeStruct(zeros.shape, BF16),
        input_output_aliases={4: 0},
        compiler_params=_params("arbitrary"), name="moe_dispatch",
    )(seg_off, cnt.reshape(-1), x, post, zeros)


def _swiglu_partial(xb, wg, wu, wd):
    hg = jnp.dot(xb, wg, preferred_element_type=F32)
    hu = jnp.dot(xb, wu, preferred_element_type=F32)
    h = hg * _sigmoid(hg) * hu
    return jnp.dot(h.astype(BF16), wd, preferred_element_type=F32)


def _moe_ffn_kernel(expert_ref, occ_ref, x_ref, wg_ref, wu_ref, wd_ref, o_ref, acc_ref):
    del expert_ref
    i, j = pl.program_id(0), pl.program_id(1)
    occupied = occ_ref[i]

    @pl.when(j == 0)
    def _():
        acc_ref[...] = jnp.zeros_like(acc_ref)

    def weights():
        return wg_ref[...].astype(BF16), wu_ref[...].astype(BF16), wd_ref[...].astype(BF16)

    chunks = (occupied + MOE_CHUNK - 1) // MOE_CHUNK
    for n in range(1, MOE_ROW_TILE // MOE_CHUNK + 1):
        rows = slice(0, n * MOE_CHUNK)

        @pl.when(chunks == n)
        def _(rows=rows):
            acc_ref[rows, :] += _swiglu_partial(x_ref[rows, :], *weights())

    @pl.when(j == pl.num_programs(1) - 1)
    def _():
        o_ref[...] = acc_ref[...].astype(BF16)


def _moe_ffn(xs, tile_expert, occupied, w_gate, w_up, w_down):
    n_rows = xs.shape[0]
    d_ff = w_gate.shape[-1]
    row = pl.BlockSpec((MOE_ROW_TILE, D_MODEL), lambda i, j, *_: (i, 0))
    up = pl.BlockSpec((None, D_MODEL, MOE_FF_TILE), lambda i, j, ex, occ: (ex[i], 0, j))
    down = pl.BlockSpec((None, MOE_FF_TILE, D_MODEL), lambda i, j, ex, occ: (ex[i], j, 0))
    grid_spec = pltpu.PrefetchScalarGridSpec(
        num_scalar_prefetch=2, grid=(n_rows // MOE_ROW_TILE, d_ff // MOE_FF_TILE),
        in_specs=[row, up, up, down], out_specs=row,
        scratch_shapes=[pltpu.VMEM((MOE_ROW_TILE, D_MODEL), F32)])
    return pl.pallas_call(
        _moe_ffn_kernel, grid_spec=grid_spec,
        out_shape=jax.ShapeDtypeStruct((n_rows, D_MODEL), BF16),
        compiler_params=_params("parallel", "arbitrary"), name="moe_swiglu",
    )(tile_expert, occupied, xs, w_gate, w_up, w_down)


def _ffn_kernel(x_ref, wg_ref, wu_ref, wd_ref, g_ref, b_ref, o_ref):
    x = x_ref[...]
    y = _swiglu_partial(x.astype(BF16), wg_ref[...], wu_ref[...], wd_ref[...])
    o_ref[...] = _layer_norm(DEEPNORM_ALPHA * x + y, g_ref[...], b_ref[...])


def _ffn(x, w_gate, w_up, w_down, g, b):
    n = x.shape[0]
    row = pl.BlockSpec((PROJ_ROW_TILE, D_MODEL), lambda i: (i, 0))
    const = lambda a: pl.BlockSpec(a.shape, lambda i: (0,) * a.ndim, pipeline_mode=pl.Buffered(1))
    g2, b2 = g.reshape(1, -1), b.reshape(1, -1)
    return pl.pallas_call(
        _ffn_kernel, grid=(n // PROJ_ROW_TILE,),
        in_specs=[row, const(w_gate), const(w_up), const(w_down), const(g2), const(b2)], out_specs=row,
        out_shape=jax.ShapeDtypeStruct((n, D_MODEL), F32),
        compiler_params=_params("parallel"), name="swiglu_ln2",
    )(x, w_gate, w_up, w_down, g2, b2)


def _combine_kernel(off_ref, cnt_ref, x_ref, cw_ref, pos_ref, g_ref, b_ref, ys_ref, o_ref,
                    buf_ref, acc_ref, sems):
    b, nblk = pl.program_id(0), pl.num_programs(0)
    copies_of = lambda blk: _chunk_copies(off_ref, cnt_ref, blk, ys_ref, buf_ref, sems, to_hbm=False)

    def start_all(blk):
        for pred, _, k, _, copy in copies_of(blk):
            if k == 0:
                copy.start()
            else:
                pl.when(pred)(copy.start)

    pl.when(b == 0)(lambda: start_all(b))
    pl.when(b + 1 < nblk)(lambda: start_all(b + 1))

    lane = lax.broadcasted_iota(jnp.int32, (ROW_TILE, LANES), 1)
    slot_col = lax.broadcasted_iota(jnp.int32, (ROW_TILE, MOE_CHUNK), 1).astype(F32)
    column = lambda ref, e: jnp.sum(jnp.where(lane == e, ref[...], 0.0), axis=-1, keepdims=True)
    weight = [column(cw_ref, e) for e in range(N_EXPERTS)]
    pos = [column(pos_ref, e) for e in range(N_EXPERTS)]

    def scatter_matrix(e, k):
        return jnp.where(pos[e] == slot_col + float(k * MOE_CHUNK), weight[e], 0.0).astype(BF16)

    first = jnp.concatenate([scatter_matrix(e, 0) for e in range(N_EXPERTS)], axis=1)
    copies = copies_of(b)
    for _, _, k, _, copy in copies:
        if k == 0:
            copy.wait()
    firsts = buf_ref[b % 2, 0:N_EXPERTS].reshape(N_EXPERTS * MOE_CHUNK, D_MODEL)
    acc_ref[...] = jnp.dot(first, firsts, preferred_element_type=F32)
    for pred, e, k, idx, copy in copies:
        if k > 0:
            @pl.when(pred)
            def _(e=e, k=k, idx=idx, copy=copy):
                copy.wait()
                acc_ref[...] += jnp.dot(scatter_matrix(e, k), buf_ref[b % 2, idx], preferred_element_type=F32)

    o_ref[...] = _layer_norm(DEEPNORM_ALPHA * x_ref[...] + acc_ref[...], g_ref[...], b_ref[...])


def _combine(x, ys, cw, pos, seg_off, cnt, g, b):
    n = x.shape[0]
    n_chunks = N_EXPERTS * (ROW_TILE // MOE_CHUNK)
    tok = lambda w: pl.BlockSpec((ROW_TILE, w), lambda i, *_: (i, 0))
    vec = pl.BlockSpec((1, D_MODEL), lambda i, *_: (0, 0))
    grid_spec = pltpu.PrefetchScalarGridSpec(
        num_scalar_prefetch=2, grid=(n // ROW_TILE,),
        in_specs=[tok(D_MODEL), tok(LANES), tok(LANES), vec, vec, pl.BlockSpec(memory_space=pl.ANY)],
        out_specs=tok(D_MODEL),
        scratch_shapes=[pltpu.VMEM((2, n_chunks, MOE_CHUNK, D_MODEL), BF16),
                        pltpu.VMEM((ROW_TILE, D_MODEL), F32),
                        pltpu.SemaphoreType.DMA((2, n_chunks))])
    return pl.pallas_call(
        _combine_kernel, grid_spec=grid_spec,
        out_shape=jax.ShapeDtypeStruct((n, D_MODEL), F32),
        compiler_params=_params("arbitrary"), name="moe_combine_ln2",
    )(seg_off, cnt.reshape(-1), x, cw, pos, g.reshape(1, -1), b.reshape(1, -1), ys)


def _moe(x, w_router, w_gate, w_up, w_down, g, b):
    n = x.shape[0]
    nblk = n // ROW_TILE
    max_rows = 2 * n + nblk * N_EXPERTS * (SEG_ALIGN - 1) + N_EXPERTS * (MOE_CHUNK + MOE_ROW_TILE - 1)
    n_tiles = max_rows // MOE_ROW_TILE
    cw, pos, post, cnt, zeros = _router(x, w_router, n_tiles * MOE_ROW_TILE)
    seg_off, tile_expert, occupied = _moe_layout(cnt, n_tiles)
    xs = _dispatch(x, post, seg_off, cnt, zeros)
    ys = _moe_ffn(xs, tile_expert, occupied, w_gate, w_up, w_down)
    return _combine(x, ys, cw, pos, seg_off, cnt, g, b)


def kernel(x, emb_ln_g, emb_ln_b, w_in, b_gate, na_rpb, sw_sink, w_branch_na, w_branch_sw, w_out,
           ln1_g, ln1_b, ffn_w_gate, ffn_w_up, ffn_w_down, moe_router, moe_w_gate, moe_w_up,
           moe_w_down, ln2_g, ln2_b):
    batch, seq, d = x.shape
    assert (seq, d) == (SEQ, D_MODEL)
    n = batch * seq
    tables = _rotary_tables()
    na_bias = _na_bias_tables(na_rpb)
    h = x.reshape(n, d)
    for layer in range(DEPTH):
        h, q_na, k_na, v_na, q_sw, k_sw, v_sw, gates = _proj(
            h, w_in[layer].astype(BF16), b_gate[layer], tables,
            embed_ln=(emb_ln_g, emb_ln_b) if layer == 0 else None)
        seq3 = lambda t: t.reshape(batch, seq, t.shape[-1])
        y_na = _na_attention(seq3(q_na), seq3(k_na), seq3(v_na), na_bias, layer)
        y_sw = _sw_attention(seq3(q_sw), seq3(k_sw), seq3(v_sw), sw_sink[layer])
        h = _merge(h, y_na.reshape(n, -1), y_sw.reshape(n, -1), gates,
                   w_branch_na[layer].astype(BF16), w_branch_sw[layer].astype(BF16),
                   w_out[layer].astype(BF16), ln1_g[layer], ln1_b[layer])
        i = layer // 2
        if layer % 2 == 0:
            h = _ffn(h, ffn_w_gate[i].astype(BF16), ffn_w_up[i].astype(BF16),
                     ffn_w_down[i].astype(BF16), ln2_g[layer], ln2_b[layer])
        else:
            h = _moe(h, moe_router[i], moe_w_gate[i], moe_w_up[i], moe_w_down[i],
                     ln2_g[layer], ln2_b[layer])
    return h.reshape(batch, seq, d)
```

```python
import functools

import numpy as np
import jax
import jax.numpy as jnp
from jax import lax
from jax.experimental import pallas as pl
from jax.experimental.pallas import tpu as pltpu

F32 = jnp.float32
BF16 = jnp.bfloat16

D_MODEL = 1024
SEQ = 2048
DEPTH = 2
HEAD_DIM = 64
NA_HEADS = 8
NA_WIDTH = NA_HEADS * HEAD_DIM
GRID_W = 64
GRID_ROWS = SEQ // GRID_W
NA_KH = 8
NA_KW = 16
SW_HEADS = 8
SW_KV_HEADS = 2
SW_GROUP = SW_HEADS // SW_KV_HEADS
SW_WIDTH = SW_HEADS * HEAD_DIM
SW_KV_WIDTH = SW_KV_HEADS * HEAD_DIM
SW_WINDOW = 128
SW_BLOCK = 128
ROT_DIM = HEAD_DIM // 4
ROPE_THETA = 500000.0
OFF_QNA = NA_WIDTH
OFF_KNA = 2 * NA_WIDTH
OFF_VNA = 3 * NA_WIDTH
OFF_QSW = OFF_VNA + SW_WIDTH
OFF_KSW = OFF_QSW + SW_KV_WIDTH
OFF_VSW = OFF_KSW + SW_KV_WIDTH
PROJ_COLS = OFF_VSW + 2 * D_MODEL
N_EXPERTS = 8
DEEPNORM_ALPHA = (2 * DEPTH) ** 0.25
LN_EPS = 1e-5
NEG_INF = -1e30
QK_SCALE = HEAD_DIM ** -0.5
LOG2E = float(np.log2(np.e))

LANES = 128
MXU_DIM = 256
V7X_VMEM_BYTES = 64 * 1024 * 1024
VMEM_LIMIT = V7X_VMEM_BYTES * 7 // 8

PROJ_ROW_TILE = 1024
ROW_TILE = 512
MOE_ROW_TILE = 1536
MOE_FF_TILE = 512
MOE_CHUNK = 256
SEG_ALIGN = 16
NA_ROWS_PER_STEP = 16
SW_BLOCKS_PER_STEP = 16
MERGE_PARTS = 4
ROUTER_BLOCKS = 4


def _layer_norm(z, g, b):
    mu = jnp.mean(z, axis=-1, keepdims=True)
    d = z - mu
    var = jnp.mean(d * d, axis=-1, keepdims=True)
    return d * lax.rsqrt(var + LN_EPS) * g + b


def _sigmoid(z):
    return 1.0 / (1.0 + jnp.exp(-z))


def _params(*sem):
    return pltpu.CompilerParams(dimension_semantics=sem, vmem_limit_bytes=VMEM_LIMIT)


def _rotary_tables():
    half = ROT_DIM // 2
    inv_freq = 1.0 / (ROPE_THETA ** (jnp.arange(0, ROT_DIM, 2, dtype=F32) / ROT_DIM))
    ang = jnp.arange(SEQ, dtype=jnp.int32).astype(F32)[:, None] * inv_freq[None, :]
    cos, sin = jnp.cos(ang), jnp.sin(ang)
    ones = jnp.ones((SEQ, HEAD_DIM - ROT_DIM), F32)
    zeros = jnp.zeros((SEQ, HEAD_DIM - ROT_DIM), F32)
    zh = jnp.zeros((SEQ, half), F32)
    cos_h = jnp.concatenate([cos, cos, ones], axis=1)
    sa_h = jnp.concatenate([-sin, zh, zeros], axis=1)
    sb_h = jnp.concatenate([zh, sin, zeros], axis=1)
    two = lambda t: jnp.concatenate([t, t], axis=1)
    return two(cos_h), two(sa_h), two(sb_h)


def _proj_kernel(*refs, embed_ln):
    if embed_ln:
        x_ref, g_ref, b_ref, w_ref, bg_ref, cos_ref, sa_ref, sb_ref, xn_ref, *outs = refs
        x = _layer_norm(x_ref[...], g_ref[...], b_ref[...])
        xn_ref[...] = x
    else:
        x_ref, w_ref, bg_ref, cos_ref, sa_ref, sb_ref, *outs = refs
        x = x_ref[...]
    qna_ref, kna_ref, vna_ref, qsw_ref, ksw_ref, vsw_ref, gate_ref = outs
    xb = x.astype(BF16)

    def mm(lo, hi):
        return jnp.dot(xb, w_ref[:, lo:hi], preferred_element_type=F32)

    qna_ref[...] = (mm(0, OFF_QNA) * (QK_SCALE * LOG2E)).astype(BF16)
    kna_ref[...] = mm(OFF_QNA, OFF_KNA).astype(BF16)
    vna_ref[...] = mm(OFF_KNA, OFF_VNA).astype(BF16)

    cos, sa, sb = cos_ref[...], sa_ref[...], sb_ref[...]
    half = ROT_DIM // 2

    def rot(t):
        return t * cos + pltpu.roll(t, LANES - half, 1) * sa + pltpu.roll(t, half, 1) * sb

    q = mm(OFF_VNA, OFF_QSW)
    for c in range(SW_WIDTH // LANES):
        sl = slice(c * LANES, (c + 1) * LANES)
        qsw_ref[:, sl] = (rot(q[:, sl]) * (QK_SCALE * LOG2E)).astype(BF16)

    kv = mm(OFF_QSW, OFF_VSW)
    k = rot(kv[:, :SW_KV_WIDTH])
    v = kv[:, SW_KV_WIDTH:]
    first = lax.broadcasted_iota(jnp.int32, k.shape, 1) < HEAD_DIM

    def dup(t):
        r = pltpu.roll(t, HEAD_DIM, 1)
        return jnp.concatenate([jnp.where(first, t, r), jnp.where(first, r, t)], axis=1)

    ksw_ref[...] = dup(k).astype(BF16)
    vsw_ref[...] = v.astype(BF16)

    gw = 512
    for c in range(2 * D_MODEL // gw):
        z = mm(OFF_VSW + c * gw, OFF_VSW + (c + 1) * gw) + bg_ref[:, c * gw:(c + 1) * gw]
        gate_ref[:, c * gw:(c + 1) * gw] = _sigmoid(z).astype(BF16)


def _proj(x, w_bf, b_gate, tables, embed_ln=None):
    n = x.shape[0]
    tiles_per_seq = SEQ // PROJ_ROW_TILE
    row = lambda w: pl.BlockSpec((PROJ_ROW_TILE, w), lambda i: (i, 0))
    full = lambda a: pl.BlockSpec(a.shape, lambda i: (0,) * a.ndim)
    tab = pl.BlockSpec((PROJ_ROW_TILE, LANES), lambda i: (i % tiles_per_seq, 0))
    bg = b_gate.reshape(1, -1)
    widths = (NA_WIDTH, NA_WIDTH, NA_WIDTH, SW_WIDTH, 2 * SW_KV_WIDTH, SW_KV_WIDTH, 2 * D_MODEL)
    ln = [t.reshape(1, -1) for t in embed_ln] if embed_ln else []
    out_specs = [row(w) for w in widths]
    out_shape = [jax.ShapeDtypeStruct((n, w), BF16) for w in widths]
    if embed_ln:
        out_specs = [row(D_MODEL)] + out_specs
        out_shape = [jax.ShapeDtypeStruct((n, D_MODEL), F32)] + out_shape
    outs = pl.pallas_call(
        functools.partial(_proj_kernel, embed_ln=bool(embed_ln)), grid=(n // PROJ_ROW_TILE,),
        in_specs=[row(D_MODEL)] + [full(t) for t in ln] + [full(w_bf), full(bg), tab, tab, tab],
        out_specs=out_specs, out_shape=out_shape,
        compiler_params=_params("parallel"), name="in_proj",
    )(x, *ln, w_bf, bg, *tables)
    return outs if embed_ln else [x] + list(outs)


def _na_bias_kernel(ext_ref, o_ref):
    qc = lax.broadcasted_iota(jnp.int32, (GRID_W, LANES), 0)
    lane = lax.broadcasted_iota(jnp.int32, (GRID_W, LANES), 1)
    kc = lane % GRID_W
    qcs = jnp.clip(qc - NA_KW // 2, 0, GRID_W - NA_KW)
    valid = (kc >= qcs) & (kc < qcs + NA_KW)
    left = lane < GRID_W
    for h in range(NA_HEADS):
        rows = slice((h % 4) * GRID_W, (h % 4 + 1) * GRID_W)

        def toeplitz(d, base):
            tile = jnp.broadcast_to(ext_ref[h, d:d + 1, :], (GRID_W, LANES))
            return pltpu.roll(tile, base, 1, stride=1, stride_axis=0)

        for d in range(2 * NA_KH - 2):
            pair = jnp.where(left, toeplitz(d, GRID_W + 1), toeplitz(d + 1, 1)) * LOG2E
            o_ref[h // 4, d, rows, :] = jnp.where(valid, pair, NEG_INF)


def _na_bias_tables(rpb):
    layers = rpb.shape[0]
    n_dr, n_dc = 2 * NA_KH - 1, 2 * NA_KW - 1
    lead = GRID_W - NA_KW
    ext = jnp.pad(rpb.astype(F32), ((0, 0), (0, 0), (0, 0), (lead, LANES - lead - n_dc)))
    table = (2, n_dr - 1, 4 * GRID_W, LANES)
    return pl.pallas_call(
        _na_bias_kernel, grid=(layers,),
        in_specs=[pl.BlockSpec((None, NA_HEADS, n_dr, LANES), lambda l: (l, 0, 0, 0))],
        out_specs=pl.BlockSpec((None,) + table, lambda l: (l, 0, 0, 0, 0)),
        out_shape=jax.ShapeDtypeStruct((layers,) + table, F32),
        compiler_params=_params("parallel"), name="na_bias_tables",
    )(ext)


def _na_kernel(q_ref, k_ref, v_ref, bias_ref, o_ref):
    lane = lax.broadcasted_iota(jnp.int32, (GRID_W, MXU_DIM), 1)
    head_mask = [(lane >= h * HEAD_DIM) & (lane < (h + 1) * HEAD_DIM) for h in range(4)]
    nk = NA_KH * GRID_W

    def rows(it, carry):
        chains = []
        for u in range(NA_ROWS_PER_STEP):
            r = it * NA_ROWS_PER_STEP + u
            rs = jnp.clip(r - NA_KH // 2, 0, GRID_ROWS - NA_KH)
            variant = rs - r + (NA_KH - 1)
            qrows = pl.ds(pl.multiple_of(r * GRID_W, GRID_W), GRID_W)
            krows = pl.ds(pl.multiple_of(rs * GRID_W, GRID_W), nk)
            q = q_ref[qrows, :]
            for g in range(2):
                gs = slice(g * MXU_DIM, (g + 1) * MXU_DIM)
                qg = q[:, gs]
                lhs = jnp.concatenate([jnp.where(m, qg, jnp.zeros_like(qg)) for m in head_mask], axis=0)
                bias = jnp.concatenate([bias_ref[g, variant + 2 * a] for a in range(NA_KH // 2)], axis=1)
                chains.append(dict(lhs=lhs, k=k_ref[krows, gs], v=v_ref[krows, gs], bias=bias,
                                   out=(qrows, gs)))
        for c in chains:
            c["s"] = lax.dot_general(c["lhs"], c["k"], (((1,), (1,)), ((), ())),
                                     preferred_element_type=F32) + c["bias"]
        for c in chains:
            c["p"] = jnp.exp2(c["s"] - jnp.max(c["s"], axis=-1, keepdims=True))
        for c in chains:
            c["l"] = jnp.sum(c["p"], axis=-1, keepdims=True)
        for c in chains:
            c["o"] = jnp.dot(c["p"].astype(BF16), c["v"], preferred_element_type=F32) / c["l"]
        for c in chains:
            og = jnp.zeros((GRID_W, MXU_DIM), F32)
            for h in range(4):
                og = og + jnp.where(head_mask[h], c["o"][h * GRID_W:(h + 1) * GRID_W], 0.0)
            o_ref[c["out"]] = og.astype(BF16)
        return carry

    lax.fori_loop(0, GRID_ROWS // NA_ROWS_PER_STEP, rows, 0)


def _na_attention(q, k, v, bias, layer):
    b = q.shape[0]
    seq = pl.BlockSpec((None, SEQ, NA_WIDTH), lambda i: (i, 0, 0))
    return pl.pallas_call(
        _na_kernel, grid=(b,),
        in_specs=[seq, seq, seq, pl.BlockSpec((None,) + bias.shape[1:], lambda i: (layer, 0, 0, 0, 0))],
        out_specs=seq, out_shape=jax.ShapeDtypeStruct((b, SEQ, NA_WIDTH), BF16),
        compiler_params=_params("parallel"), name="na_attention",
    )(q, k, v, bias)


def _sw_masks():
    nkeys = 3 * SW_BLOCK
    jk = np.arange(nkeys)[:, None]
    iq = np.arange(SW_BLOCK)[None, :]
    band = (jk >= iq) & (jk <= iq + 2 * SW_WINDOW)
    in_seq = [(jk >= SW_BLOCK), np.ones_like(band), (jk < 2 * SW_BLOCK)]
    return jnp.asarray(np.stack([np.where(band & ok, 0.0, NEG_INF) for ok in in_seq]), F32)


def _sw_kernel(sink_ref, q_ref, k_ref, v_ref, mask_ref, o_ref):
    nb = SEQ // SW_BLOCK
    first = lax.broadcasted_iota(jnp.int32, (SW_BLOCK, LANES), 1) < HEAD_DIM
    ones = jnp.ones((8, 3 * SW_BLOCK), BF16)

    def blocks(it, carry):
        chains = []
        for u in range(SW_BLOCKS_PER_STEP):
            n = it * SW_BLOCKS_PER_STEP + u
            mask = mask_ref[jnp.where(n == 0, 0, jnp.where(n == nb - 1, 2, 1))]
            starts = [pl.multiple_of(jnp.clip(n + d, 0, nb - 1) * SW_BLOCK, SW_BLOCK) for d in (-1, 0, 1)]
            qrows = pl.ds(pl.multiple_of(n * SW_BLOCK, SW_BLOCK), SW_BLOCK)
            q = q_ref[qrows, :]
            vv = jnp.concatenate([v_ref[pl.ds(s, SW_BLOCK), :] for s in starts], axis=0)
            vt = vv.astype(F32).T.astype(BF16)
            for kvh in range(SW_KV_HEADS):
                ks = slice(kvh * LANES, (kvh + 1) * LANES)
                kk = jnp.concatenate([k_ref[pl.ds(s, SW_BLOCK), ks] for s in starts], axis=0)
                parts, sinks = [], []
                for j in range(SW_GROUP):
                    h = kvh * SW_GROUP + j
                    qc = q[:, (h // 2) * LANES:(h // 2 + 1) * LANES]
                    keep = first if h % 2 == 0 else jnp.logical_not(first)
                    parts.append(jnp.where(keep, qc, jnp.zeros_like(qc)))
                    sinks.append(jnp.full((1, SW_BLOCK), sink_ref[h] * LOG2E, F32))
                vt_ones = jnp.concatenate([vt[kvh * HEAD_DIM:(kvh + 1) * HEAD_DIM], ones], axis=0)
                chains.append(dict(k=kk, vt=vt_ones, q=jnp.concatenate(parts, axis=0),
                                   sink=jnp.concatenate(sinks, axis=1), mask=mask, qrows=qrows, kvh=kvh))
        for c in chains:
            s = lax.dot_general(c["k"], c["q"], (((1,), (1,)), ((), ())), preferred_element_type=F32)
            prev_mask, next_mask = c["mask"][:SW_BLOCK], c["mask"][2 * SW_BLOCK:]
            cols = []
            for j in range(SW_GROUP):
                sj = s[:, j * SW_BLOCK:(j + 1) * SW_BLOCK]
                cols.append(jnp.concatenate([sj[:SW_BLOCK] + prev_mask, sj[SW_BLOCK:2 * SW_BLOCK],
                                             sj[2 * SW_BLOCK:] + next_mask], axis=0))
            c["s"] = jnp.concatenate(cols, axis=1)
        for c in chains:
            c["m"] = jnp.maximum(jnp.max(c["s"], axis=0, keepdims=True), c["sink"])
        for c in chains:
            c["p"] = jnp.exp2(c["s"] - c["m"]).astype(BF16)
        for c in chains:
            o = jnp.dot(c["vt"], c["p"], preferred_element_type=F32)
            denom = o[HEAD_DIM:HEAD_DIM + 1] + jnp.exp2(c["sink"] - c["m"])
            c["o"] = o[:HEAD_DIM] / denom
        for c in chains:
            for pair in range(SW_GROUP // 2):
                even = c["o"][:, (2 * pair) * SW_BLOCK:(2 * pair + 1) * SW_BLOCK]
                odd = c["o"][:, (2 * pair + 1) * SW_BLOCK:(2 * pair + 2) * SW_BLOCK]
                col = (c["kvh"] * (SW_GROUP // 2) + pair) * LANES
                o_ref[c["qrows"], col:col + LANES] = jnp.concatenate([even, odd], axis=0).T.astype(BF16)
        return carry

    lax.fori_loop(0, nb // SW_BLOCKS_PER_STEP, blocks, 0)


def _sw_attention(q, k2, v, sink):
    b = q.shape[0]
    masks = _sw_masks()
    seq = lambda w: pl.BlockSpec((None, SEQ, w), lambda i: (i, 0, 0))
    return pl.pallas_call(
        _sw_kernel, grid=(b,),
        in_specs=[pl.BlockSpec(memory_space=pltpu.SMEM), seq(SW_WIDTH), seq(2 * SW_KV_WIDTH),
                  seq(SW_KV_WIDTH), pl.BlockSpec(masks.shape, lambda i: (0, 0, 0))],
        out_specs=seq(SW_WIDTH), out_shape=jax.ShapeDtypeStruct((b, SEQ, SW_WIDTH), BF16),
        compiler_params=_params("parallel"), name="sw_attention",
    )(sink.astype(F32), q, k2, v, masks)


def _merge_kernel(x_ref, yna_ref, ysw_ref, gate_ref, wna_ref, wsw_ref, wout_ref, g_ref, b_ref, o_ref):
    part = x_ref.shape[0] // MERGE_PARTS
    parts = [slice(p * part, (p + 1) * part) for p in range(MERGE_PARTS)]
    dot = functools.partial(jnp.dot, preferred_element_type=F32)
    branches = [(dot(yna_ref[r, :], wna_ref[...]), dot(ysw_ref[r, :], wsw_ref[...])) for r in parts]
    mixed = [(gate_ref[r, :D_MODEL].astype(F32) * a + gate_ref[r, D_MODEL:].astype(F32) * s).astype(BF16)
             for r, (a, s) in zip(parts, branches)]
    summed = [DEEPNORM_ALPHA * x_ref[r, :] + dot(m, wout_ref[...]) for r, m in zip(parts, mixed)]
    for r, z in zip(parts, summed):
        o_ref[r, :] = _layer_norm(z, g_ref[...], b_ref[...])


def _merge(x, y_na, y_sw, gates, w_na, w_sw, w_out, g, b):
    n = x.shape[0]
    row = lambda w: pl.BlockSpec((PROJ_ROW_TILE, w), lambda i: (i, 0))
    full = lambda a: pl.BlockSpec(a.shape, lambda i: (0,) * a.ndim)
    g2, b2 = g.reshape(1, -1), b.reshape(1, -1)
    return pl.pallas_call(
        _merge_kernel, grid=(n // PROJ_ROW_TILE,),
        in_specs=[row(D_MODEL), row(NA_WIDTH), row(SW_WIDTH), row(2 * D_MODEL),
                  full(w_na), full(w_sw), full(w_out), full(g2), full(b2)],
        out_specs=row(D_MODEL), out_shape=jax.ShapeDtypeStruct((n, D_MODEL), F32),
        compiler_params=_params("parallel"), name="merge_ln1",
    )(x, y_na, y_sw, gates, w_na, w_sw, w_out, g2, b2)


def _router_kernel(x_ref, w_ref, cw_ref, pos_ref, post_ref, cnt_ref, zero_ref):
    zero_ref[...] = jnp.zeros_like(zero_ref)

    blocks = [slice(p * ROW_TILE, (p + 1) * ROW_TILE) for p in range(ROUTER_BLOCKS)]
    dot = functools.partial(jnp.dot, preferred_element_type=F32)
    w = w_ref[...]
    wh = w.astype(BF16)
    wl = (w - wh.astype(F32)).astype(BF16)
    xs = [x_ref[r, :] for r in blocks]
    xh = [x.astype(BF16) for x in xs]
    xl = [(x - h.astype(F32)).astype(BF16) for x, h in zip(xs, xh)]
    logits = [dot(h, wh) + dot(l, wh) + dot(h, wl) for h, l in zip(xh, xl)]
    lane = lax.broadcasted_iota(jnp.int32, (ROW_TILE, LANES), 1).astype(F32)
    lg = [jnp.where(lane < N_EXPERTS, t, -jnp.inf) for t in logits]
    m1 = [jnp.max(t, axis=-1, keepdims=True) for t in lg]
    i1 = [jnp.min(jnp.where(t == m, lane, float(LANES)), axis=-1, keepdims=True) for t, m in zip(lg, m1)]
    lg2 = [jnp.where(lane == i, -jnp.inf, t) for t, i in zip(lg, i1)]
    m2 = [jnp.max(t, axis=-1, keepdims=True) for t in lg2]
    i2 = [jnp.min(jnp.where(t == m, lane, float(LANES)), axis=-1, keepdims=True) for t, m in zip(lg2, m2)]
    e2 = [jnp.exp(b - a) for a, b in zip(m1, m2)]
    cw = [jnp.where(lane == a, 1.0 / (1.0 + e), 0.0) + jnp.where(lane == b, e / (1.0 + e), 0.0)
          for a, b, e in zip(i1, i2, e2)]

    t = ROW_TILE
    tri = jnp.where(lax.broadcasted_iota(jnp.int32, (t, t), 0) <= lax.broadcasted_iota(jnp.int32, (t, t), 1),
                    1.0, 0.0).astype(BF16)
    sel = [jnp.where(c.T[:N_EXPERTS] > 0.0, 1.0, 0.0) for c in cw]
    incl = [dot(s.astype(BF16), tri) for s in sel]
    post = [jnp.where(s > 0.0, n - 1.0, -1.0) for s, n in zip(sel, incl)]
    unrouted = jnp.full((LANES - N_EXPERTS, t), -1.0, F32)
    for p, r in enumerate(blocks):
        cw_ref[r, :] = cw[p]
        post_ref[:, r] = post[p]
        pos_ref[r, :] = jnp.concatenate([post[p], unrouted], axis=0).T
        cnt_ref[p * N_EXPERTS:(p + 1) * N_EXPERTS, :] = jnp.broadcast_to(incl[p][:, t - 1:t], (N_EXPERTS, LANES))


def _router(x, w_router, n_rows):
    n = x.shape[0]
    nblk = n // ROW_TILE
    step = ROUTER_BLOCKS * ROW_TILE
    steps = n // step
    assert n_rows % (steps * SEG_ALIGN) == 0
    w = jnp.zeros((D_MODEL, LANES), F32).at[:, :N_EXPERTS].set(w_router)
    tok = pl.BlockSpec((step, LANES), lambda i: (i, 0))
    cw, pos, post, cnt, zeros = pl.pallas_call(
        _router_kernel, grid=(steps,),
        in_specs=[pl.BlockSpec((step, D_MODEL), lambda i: (i, 0)),
                  pl.BlockSpec((D_MODEL, LANES), lambda i: (0, 0))],
        out_specs=[tok, tok, pl.BlockSpec((N_EXPERTS, step), lambda i: (0, i)),
                   pl.BlockSpec((ROUTER_BLOCKS * N_EXPERTS, LANES), lambda i: (i, 0)),
                   pl.BlockSpec((n_rows // steps, D_MODEL), lambda i: (i, 0))],
        out_shape=[jax.ShapeDtypeStruct((n, LANES), F32), jax.ShapeDtypeStruct((n, LANES), F32),
                   jax.ShapeDtypeStruct((N_EXPERTS, n), F32),
                   jax.ShapeDtypeStruct((nblk * N_EXPERTS, LANES), F32),
                   jax.ShapeDtypeStruct((n_rows, D_MODEL), BF16)],
        compiler_params=_params("parallel"), name="router_top2",
    )(x, w)
    return cw, pos, post, cnt[:, 0].astype(jnp.int32).reshape(nblk, N_EXPERTS), zeros


def _moe_layout(cnt, n_tiles):
    seg = (cnt + SEG_ALIGN - 1) // SEG_ALIGN * SEG_ALIGN
    rows_e = jnp.sum(seg, axis=0)
    tiles_e = (rows_e + MOE_CHUNK + MOE_ROW_TILE - 1) // MOE_ROW_TILE
    tile_end = jnp.cumsum(tiles_e)
    tile_off = tile_end - tiles_e
    seg_off = (tile_off * MOE_ROW_TILE)[None, :] + jnp.cumsum(seg, axis=0) - seg
    tile = jnp.arange(n_tiles, dtype=jnp.int32)
    tile_expert = jnp.minimum(jnp.sum(tile[:, None] >= tile_end[None, :], axis=1), N_EXPERTS - 1)
    occupied = jnp.clip(rows_e[tile_expert] - (tile - tile_off[tile_expert]) * MOE_ROW_TILE, 0, MOE_ROW_TILE)
    as_i32 = lambda t: t.astype(jnp.int32)
    return as_i32(seg_off).reshape(-1), as_i32(tile_expert), as_i32(occupied)


def _chunk_copies(off_ref, cnt_ref, block, hbm_ref, vmem_ref, sems, to_hbm):
    slot = block % 2
    out = []
    for e in range(N_EXPERTS):
        c = cnt_ref[block * N_EXPERTS + e]
        off = pl.multiple_of(off_ref[block * N_EXPERTS + e], SEG_ALIGN)
        for k in range(ROW_TILE // MOE_CHUNK):
            idx = k * N_EXPERTS + e
            rows = hbm_ref.at[pl.ds(off + k * MOE_CHUNK, MOE_CHUNK)]
            staged = vmem_ref.at[slot, idx]
            src, dst = (staged, rows) if to_hbm else (rows, staged)
            out.append((c > k * MOE_CHUNK, e, k, idx, pltpu.make_async_copy(src, dst, sems.at[slot, idx])))
    return out


def _dispatch_kernel(off_ref, cnt_ref, x_ref, post_ref, xs_in_ref, xs_ref, stage_ref, sems):
    del xs_in_ref
    b, nblk = pl.program_id(0), pl.num_programs(0)
    copies_of = lambda blk: _chunk_copies(off_ref, cnt_ref, blk, xs_ref, stage_ref, sems, to_hbm=True)

    def wait_all(blk):
        for pred, _, _, _, copy in copies_of(blk):
            pl.when(pred)(copy.wait)

    xb = x_ref[...].astype(BF16)
    slot_row = lax.broadcasted_iota(jnp.int32, (MOE_CHUNK, ROW_TILE), 0).astype(F32)

    def gather_matrix(e, k):
        return jnp.where(post_ref[e:e + 1, :] == slot_row + float(k * MOE_CHUNK), 1.0, 0.0).astype(BF16)

    firsts = jnp.concatenate([gather_matrix(e, 0) for e in range(N_EXPERTS)], axis=0)
    stage_ref[b % 2, 0:N_EXPERTS] = jnp.dot(firsts, xb, preferred_element_type=F32).astype(BF16).reshape(
        N_EXPERTS, MOE_CHUNK, D_MODEL)
    copies = copies_of(b)
    for pred, e, k, idx, _ in copies:
        if k > 0:
            @pl.when(pred)
            def _(e=e, k=k, idx=idx):
                stage_ref[b % 2, idx] = jnp.dot(gather_matrix(e, k), xb, preferred_element_type=F32).astype(BF16)
    pl.when(b >= 1)(lambda: wait_all(b - 1))
    for pred, _, _, _, copy in copies:
        pl.when(pred)(copy.start)
    pl.when(b == nblk - 1)(lambda: wait_all(b))


def _dispatch(x, post, seg_off, cnt, zeros):
    n = x.shape[0]
    n_chunks = N_EXPERTS * (ROW_TILE // MOE_CHUNK)
    grid_spec = pltpu.PrefetchScalarGridSpec(
        num_scalar_prefetch=2, grid=(n // ROW_TILE,),
        in_specs=[pl.BlockSpec((ROW_TILE, D_MODEL), lambda i, *_: (i, 0)),
                  pl.BlockSpec((N_EXPERTS, ROW_TILE), lambda i, *_: (0, i)),
                  pl.BlockSpec(memory_space=pl.ANY)],
        out_specs=pl.BlockSpec(memory_space=pl.ANY),
        scratch_shapes=[pltpu.VMEM((2, n_chunks, MOE_CHUNK, D_MODEL), BF16),
                        pltpu.SemaphoreType.DMA((2, n_chunks))])
    return pl.pallas_call(
        _dispatch_kernel, grid_spec=grid_spec,
        out_shape=jax.ShapeDtypeStruct(zeros.shape, BF16),
        input_output_aliases={4: 0},
        compiler_params=_params("arbitrary"), name="moe_dispatch",
    )(seg_off, cnt.reshape(-1), x, post, zeros)


def _swiglu_partial(xb, wg, wu, wd):
    hg = jnp.dot(xb, wg, preferred_element_type=F32)
    hu = jnp.dot(xb, wu, preferred_element_type=F32)
    h = hg * _sigmoid(hg) * hu
    return jnp.dot(h.astype(BF16), wd, preferred_element_type=F32)


def _moe_ffn_kernel(expert_ref, occ_ref, x_ref, wg_ref, wu_ref, wd_ref, o_ref, acc_ref):
    del expert_ref
    i, j = pl.program_id(0), pl.program_id(1)
    occupied = occ_ref[i]

    @pl.when(j == 0)
    def _():
        acc_ref[...] = jnp.zeros_like(acc_ref)

    def weights():
        return wg_ref[...].astype(BF16), wu_ref[...].astype(BF16), wd_ref[...].astype(BF16)

    chunks = (occupied + MOE_CHUNK - 1) // MOE_CHUNK
    for n in range(1, MOE_ROW_TILE // MOE_CHUNK + 1):
        rows = slice(0, n * MOE_CHUNK)

        @pl.when(chunks == n)
        def _(rows=rows):
            acc_ref[rows, :] += _swiglu_partial(x_ref[rows, :], *weights())

    @pl.when(j == pl.num_programs(1) - 1)
    def _():
        o_ref[...] = acc_ref[...].astype(BF16)


def _moe_ffn(xs, tile_expert, occupied, w_gate, w_up, w_down):
    n_rows = xs.shape[0]
    d_ff = w_gate.shape[-1]
    row = pl.BlockSpec((MOE_ROW_TILE, D_MODEL), lambda i, j, *_: (i, 0))
    up = pl.BlockSpec((None, D_MODEL, MOE_FF_TILE), lambda i, j, ex, occ: (ex[i], 0, j))
    down = pl.BlockSpec((None, MOE_FF_TILE, D_MODEL), lambda i, j, ex, occ: (ex[i], j, 0))
    grid_spec = pltpu.PrefetchScalarGridSpec(
        num_scalar_prefetch=2, grid=(n_rows // MOE_ROW_TILE, d_ff // MOE_FF_TILE),
        in_specs=[row, up, up, down], out_specs=row,
        scratch_shapes=[pltpu.VMEM((MOE_ROW_TILE, D_MODEL), F32)])
    return pl.pallas_call(
        _moe_ffn_kernel, grid_spec=grid_spec,
        out_shape=jax.ShapeDtypeStruct((n_rows, D_MODEL), BF16),
        compiler_params=_params("parallel", "arbitrary"), name="moe_swiglu",
    )(tile_expert, occupied, xs, w_gate, w_up, w_down)


def _ffn_kernel(x_ref, wg_ref, wu_ref, wd_ref, g_ref, b_ref, o_ref):
    x = x_ref[...]
    y = _swiglu_partial(x.astype(BF16), wg_ref[...], wu_ref[...], wd_ref[...])
    o_ref[...] = _layer_norm(DEEPNORM_ALPHA * x + y, g_ref[...], b_ref[...])


def _ffn(x, w_gate, w_up, w_down, g, b):
    n = x.shape[0]
    row = pl.BlockSpec((PROJ_ROW_TILE, D_MODEL), lambda i: (i, 0))
    const = lambda a: pl.BlockSpec(a.shape, lambda i: (0,) * a.ndim, pipeline_mode=pl.Buffered(1))
    g2, b2 = g.reshape(1, -1), b.reshape(1, -1)
    return pl.pallas_call(
        _ffn_kernel, grid=(n // PROJ_ROW_TILE,),
        in_specs=[row, const(w_gate), const(w_up), const(w_down), const(g2), const(b2)], out_specs=row,
        out_shape=jax.ShapeDtypeStruct((n, D_MODEL), F32),
        compiler_params=_params("parallel"), name="swiglu_ln2",
    )(x, w_gate, w_up, w_down, g2, b2)


def _combine_kernel(off_ref, cnt_ref, x_ref, cw_ref, pos_ref, g_ref, b_ref, ys_ref, o_ref,
                    buf_ref, acc_ref, sems):
    b, nblk = pl.program_id(0), pl.num_programs(0)
    copies_of = lambda blk: _chunk_copies(off_ref, cnt_ref, blk, ys_ref, buf_ref, sems, to_hbm=False)

    def start_all(blk):
        for pred, _, k, _, copy in copies_of(blk):
            if k == 0:
                copy.start()
            else:
                pl.when(pred)(copy.start)

    pl.when(b == 0)(lambda: start_all(b))
    pl.when(b + 1 < nblk)(lambda: start_all(b + 1))

    lane = lax.broadcasted_iota(jnp.int32, (ROW_TILE, LANES), 1)
    slot_col = lax.broadcasted_iota(jnp.int32, (ROW_TILE, MOE_CHUNK), 1).astype(F32)
    column = lambda ref, e: jnp.sum(jnp.where(lane == e, ref[...], 0.0), axis=-1, keepdims=True)
    weight = [column(cw_ref, e) for e in range(N_EXPERTS)]
    pos = [column(pos_ref, e) for e in range(N_EXPERTS)]

    def scatter_matrix(e, k):
        return jnp.where(pos[e] == slot_col + float(k * MOE_CHUNK), weight[e], 0.0).astype(BF16)

    first = jnp.concatenate([scatter_matrix(e, 0) for e in range(N_EXPERTS)], axis=1)
    copies = copies_of(b)
    for _, _, k, _, copy in copies:
        if k == 0:
            copy.wait()
    firsts = buf_ref[b % 2, 0:N_EXPERTS].reshape(N_EXPERTS * MOE_CHUNK, D_MODEL)
    acc_ref[...] = jnp.dot(first, firsts, preferred_element_type=F32)
    for pred, e, k, idx, copy in copies:
        if k > 0:
            @pl.when(pred)
            def _(e=e, k=k, idx=idx, copy=copy):
                copy.wait()
                acc_ref[...] += jnp.dot(scatter_matrix(e, k), buf_ref[b % 2, idx], preferred_element_type=F32)

    o_ref[...] = _layer_norm(DEEPNORM_ALPHA * x_ref[...] + acc_ref[...], g_ref[...], b_ref[...])


def _combine(x, ys, cw, pos, seg_off, cnt, g, b):
    n = x.shape[0]
    n_chunks = N_EXPERTS * (ROW_TILE // MOE_CHUNK)
    tok = lambda w: pl.BlockSpec((ROW_TILE, w), lambda i, *_: (i, 0))
    vec = pl.BlockSpec((1, D_MODEL), lambda i, *_: (0, 0))
    grid_spec = pltpu.PrefetchScalarGridSpec(
        num_scalar_prefetch=2, grid=(n // ROW_TILE,),
        in_specs=[tok(D_MODEL), tok(LANES), tok(LANES), vec, vec, pl.BlockSpec(memory_space=pl.ANY)],
        out_specs=tok(D_MODEL),
        scratch_shapes=[pltpu.VMEM((2, n_chunks, MOE_CHUNK, D_MODEL), BF16),
                        pltpu.VMEM((ROW_TILE, D_MODEL), F32),
                        pltpu.SemaphoreType.DMA((2, n_chunks))])
    return pl.pallas_call(
        _combine_kernel, grid_spec=grid_spec,
        out_shape=jax.ShapeDtypeStruct((n, D_MODEL), F32),
        compiler_params=_params("arbitrary"), name="moe_combine_ln2",
    )(seg_off, cnt.reshape(-1), x, cw, pos, g.reshape(1, -1), b.reshape(1, -1), ys)


def _moe(x, w_router, w_gate, w_up, w_down, g, b):
    n = x.shape[0]
    nblk = n // ROW_TILE
    max_rows = 2 * n + nblk * N_EXPERTS * (SEG_ALIGN - 1) + N_EXPERTS * (MOE_CHUNK + MOE_ROW_TILE - 1)
    n_tiles = max_rows // MOE_ROW_TILE
    cw, pos, post, cnt, zeros = _router(x, w_router, n_tiles * MOE_ROW_TILE)
    seg_off, tile_expert, occupied = _moe_layout(cnt, n_tiles)
    xs = _dispatch(x, post, seg_off, cnt, zeros)
    ys = _moe_ffn(xs, tile_expert, occupied, w_gate, w_up, w_down)
    return _combine(x, ys, cw, pos, seg_off, cnt, g, b)


def kernel(x, emb_ln_g, emb_ln_b, w_in, b_gate, na_rpb, sw_sink, w_branch_na, w_branch_sw, w_out,
           ln1_g, ln1_b, ffn_w_gate, ffn_w_up, ffn_w_down, moe_router, moe_w_gate, moe_w_up,
           moe_w_down, ln2_g, ln2_b):
    batch, seq, d = x.shape
    assert (seq, d) == (SEQ, D_MODEL)
    n = batch * seq
    tables = _rotary_tables()
    na_bias = _na_bias_tables(na_rpb)
    h = x.reshape(n, d)
    for layer in range(DEPTH):
        h, q_na, k_na, v_na, q_sw, k_sw, v_sw, gates = _proj(
            h, w_in[layer].astype(BF16), b_gate[layer], tables,
            embed_ln=(emb_ln_g, emb_ln_b) if layer == 0 else None)
        seq3 = lambda t: t.reshape(batch, seq, t.shape[-1])
        y_na = _na_attention(seq3(q_na), seq3(k_na), seq3(v_na), na_bias, layer)
        y_sw = _sw_attention(seq3(q_sw), seq3(k_sw), seq3(v_sw), sw_sink[layer])
        h = _merge(h, y_na.reshape(n, -1), y_sw.reshape(n, -1), gates,
                   w_branch_na[layer].astype(BF16), w_branch_sw[layer].astype(BF16),
                   w_out[layer].astype(BF16), ln1_g[layer], ln1_b[layer])
        i = layer // 2
        if layer % 2 == 0:
            h = _ffn(h, ffn_w_gate[i].astype(BF16), ffn_w_up[i].astype(BF16),
                     ffn_w_down[i].astype(BF16), ln2_g[layer], ln2_b[layer])
        else:
            h = _moe(h, moe_router[i], moe_w_gate[i], moe_w_up[i], moe_w_down[i],
                     ln2_g[layer], ln2_b[layer])
    return h.reshape(batch, seq, d)
```

```python
import functools

import numpy as np
import jax
import jax.numpy as jnp
from jax import lax
from jax.experimental import pallas as pl
from jax.experimental.pallas import tpu as pltpu

F32 = jnp.float32
BF16 = jnp.bfloat16

D_MODEL = 1024
SEQ = 2048
DEPTH = 2
HEAD_DIM = 64
NA_HEADS = 8
NA_WIDTH = NA_HEADS * HEAD_DIM
GRID_W = 64
GRID_ROWS = SEQ // GRID_W
NA_KH = 8
NA_KW = 16
SW_HEADS = 8
SW_KV_HEADS = 2
SW_GROUP = SW_HEADS // SW_KV_HEADS
SW_WIDTH = SW_HEADS * HEAD_DIM
SW_KV_WIDTH = SW_KV_HEADS * HEAD_DIM
SW_WINDOW = 128
SW_BLOCK = 128
ROT_DIM = HEAD_DIM // 4
ROPE_THETA = 500000.0
OFF_QNA = NA_WIDTH
OFF_KNA = 2 * NA_WIDTH
OFF_VNA = 3 * NA_WIDTH
OFF_QSW = OFF_VNA + SW_WIDTH
OFF_KSW = OFF_QSW + SW_KV_WIDTH
OFF_VSW = OFF_KSW + SW_KV_WIDTH
PROJ_COLS = OFF_VSW + 2 * D_MODEL
N_EXPERTS = 8
DEEPNORM_ALPHA = (2 * DEPTH) ** 0.25
LN_EPS = 1e-5
NEG_INF = -1e30
QK_SCALE = HEAD_DIM ** -0.5
LOG2E = float(np.log2(np.e))

LANES = 128
MXU_DIM = 256
V7X_VMEM_BYTES = 64 * 1024 * 1024
VMEM_LIMIT = V7X_VMEM_BYTES * 7 // 8

PROJ_ROW_TILE = 1024
ROW_TILE = 512
MOE_ROW_TILE = 1536
MOE_FF_TILE = 512
MOE_CHUNK = 256
MOE_COMPUTE_ROWS = 128
SEG_ALIGN = 16
NA_ROWS_PER_STEP = 16
SW_BLOCKS_PER_STEP = 16
MERGE_PARTS = 4
ROUTER_BLOCKS = 4


def _layer_norm(z, g, b):
    mu = jnp.mean(z, axis=-1, keepdims=True)
    d = z - mu
    var = jnp.mean(d * d, axis=-1, keepdims=True)
    return d * lax.rsqrt(var + LN_EPS) * g + b


def _sigmoid(z):
    return 1.0 / (1.0 + jnp.exp(-z))


def _params(*sem):
    return pltpu.CompilerParams(dimension_semantics=sem, vmem_limit_bytes=VMEM_LIMIT)


def _rotary_tables():
    half = ROT_DIM // 2
    inv_freq = 1.0 / (ROPE_THETA ** (jnp.arange(0, ROT_DIM, 2, dtype=F32) / ROT_DIM))
    ang = jnp.arange(SEQ, dtype=jnp.int32).astype(F32)[:, None] * inv_freq[None, :]
    cos, sin = jnp.cos(ang), jnp.sin(ang)
    ones = jnp.ones((SEQ, HEAD_DIM - ROT_DIM), F32)
    zeros = jnp.zeros((SEQ, HEAD_DIM - ROT_DIM), F32)
    zh = jnp.zeros((SEQ, half), F32)
    cos_h = jnp.concatenate([cos, cos, ones], axis=1)
    sa_h = jnp.concatenate([-sin, zh, zeros], axis=1)
    sb_h = jnp.concatenate([zh, sin, zeros], axis=1)
    two = lambda t: jnp.concatenate([t, t], axis=1)
    return two(cos_h), two(sa_h), two(sb_h)


def _proj_kernel(*refs, embed_ln):
    if embed_ln:
        x_ref, g_ref, b_ref, w_ref, bg_ref, cos_ref, sa_ref, sb_ref, xn_ref, *outs = refs
        x = _layer_norm(x_ref[...], g_ref[...], b_ref[...])
        xn_ref[...] = x
    else:
        x_ref, w_ref, bg_ref, cos_ref, sa_ref, sb_ref, *outs = refs
        x = x_ref[...]
    qna_ref, kna_ref, vna_ref, qsw_ref, ksw_ref, vsw_ref, gate_ref = outs
    xb = x.astype(BF16)

    def mm(lo, hi):
        return jnp.dot(xb, w_ref[:, lo:hi], preferred_element_type=F32)

    qna_ref[...] = (mm(0, OFF_QNA) * (QK_SCALE * LOG2E)).astype(BF16)
    kna_ref[...] = mm(OFF_QNA, OFF_KNA).astype(BF16)
    vna_ref[...] = mm(OFF_KNA, OFF_VNA).astype(BF16)

    cos, sa, sb = cos_ref[...], sa_ref[...], sb_ref[...]
    half = ROT_DIM // 2

    def rot(t):
        return t * cos + pltpu.roll(t, LANES - half, 1) * sa + pltpu.roll(t, half, 1) * sb

    q = mm(OFF_VNA, OFF_QSW)
    for c in range(SW_WIDTH // LANES):
        sl = slice(c * LANES, (c + 1) * LANES)
        qsw_ref[:, sl] = (rot(q[:, sl]) * (QK_SCALE * LOG2E)).astype(BF16)

    kv = mm(OFF_QSW, OFF_VSW)
    k = rot(kv[:, :SW_KV_WIDTH])
    v = kv[:, SW_KV_WIDTH:]
    first = lax.broadcasted_iota(jnp.int32, k.shape, 1) < HEAD_DIM

    def dup(t):
        r = pltpu.roll(t, HEAD_DIM, 1)
        return jnp.concatenate([jnp.where(first, t, r), jnp.where(first, r, t)], axis=1)

    ksw_ref[...] = dup(k).astype(BF16)
    vsw_ref[...] = v.astype(BF16)

    gw = 512
    for c in range(2 * D_MODEL // gw):
        z = mm(OFF_VSW + c * gw, OFF_VSW + (c + 1) * gw) + bg_ref[:, c * gw:(c + 1) * gw]
        gate_ref[:, c * gw:(c + 1) * gw] = _sigmoid(z).astype(BF16)


def _proj(x, w_bf, b_gate, tables, embed_ln=None):
    n = x.shape[0]
    tiles_per_seq = SEQ // PROJ_ROW_TILE
    row = lambda w: pl.BlockSpec((PROJ_ROW_TILE, w), lambda i: (i, 0))
    full = lambda a: pl.BlockSpec(a.shape, lambda i: (0,) * a.ndim)
    tab = pl.BlockSpec((PROJ_ROW_TILE, LANES), lambda i: (i % tiles_per_seq, 0))
    bg = b_gate.reshape(1, -1)
    widths = (NA_WIDTH, NA_WIDTH, NA_WIDTH, SW_WIDTH, 2 * SW_KV_WIDTH, SW_KV_WIDTH, 2 * D_MODEL)
    ln = [t.reshape(1, -1) for t in embed_ln] if embed_ln else []
    out_specs = [row(w) for w in widths]
    out_shape = [jax.ShapeDtypeStruct((n, w), BF16) for w in widths]
    if embed_ln:
        out_specs = [row(D_MODEL)] + out_specs
        out_shape = [jax.ShapeDtypeStruct((n, D_MODEL), F32)] + out_shape
    outs = pl.pallas_call(
        functools.partial(_proj_kernel, embed_ln=bool(embed_ln)), grid=(n // PROJ_ROW_TILE,),
        in_specs=[row(D_MODEL)] + [full(t) for t in ln] + [full(w_bf), full(bg), tab, tab, tab],
        out_specs=out_specs, out_shape=out_shape,
        compiler_params=_params("parallel"), name="in_proj",
    )(x, *ln, w_bf, bg, *tables)
    return outs if embed_ln else [x] + list(outs)


def _na_bias_kernel(ext_ref, o_ref):
    qc = lax.broadcasted_iota(jnp.int32, (GRID_W, LANES), 0)
    lane = lax.broadcasted_iota(jnp.int32, (GRID_W, LANES), 1)
    kc = lane % GRID_W
    qcs = jnp.clip(qc - NA_KW // 2, 0, GRID_W - NA_KW)
    valid = (kc >= qcs) & (kc < qcs + NA_KW)
    left = lane < GRID_W
    for h in range(NA_HEADS):
        rows = slice((h % 4) * GRID_W, (h % 4 + 1) * GRID_W)

        def toeplitz(d, base):
            tile = jnp.broadcast_to(ext_ref[h, d:d + 1, :], (GRID_W, LANES))
            return pltpu.roll(tile, base, 1, stride=1, stride_axis=0)

        for d in range(2 * NA_KH - 2):
            pair = jnp.where(left, toeplitz(d, GRID_W + 1), toeplitz(d + 1, 1)) * LOG2E
            o_ref[h // 4, d, rows, :] = jnp.where(valid, pair, NEG_INF)


def _na_bias_tables(rpb):
    layers = rpb.shape[0]
    n_dr, n_dc = 2 * NA_KH - 1, 2 * NA_KW - 1
    lead = GRID_W - NA_KW
    ext = jnp.pad(rpb.astype(F32), ((0, 0), (0, 0), (0, 0), (lead, LANES - lead - n_dc)))
    table = (2, n_dr - 1, 4 * GRID_W, LANES)
    return pl.pallas_call(
        _na_bias_kernel, grid=(layers,),
        in_specs=[pl.BlockSpec((None, NA_HEADS, n_dr, LANES), lambda l: (l, 0, 0, 0))],
        out_specs=pl.BlockSpec((None,) + table, lambda l: (l, 0, 0, 0, 0)),
        out_shape=jax.ShapeDtypeStruct((layers,) + table, F32),
        compiler_params=_params("parallel"), name="na_bias_tables",
    )(ext)


def _na_kernel(q_ref, k_ref, v_ref, bias_ref, o_ref):
    lane = lax.broadcasted_iota(jnp.int32, (GRID_W, MXU_DIM), 1)
    head_mask = [(lane >= h * HEAD_DIM) & (lane < (h + 1) * HEAD_DIM) for h in range(4)]
    nk = NA_KH * GRID_W

    def rows(it, carry):
        chains = []
        for u in range(NA_ROWS_PER_STEP):
            r = it * NA_ROWS_PER_STEP + u
            rs = jnp.clip(r - NA_KH // 2, 0, GRID_ROWS - NA_KH)
            variant = rs - r + (NA_KH - 1)
            qrows = pl.ds(pl.multiple_of(r * GRID_W, GRID_W), GRID_W)
            krows = pl.ds(pl.multiple_of(rs * GRID_W, GRID_W), nk)
            q = q_ref[qrows, :]
            for g in range(2):
                gs = slice(g * MXU_DIM, (g + 1) * MXU_DIM)
                qg = q[:, gs]
                lhs = jnp.concatenate([jnp.where(m, qg, jnp.zeros_like(qg)) for m in head_mask], axis=0)
                bias = jnp.concatenate([bias_ref[g, variant + 2 * a] for a in range(NA_KH // 2)], axis=1)
                chains.append(dict(lhs=lhs, k=k_ref[krows, gs], v=v_ref[krows, gs], bias=bias,
                                   out=(qrows, gs)))
        for c in chains:
            c["s"] = lax.dot_general(c["lhs"], c["k"], (((1,), (1,)), ((), ())),
                                     preferred_element_type=F32) + c["bias"]
        for c in chains:
            c["p"] = jnp.exp2(c["s"] - jnp.max(c["s"], axis=-1, keepdims=True))
        for c in chains:
            c["l"] = jnp.sum(c["p"], axis=-1, keepdims=True)
        for c in chains:
            c["o"] = jnp.dot(c["p"].astype(BF16), c["v"], preferred_element_type=F32) / c["l"]
        for c in chains:
            og = jnp.zeros((GRID_W, MXU_DIM), F32)
            for h in range(4):
                og = og + jnp.where(head_mask[h], c["o"][h * GRID_W:(h + 1) * GRID_W], 0.0)
            o_ref[c["out"]] = og.astype(BF16)
        return carry

    lax.fori_loop(0, GRID_ROWS // NA_ROWS_PER_STEP, rows, 0)


def _na_attention(q, k, v, bias, layer):
    b = q.shape[0]
    seq = pl.BlockSpec((None, SEQ, NA_WIDTH), lambda i: (i, 0, 0))
    return pl.pallas_call(
        _na_kernel, grid=(b,),
        in_specs=[seq, seq, seq, pl.BlockSpec((None,) + bias.shape[1:], lambda i: (layer, 0, 0, 0, 0))],
        out_specs=seq, out_shape=jax.ShapeDtypeStruct((b, SEQ, NA_WIDTH), BF16),
        compiler_params=_params("parallel"), name="na_attention",
    )(q, k, v, bias)


def _sw_masks():
    nkeys = 3 * SW_BLOCK
    jk = np.arange(nkeys)[:, None]
    iq = np.arange(SW_BLOCK)[None, :]
    band = (jk >= iq) & (jk <= iq + 2 * SW_WINDOW)
    in_seq = [(jk >= SW_BLOCK), np.ones_like(band), (jk < 2 * SW_BLOCK)]
    return jnp.asarray(np.stack([np.where(band & ok, 0.0, NEG_INF) for ok in in_seq]), F32)


def _sw_kernel(sink_ref, q_ref, k_ref, v_ref, mask_ref, o_ref):
    nb = SEQ // SW_BLOCK
    first = lax.broadcasted_iota(jnp.int32, (SW_BLOCK, LANES), 1) < HEAD_DIM
    ones = jnp.ones((8, 3 * SW_BLOCK), BF16)

    def blocks(it, carry):
        chains = []
        for u in range(SW_BLOCKS_PER_STEP):
            n = it * SW_BLOCKS_PER_STEP + u
            mask = mask_ref[jnp.where(n == 0, 0, jnp.where(n == nb - 1, 2, 1))]
            starts = [pl.multiple_of(jnp.clip(n + d, 0, nb - 1) * SW_BLOCK, SW_BLOCK) for d in (-1, 0, 1)]
            qrows = pl.ds(pl.multiple_of(n * SW_BLOCK, SW_BLOCK), SW_BLOCK)
            q = q_ref[qrows, :]
            vv = jnp.concatenate([v_ref[pl.ds(s, SW_BLOCK), :] for s in starts], axis=0)
            vt = vv.astype(F32).T.astype(BF16)
            for kvh in range(SW_KV_HEADS):
                ks = slice(kvh * LANES, (kvh + 1) * LANES)
                kk = jnp.concatenate([k_ref[pl.ds(s, SW_BLOCK), ks] for s in starts], axis=0)
                parts, sinks = [], []
                for j in range(SW_GROUP):
                    h = kvh * SW_GROUP + j
                    qc = q[:, (h // 2) * LANES:(h // 2 + 1) * LANES]
                    keep = first if h % 2 == 0 else jnp.logical_not(first)
                    parts.append(jnp.where(keep, qc, jnp.zeros_like(qc)))
                    sinks.append(jnp.full((1, SW_BLOCK), sink_ref[h] * LOG2E, F32))
                vt_ones = jnp.concatenate([vt[kvh * HEAD_DIM:(kvh + 1) * HEAD_DIM], ones], axis=0)
                chains.append(dict(k=kk, vt=vt_ones, q=jnp.concatenate(parts, axis=0),
                                   sink=jnp.concatenate(sinks, axis=1), mask=mask, qrows=qrows, kvh=kvh))
        for c in chains:
            s = lax.dot_general(c["k"], c["q"], (((1,), (1,)), ((), ())), preferred_element_type=F32)
            prev_mask, next_mask = c["mask"][:SW_BLOCK], c["mask"][2 * SW_BLOCK:]
            cols = []
            for j in range(SW_GROUP):
                sj = s[:, j * SW_BLOCK:(j + 1) * SW_BLOCK]
                cols.append(jnp.concatenate([sj[:SW_BLOCK] + prev_mask, sj[SW_BLOCK:2 * SW_BLOCK],
                                             sj[2 * SW_BLOCK:] + next_mask], axis=0))
            c["s"] = jnp.concatenate(cols, axis=1)
        for c in chains:
            c["m"] = jnp.maximum(jnp.max(c["s"], axis=0, keepdims=True), c["sink"])
        for c in chains:
            c["p"] = jnp.exp2(c["s"] - c["m"]).astype(BF16)
        for c in chains:
            o = jnp.dot(c["vt"], c["p"], preferred_element_type=F32)
            denom = o[HEAD_DIM:HEAD_DIM + 1] + jnp.exp2(c["sink"] - c["m"])
            c["o"] = o[:HEAD_DIM] / denom
        for c in chains:
            for pair in range(SW_GROUP // 2):
                even = c["o"][:, (2 * pair) * SW_BLOCK:(2 * pair + 1) * SW_BLOCK]
                odd = c["o"][:, (2 * pair + 1) * SW_BLOCK:(2 * pair + 2) * SW_BLOCK]
                col = (c["kvh"] * (SW_GROUP // 2) + pair) * LANES
                o_ref[c["qrows"], col:col + LANES] = jnp.concatenate([even, odd], axis=0).T.astype(BF16)
        return carry

    lax.fori_loop(0, nb // SW_BLOCKS_PER_STEP, blocks, 0)


def _sw_attention(q, k2, v, sink):
    b = q.shape[0]
    masks = _sw_masks()
    seq = lambda w: pl.BlockSpec((None, SEQ, w), lambda i: (i, 0, 0))
    return pl.pallas_call(
        _sw_kernel, grid=(b,),
        in_specs=[pl.BlockSpec(memory_space=pltpu.SMEM), seq(SW_WIDTH), seq(2 * SW_KV_WIDTH),
                  seq(SW_KV_WIDTH), pl.BlockSpec(masks.shape, lambda i: (0, 0, 0))],
        out_specs=seq(SW_WIDTH), out_shape=jax.ShapeDtypeStruct((b, SEQ, SW_WIDTH), BF16),
        compiler_params=_params("parallel"), name="sw_attention",
    )(sink.astype(F32), q, k2, v, masks)


def _merge_kernel(x_ref, yna_ref, ysw_ref, gate_ref, wna_ref, wsw_ref, wout_ref, g_ref, b_ref, o_ref):
    part = x_ref.shape[0] // MERGE_PARTS
    parts = [slice(p * part, (p + 1) * part) for p in range(MERGE_PARTS)]
    dot = functools.partial(jnp.dot, preferred_element_type=F32)
    branches = [(dot(yna_ref[r, :], wna_ref[...]), dot(ysw_ref[r, :], wsw_ref[...])) for r in parts]
    mixed = [(gate_ref[r, :D_MODEL].astype(F32) * a + gate_ref[r, D_MODEL:].astype(F32) * s).astype(BF16)
             for r, (a, s) in zip(parts, branches)]
    summed = [DEEPNORM_ALPHA * x_ref[r, :] + dot(m, wout_ref[...]) for r, m in zip(parts, mixed)]
    for r, z in zip(parts, summed):
        o_ref[r, :] = _layer_norm(z, g_ref[...], b_ref[...])


def _merge(x, y_na, y_sw, gates, w_na, w_sw, w_out, g, b):
    n = x.shape[0]
    row = lambda w: pl.BlockSpec((PROJ_ROW_TILE, w), lambda i: (i, 0))
    full = lambda a: pl.BlockSpec(a.shape, lambda i: (0,) * a.ndim)
    g2, b2 = g.reshape(1, -1), b.reshape(1, -1)
    return pl.pallas_call(
        _merge_kernel, grid=(n // PROJ_ROW_TILE,),
        in_specs=[row(D_MODEL), row(NA_WIDTH), row(SW_WIDTH), row(2 * D_MODEL),
                  full(w_na), full(w_sw), full(w_out), full(g2), full(b2)],
        out_specs=row(D_MODEL), out_shape=jax.ShapeDtypeStruct((n, D_MODEL), F32),
        compiler_params=_params("parallel"), name="merge_ln1",
    )(x, y_na, y_sw, gates, w_na, w_sw, w_out, g2, b2)


def _router_kernel(x_ref, w_ref, cw_ref, pos_ref, post_ref, cnt_ref, zero_ref):
    zero_ref[...] = jnp.zeros_like(zero_ref)

    blocks = [slice(p * ROW_TILE, (p + 1) * ROW_TILE) for p in range(ROUTER_BLOCKS)]
    dot = functools.partial(jnp.dot, preferred_element_type=F32)
    w = w_ref[...]
    wh = w.astype(BF16)
    wl = (w - wh.astype(F32)).astype(BF16)
    xs = [x_ref[r, :] for r in blocks]
    xh = [x.astype(BF16) for x in xs]
    xl = [(x - h.astype(F32)).astype(BF16) for x, h in zip(xs, xh)]
    logits = [dot(h, wh) + dot(l, wh) + dot(h, wl) for h, l in zip(xh, xl)]
    lane = lax.broadcasted_iota(jnp.int32, (ROW_TILE, LANES), 1).astype(F32)
    lg = [jnp.where(lane < N_EXPERTS, t, -jnp.inf) for t in logits]
    m1 = [jnp.max(t, axis=-1, keepdims=True) for t in lg]
    i1 = [jnp.min(jnp.where(t == m, lane, float(LANES)), axis=-1, keepdims=True) for t, m in zip(lg, m1)]
    lg2 = [jnp.where(lane == i, -jnp.inf, t) for t, i in zip(lg, i1)]
    m2 = [jnp.max(t, axis=-1, keepdims=True) for t in lg2]
    i2 = [jnp.min(jnp.where(t == m, lane, float(LANES)), axis=-1, keepdims=True) for t, m in zip(lg2, m2)]
    e2 = [jnp.exp(b - a) for a, b in zip(m1, m2)]
    cw = [jnp.where(lane == a, 1.0 / (1.0 + e), 0.0) + jnp.where(lane == b, e / (1.0 + e), 0.0)
          for a, b, e in zip(i1, i2, e2)]

    t = ROW_TILE
    tri = jnp.where(lax.broadcasted_iota(jnp.int32, (t, t), 0) <= lax.broadcasted_iota(jnp.int32, (t, t), 1),
                    1.0, 0.0).astype(BF16)
    sel = [jnp.where(c.T[:N_EXPERTS] > 0.0, 1.0, 0.0) for c in cw]
    incl = [dot(s.astype(BF16), tri) for s in sel]
    post = [jnp.where(s > 0.0, n - 1.0, -1.0) for s, n in zip(sel, incl)]
    unrouted = jnp.full((LANES - N_EXPERTS, t), -1.0, F32)
    for p, r in enumerate(blocks):
        cw_ref[r, :] = cw[p]
        post_ref[:, r] = post[p]
        pos_ref[r, :] = jnp.concatenate([post[p], unrouted], axis=0).T
        cnt_ref[p * N_EXPERTS:(p + 1) * N_EXPERTS, :] = jnp.broadcast_to(incl[p][:, t - 1:t], (N_EXPERTS, LANES))


def _router(x, w_router, n_rows):
    n = x.shape[0]
    nblk = n // ROW_TILE
    step = ROUTER_BLOCKS * ROW_TILE
    steps = n // step
    assert n_rows % (steps * SEG_ALIGN) == 0
    w = jnp.zeros((D_MODEL, LANES), F32).at[:, :N_EXPERTS].set(w_router)
    tok = pl.BlockSpec((step, LANES), lambda i: (i, 0))
    cw, pos, post, cnt, zeros = pl.pallas_call(
        _router_kernel, grid=(steps,),
        in_specs=[pl.BlockSpec((step, D_MODEL), lambda i: (i, 0)),
                  pl.BlockSpec((D_MODEL, LANES), lambda i: (0, 0))],
        out_specs=[tok, tok, pl.BlockSpec((N_EXPERTS, step), lambda i: (0, i)),
                   pl.BlockSpec((ROUTER_BLOCKS * N_EXPERTS, LANES), lambda i: (i, 0)),
                   pl.BlockSpec((n_rows // steps, D_MODEL), lambda i: (i, 0))],
        out_shape=[jax.ShapeDtypeStruct((n, LANES), F32), jax.ShapeDtypeStruct((n, LANES), F32),
                   jax.ShapeDtypeStruct((N_EXPERTS, n), F32),
                   jax.ShapeDtypeStruct((nblk * N_EXPERTS, LANES), F32),
                   jax.ShapeDtypeStruct((n_rows, D_MODEL), BF16)],
        compiler_params=_params("parallel"), name="router_top2",
    )(x, w)
    return cw, pos, post, cnt[:, 0].astype(jnp.int32).reshape(nblk, N_EXPERTS), zeros


def _moe_layout(cnt, n_tiles):
    seg = (cnt + SEG_ALIGN - 1) // SEG_ALIGN * SEG_ALIGN
    rows_e = jnp.sum(seg, axis=0)
    tiles_e = (rows_e + MOE_CHUNK + MOE_ROW_TILE - 1) // MOE_ROW_TILE
    tile_end = jnp.cumsum(tiles_e)
    tile_off = tile_end - tiles_e
    seg_off = (tile_off * MOE_ROW_TILE)[None, :] + jnp.cumsum(seg, axis=0) - seg
    tile = jnp.arange(n_tiles, dtype=jnp.int32)
    tile_expert = jnp.minimum(jnp.sum(tile[:, None] >= tile_end[None, :], axis=1), N_EXPERTS - 1)
    occupied = jnp.clip(rows_e[tile_expert] - (tile - tile_off[tile_expert]) * MOE_ROW_TILE, 0, MOE_ROW_TILE)
    as_i32 = lambda t: t.astype(jnp.int32)
    return as_i32(seg_off).reshape(-1), as_i32(tile_expert), as_i32(occupied)


def _chunk_copies(off_ref, cnt_ref, block, hbm_ref, vmem_ref, sems, to_hbm):
    slot = block % 2
    out = []
    for e in range(N_EXPERTS):
        c = cnt_ref[block * N_EXPERTS + e]
        off = pl.multiple_of(off_ref[block * N_EXPERTS + e], SEG_ALIGN)
        for k in range(ROW_TILE // MOE_CHUNK):
            idx = k * N_EXPERTS + e
            rows = hbm_ref.at[pl.ds(off + k * MOE_CHUNK, MOE_CHUNK)]
            staged = vmem_ref.at[slot, idx]
            src, dst = (staged, rows) if to_hbm else (rows, staged)
            out.append((c > k * MOE_CHUNK, e, k, idx, pltpu.make_async_copy(src, dst, sems.at[slot, idx])))
    return out


def _dispatch_kernel(off_ref, cnt_ref, x_ref, post_ref, xs_in_ref, xs_ref, stage_ref, sems):
    del xs_in_ref
    b, nblk = pl.program_id(0), pl.num_programs(0)
    copies_of = lambda blk: _chunk_copies(off_ref, cnt_ref, blk, xs_ref, stage_ref, sems, to_hbm=True)

    def wait_all(blk):
        for pred, _, _, _, copy in copies_of(blk):
            pl.when(pred)(copy.wait)

    xb = x_ref[...].astype(BF16)
    slot_row = lax.broadcasted_iota(jnp.int32, (MOE_CHUNK, ROW_TILE), 0).astype(F32)

    def gather_matrix(e, k):
        return jnp.where(post_ref[e:e + 1, :] == slot_row + float(k * MOE_CHUNK), 1.0, 0.0).astype(BF16)

    firsts = jnp.concatenate([gather_matrix(e, 0) for e in range(N_EXPERTS)], axis=0)
    stage_ref[b % 2, 0:N_EXPERTS] = jnp.dot(firsts, xb, preferred_element_type=F32).astype(BF16).reshape(
        N_EXPERTS, MOE_CHUNK, D_MODEL)
    copies = copies_of(b)
    for pred, e, k, idx, _ in copies:
        if k > 0:
            @pl.when(pred)
            def _(e=e, k=k, idx=idx):
                stage_ref[b % 2, idx] = jnp.dot(gather_matrix(e, k), xb, preferred_element_type=F32).astype(BF16)
    pl.when(b >= 1)(lambda: wait_all(b - 1))
    for pred, _, _, _, copy in copies:
        pl.when(pred)(copy.start)
    pl.when(b == nblk - 1)(lambda: wait_all(b))


def _dispatch(x, post, seg_off, cnt, zeros):
    n = x.shape[0]
    n_chunks = N_EXPERTS * (ROW_TILE // MOE_CHUNK)
    grid_spec = pltpu.PrefetchScalarGridSpec(
        num_scalar_prefetch=2, grid=(n // ROW_TILE,),
        in_specs=[pl.BlockSpec((ROW_TILE, D_MODEL), lambda i, *_: (i, 0)),
                  pl.BlockSpec((N_EXPERTS, ROW_TILE), lambda i, *_: (0, i)),
                  pl.BlockSpec(memory_space=pl.ANY)],
        out_specs=pl.BlockSpec(memory_space=pl.ANY),
        scratch_shapes=[pltpu.VMEM((2, n_chunks, MOE_CHUNK, D_MODEL), BF16),
                        pltpu.SemaphoreType.DMA((2, n_chunks))])
    return pl.pallas_call(
        _dispatch_kernel, grid_spec=grid_spec,
        out_shape=jax.ShapeDtypeStruct(zeros.shape, BF16),
        input_output_aliases={4: 0},
        compiler_params=_params("arbitrary"), name="moe_dispatch",
    )(seg_off, cnt.reshape(-1), x, post, zeros)


def _swiglu_partial(xb, wg, wu, wd):
    hg = jnp.dot(xb, wg, preferred_element_type=F32)
    hu = jnp.dot(xb, wu, preferred_element_type=F32)
    h = hg * _sigmoid(hg) * hu
    return jnp.dot(h.astype(BF16), wd, preferred_element_type=F32)


def _moe_ffn_kernel(expert_ref, occ_ref, x_ref, wg_ref, wu_ref, wd_ref, o_ref, acc_ref):
    del expert_ref
    i, j = pl.program_id(0), pl.program_id(1)
    occupied = occ_ref[i]

    @pl.when(j == 0)
    def _():
        acc_ref[...] = jnp.zeros_like(acc_ref)

    def weights():
        return wg_ref[...].astype(BF16), wu_ref[...].astype(BF16), wd_ref[...].astype(BF16)

    chunks = (occupied + MOE_COMPUTE_ROWS - 1) // MOE_COMPUTE_ROWS
    for n in range(1, MOE_ROW_TILE // MOE_COMPUTE_ROWS + 1):
        rows = slice(0, n * MOE_COMPUTE_ROWS)

        @pl.when(chunks == n)
        def _(rows=rows):
            acc_ref[rows, :] += _swiglu_partial(x_ref[rows, :], *weights())

    @pl.when(j == pl.num_programs(1) - 1)
    def _():
        o_ref[...] = acc_ref[...].astype(BF16)


def _moe_ffn(xs, tile_expert, occupied, w_gate, w_up, w_down):
    n_rows = xs.shape[0]
    d_ff = w_gate.shape[-1]
    row = pl.BlockSpec((MOE_ROW_TILE, D_MODEL), lambda i, j, *_: (i, 0))
    up = pl.BlockSpec((None, D_MODEL, MOE_FF_TILE), lambda i, j, ex, occ: (ex[i], 0, j))
    down = pl.BlockSpec((None, MOE_FF_TILE, D_MODEL), lambda i, j, ex, occ: (ex[i], j, 0))
    grid_spec = pltpu.PrefetchScalarGridSpec(
        num_scalar_prefetch=2, grid=(n_rows // MOE_ROW_TILE, d_ff // MOE_FF_TILE),
        in_specs=[row, up, up, down], out_specs=row,
        scratch_shapes=[pltpu.VMEM((MOE_ROW_TILE, D_MODEL), F32)])
    return pl.pallas_call(
        _moe_ffn_kernel, grid_spec=grid_spec,
        out_shape=jax.ShapeDtypeStruct((n_rows, D_MODEL), BF16),
        compiler_params=_params("parallel", "arbitrary"), name="moe_swiglu",
    )(tile_expert, occupied, xs, w_gate, w_up, w_down)


def _ffn_kernel(x_ref, wg_ref, wu_ref, wd_ref, g_ref, b_ref, o_ref):
    x = x_ref[...]
    y = _swiglu_partial(x.astype(BF16), wg_ref[...], wu_ref[...], wd_ref[...])
    o_ref[...] = _layer_norm(DEEPNORM_ALPHA * x + y, g_ref[...], b_ref[...])


def _ffn(x, w_gate, w_up, w_down, g, b):
    n = x.shape[0]
    row = pl.BlockSpec((PROJ_ROW_TILE, D_MODEL), lambda i: (i, 0))
    const = lambda a: pl.BlockSpec(a.shape, lambda i: (0,) * a.ndim, pipeline_mode=pl.Buffered(1))
    g2, b2 = g.reshape(1, -1), b.reshape(1, -1)
    return pl.pallas_call(
        _ffn_kernel, grid=(n // PROJ_ROW_TILE,),
        in_specs=[row, const(w_gate), const(w_up), const(w_down), const(g2), const(b2)], out_specs=row,
        out_shape=jax.ShapeDtypeStruct((n, D_MODEL), F32),
        compiler_params=_params("parallel"), name="swiglu_ln2",
    )(x, w_gate, w_up, w_down, g2, b2)


def _combine_kernel(off_ref, cnt_ref, x_ref, cw_ref, pos_ref, g_ref, b_ref, ys_ref, o_ref,
                    buf_ref, acc_ref, sems):
    b, nblk = pl.program_id(0), pl.num_programs(0)
    copies_of = lambda blk: _chunk_copies(off_ref, cnt_ref, blk, ys_ref, buf_ref, sems, to_hbm=False)

    def start_all(blk):
        for pred, _, k, _, copy in copies_of(blk):
            if k == 0:
                copy.start()
            else:
                pl.when(pred)(copy.start)

    pl.when(b == 0)(lambda: start_all(b))
    pl.when(b + 1 < nblk)(lambda: start_all(b + 1))

    lane = lax.broadcasted_iota(jnp.int32, (ROW_TILE, LANES), 1)
    slot_col = lax.broadcasted_iota(jnp.int32, (ROW_TILE, MOE_CHUNK), 1).astype(F32)
    column = lambda ref, e: jnp.sum(jnp.where(lane == e, ref[...], 0.0), axis=-1, keepdims=True)
    weight = [column(cw_ref, e) for e in range(N_EXPERTS)]
    pos = [column(pos_ref, e) for e in range(N_EXPERTS)]

    def scatter_matrix(e, k):
        return jnp.where(pos[e] == slot_col + float(k * MOE_CHUNK), weight[e], 0.0).astype(BF16)

    first = jnp.concatenate([scatter_matrix(e, 0) for e in range(N_EXPERTS)], axis=1)
    copies = copies_of(b)
    for _, _, k, _, copy in copies:
        if k == 0:
            copy.wait()
    firsts = buf_ref[b % 2, 0:N_EXPERTS].reshape(N_EXPERTS * MOE_CHUNK, D_MODEL)
    acc_ref[...] = jnp.dot(first, firsts, preferred_element_type=F32)
    for pred, e, k, idx, copy in copies:
        if k > 0:
            @pl.when(pred)
            def _(e=e, k=k, idx=idx, copy=copy):
                copy.wait()
                acc_ref[...] += jnp.dot(scatter_matrix(e, k), buf_ref[b % 2, idx], preferred_element_type=F32)

    o_ref[...] = _layer_norm(DEEPNORM_ALPHA * x_ref[...] + acc_ref[...], g_ref[...], b_ref[...])


def _combine(x, ys, cw, pos, seg_off, cnt, g, b):
    n = x.shape[0]
    n_chunks = N_EXPERTS * (ROW_TILE // MOE_CHUNK)
    tok = lambda w: pl.BlockSpec((ROW_TILE, w), lambda i, *_: (i, 0))
    vec = pl.BlockSpec((1, D_MODEL), lambda i, *_: (0, 0))
    grid_spec = pltpu.PrefetchScalarGridSpec(
        num_scalar_prefetch=2, grid=(n // ROW_TILE,),
        in_specs=[tok(D_MODEL), tok(LANES), tok(LANES), vec, vec, pl.BlockSpec(memory_space=pl.ANY)],
        out_specs=tok(D_MODEL),
        scratch_shapes=[pltpu.VMEM((2, n_chunks, MOE_CHUNK, D_MODEL), BF16),
                        pltpu.VMEM((ROW_TILE, D_MODEL), F32),
                        pltpu.SemaphoreType.DMA((2, n_chunks))])
    return pl.pallas_call(
        _combine_kernel, grid_spec=grid_spec,
        out_shape=jax.ShapeDtypeStruct((n, D_MODEL), F32),
        compiler_params=_params("arbitrary"), name="moe_combine_ln2",
    )(seg_off, cnt.reshape(-1), x, cw, pos, g.reshape(1, -1), b.reshape(1, -1), ys)


def _moe(x, w_router, w_gate, w_up, w_down, g, b):
    n = x.shape[0]
    nblk = n // ROW_TILE
    max_rows = 2 * n + nblk * N_EXPERTS * (SEG_ALIGN - 1) + N_EXPERTS * (MOE_CHUNK + MOE_ROW_TILE - 1)
    n_tiles = max_rows // MOE_ROW_TILE
    cw, pos, post, cnt, zeros = _router(x, w_router, n_tiles * MOE_ROW_TILE)
    seg_off, tile_expert, occupied = _moe_layout(cnt, n_tiles)
    xs = _dispatch(x, post, seg_off, cnt, zeros)
    ys = _moe_ffn(xs, tile_expert, occupied, w_gate, w_up, w_down)
    return _combine(x, ys, cw, pos, seg_off, cnt, g, b)


def kernel(x, emb_ln_g, emb_ln_b, w_in, b_gate, na_rpb, sw_sink, w_branch_na, w_branch_sw, w_out,
           ln1_g, ln1_b, ffn_w_gate, ffn_w_up, ffn_w_down, moe_router, moe_w_gate, moe_w_up,
           moe_w_down, ln2_g, ln2_b):
    batch, seq, d = x.shape
    assert (seq, d) == (SEQ, D_MODEL)
    n = batch * seq
    tables = _rotary_tables()
    na_bias = _na_bias_tables(na_rpb)
    h = x.reshape(n, d)
    for layer in range(DEPTH):
        h, q_na, k_na, v_na, q_sw, k_sw, v_sw, gates = _proj(
            h, w_in[layer].astype(BF16), b_gate[layer], tables,
            embed_ln=(emb_ln_g, emb_ln_b) if layer == 0 else None)
        seq3 = lambda t: t.reshape(batch, seq, t.shape[-1])
        y_na = _na_attention(seq3(q_na), seq3(k_na), seq3(v_na), na_bias, layer)
        y_sw = _sw_attention(seq3(q_sw), seq3(k_sw), seq3(v_sw), sw_sink[layer])
        h = _merge(h, y_na.reshape(n, -1), y_sw.reshape(n, -1), gates,
                   w_branch_na[layer].astype(BF16), w_branch_sw[layer].astype(BF16),
                   w_out[layer].astype(BF16), ln1_g[layer], ln1_b[layer])
        i = layer // 2
        if layer % 2 == 0:
            h = _ffn(h, ffn_w_gate[i].astype(BF16), ffn_w_up[i].astype(BF16),
                     ffn_w_down[i].astype(BF16), ln2_g[layer], ln2_b[layer])
        else:
            h = _moe(h, moe_router[i], moe_w_gate[i], moe_w_up[i], moe_w_down[i],
                     ln2_g[layer], ln2_b[layer])
    return h.reshape(batch, seq, d)
```
